```python
import jax
import jax.numpy as jnp
from jax import lax
import numpy as np

D_MODEL = 2048
BATCH = 8
SEQ = 4096
DEPTH = 4

D_MIX = D_MODEL
A_WIDTH = D_MIX // 4
A_HEADS = 8
A_HEAD_DIM = A_WIDTH // A_HEADS
IDX_HEADS = 8
IDX_DIM = 64
TOPK_MAX = 256
Q_BLOCK = 128
B_WIDTH = D_MIX // 4
B_GROUPS = 8
B_GROUP_DIM = B_WIDTH // B_GROUPS
B_CHUNK = 128
C_WIDTH = D_MIX // 2
C_HEADS = 8
C_VAL_DIM = C_WIDTH // C_HEADS
C_KEY_DIM = C_VAL_DIM // 2
C_KWIDTH = C_HEADS * C_KEY_DIM
C_GATE_RANK = 16
C_GATE_TAU = 16.0
C_CHUNK = 64
ROPE_THETA = 500000.0
ROPE_FRACTION = 4
ROPE_DIM = A_HEAD_DIM // ROPE_FRACTION
D_FF = 5504
N_EXPERTS = 8
TOP_K_EXPERTS = 2
N_DENSE = (DEPTH + 1) // 2
N_MOE = DEPTH // 2
N_ADA = 6
EPS = 1e-6

IN_SIZES = (A_WIDTH, A_HEAD_DIM, A_HEAD_DIM, IDX_HEADS * IDX_DIM, IDX_DIM, IDX_HEADS,
            B_WIDTH, B_WIDTH,
            C_KWIDTH, C_KWIDTH, C_WIDTH, C_GATE_RANK, C_WIDTH)
N_IN = sum(IN_SIZES)

kernel_name = 'hybrid_dsa_sgu_gla_adaln_moe'


def split_points():
    pts, acc = [], 0
    for s in IN_SIZES[:-1]:
        acc += s
        pts.append(acc)
    return pts


def _rms(x):
    xf = x.astype(jnp.float32)
    return xf * lax.rsqrt(jnp.mean(xf * xf, axis=-1, keepdims=True) + EPS)


def rms_norm(x, g):
    return (_rms(x) * g.astype(jnp.float32)).astype(x.dtype)


def layer_norm(x, g):
    xf = x.astype(jnp.float32)
    xc = xf - jnp.mean(xf, axis=-1, keepdims=True)
    y = xc * lax.rsqrt(jnp.mean(xc * xc, axis=-1, keepdims=True) + EPS)
    return (y * g.astype(jnp.float32)).astype(x.dtype)


def rope_tables(seq_len):
    inv_freq = ROPE_THETA ** (-jnp.arange(0, ROPE_DIM, 2, dtype=jnp.float32) / ROPE_DIM)
    ang = jnp.arange(seq_len, dtype=jnp.float32)[:, None] * inv_freq[None, :]
    return jnp.cos(ang)[None, :, None, :], jnp.sin(ang)[None, :, None, :]


def partial_rope(x, cos, sin):
    half = ROPE_DIM // 2
    x1 = x[..., :half].astype(jnp.float32)
    x2 = x[..., half:ROPE_DIM].astype(jnp.float32)
    rot = jnp.concatenate([x1 * cos - x2 * sin, x2 * cos + x1 * sin], axis=-1).astype(x.dtype)
    return jnp.concatenate([rot, x[..., ROPE_DIM:]], axis=-1)


def dsa_attention(q, k, v, iq, ik, iw, cos, sin, g_q, g_k):
    bsz, seq = q.shape[:2]
    out_dtype = q.dtype
    topk = min(TOPK_MAX, seq // 4)
    n_blk = seq // Q_BLOCK
    q = partial_rope(rms_norm(q, g_q), cos, sin)
    k = partial_rope(rms_norm(k, g_k)[:, :, None, :], cos, sin)[:, :, 0, :]
    iq = partial_rope(iq, cos, sin).astype(jnp.float32) * (IDX_DIM ** -0.5)
    ik = partial_rope(ik[:, :, None, :], cos, sin)[:, :, 0, :].astype(jnp.float32)
    iw = iw.astype(jnp.float32) * (IDX_HEADS ** -0.5)
    key_pos = jnp.arange(seq)
    gather = jax.vmap(lambda table, idx: table[idx])

    def blocks(a):
        return a.reshape(bsz, n_blk, Q_BLOCK, *a.shape[2:]).swapaxes(0, 1)

    def one_block(args):
        qb, iqb, iwb, blk = args
        q_pos = blk * Q_BLOCK + jnp.arange(Q_BLOCK)
        idx_logits = jnp.einsum('bqhd,bsd->bqhs', iqb, ik)
        score = jnp.einsum('bqhs,bqh->bqs', jax.nn.relu(idx_logits), iwb)
        score = jnp.where(key_pos[None, None, :] <= q_pos[None, :, None], score, -jnp.inf)
        _, sel = lax.top_k(score, topk)
        valid = sel <= q_pos[None, :, None]
        k_sel = gather(k, sel).astype(jnp.float32)
        v_sel = gather(v, sel).astype(jnp.float32)
        s = jnp.einsum('bqhd,bqkd->bqhk', qb.astype(jnp.float32), k_sel) * (A_HEAD_DIM ** -0.5)
        s = jnp.where(valid[:, :, None, :], s, -jnp.inf)
        p = jax.nn.softmax(s, axis=-1)
        return jnp.einsum('bqhk,bqkd->bqhd', p, v_sel).astype(out_dtype)

    out = lax.map(one_block, (blocks(q), blocks(iq), blocks(iw), jnp.arange(n_blk)))
    return out.swapaxes(0, 1).reshape(bsz, seq, A_HEADS, A_HEAD_DIM)


def spatial_gating(u, v, g_v, w_s, b_s):
    bsz, seq, _ = u.shape
    n_chk = seq // B_CHUNK
    u = jax.nn.gelu(u)
    v = layer_norm(jax.nn.gelu(v), g_v).reshape(bsz, n_chk, B_CHUNK, B_GROUPS, B_GROUP_DIM)
    causal = jnp.tril(jnp.ones((B_CHUNK, B_CHUNK), dtype=bool))[None]
    w = jnp.where(causal, w_s, jnp.zeros_like(w_s))
    mixed = jnp.einsum('gts,bnsgd->bntgd', w, v) + b_s.T[None, None, :, :, None]
    return u * mixed.reshape(bsz, seq, B_WIDTH)


def gated_linear_attention(q, k, v, a_low, w_a2, b_a):
    bsz, seq = q.shape[:2]
    n_chk = seq // C_CHUNK
    f32 = jnp.float32
    shp_k = (bsz, n_chk, C_CHUNK, C_HEADS, C_KEY_DIM)
    q = q.astype(f32).reshape(shp_k) * (C_KEY_DIM ** -0.5)
    k = k.astype(f32).reshape(shp_k)
    v = v.astype(f32).reshape(bsz, n_chk, C_CHUNK, C_HEADS, C_VAL_DIM)
    log_a = jax.nn.log_sigmoid((a_low @ w_a2 + b_a).astype(f32)) / C_GATE_TAU
    b = jnp.cumsum(log_a.reshape(shp_k), axis=2)
    b_last = b[:, :, -1]
    q_dec = q * jnp.exp(b)
    causal = jnp.tril(jnp.ones((C_CHUNK, C_CHUNK), dtype=bool))
    att = jnp.einsum('bnthd,bnshd->bnhts', q_dec, k * jnp.exp(-b))
    att = jnp.where(causal, att, 0.0)
    o_intra = jnp.einsum('bnhts,bnshv->bnthv', att, v)
    chunk_state = jnp.einsum('bnshd,bnshv->bnhdv', k * jnp.exp(b_last[:, :, None] - b), v)

    def step(state, inp):
        decay, upd = inp
        return decay[..., None] * state + upd, state

    s0 = jnp.zeros((bsz, C_HEADS, C_KEY_DIM, C_VAL_DIM), f32)
    _, s_prev = lax.scan(step, s0, (jnp.exp(b_last).swapaxes(0, 1), chunk_state.swapaxes(0, 1)))
    o_inter = jnp.einsum('bnthd,bnhdv->bnthv', q_dec, s_prev.swapaxes(0, 1))
    return (o_intra + o_inter).reshape(bsz, seq, C_HEADS, C_VAL_DIM)


def hybrid_token_mixer(h, cos, sin, w_in, g_q, g_k, g_v_b, w_s, b_s, w_a2, b_a, g_out, w_out):
    bsz, seq, _ = h.shape
    aq, ak, av, iq, ik, iw, bu, bv, cq, ck, cv, ca, cr = jnp.split(h @ w_in, split_points(), axis=-1)
    o_a = dsa_attention(aq.reshape(bsz, seq, A_HEADS, A_HEAD_DIM), ak, av,
                        iq.reshape(bsz, seq, IDX_HEADS, IDX_DIM), ik, iw, cos, sin, g_q, g_k)
    o_b = spatial_gating(bu, bv, g_v_b, w_s, b_s).reshape(bsz, seq, B_GROUPS, B_GROUP_DIM)
    o_c = gated_linear_attention(cq, ck, cv, ca, w_a2, b_a)
    merged = jnp.concatenate([
        _rms(o_a).reshape(bsz, seq, A_WIDTH),
        _rms(o_b).reshape(bsz, seq, B_WIDTH),
        _rms(o_c).reshape(bsz, seq, C_WIDTH) * jax.nn.silu(cr.astype(jnp.float32)),
    ], axis=-1) * g_out.astype(jnp.float32)
    return merged.astype(h.dtype) @ w_out


def swiglu(h, w_gate, w_up, w_down):
    return (jax.nn.silu(h @ w_gate) * (h @ w_up)) @ w_down


def moe_swiglu(h, w_router, w_e_gate, w_e_up, w_e_down):
    bsz, seq, dm = h.shape
    hf = h.reshape(-1, dm)
    logits = (hf @ w_router).astype(jnp.float32)
    top_val, top_idx = lax.top_k(logits, TOP_K_EXPERTS)
    top_w = jax.nn.softmax(top_val, axis=-1)
    gates = jnp.sum(jax.nn.one_hot(top_idx, N_EXPERTS, dtype=jnp.float32) * top_w[..., None], axis=1)
    gates = gates.astype(hf.dtype)
    y = jnp.zeros_like(hf)
    for e in range(N_EXPERTS):
        y = y + gates[:, e, None] * swiglu(hf, w_e_gate[e], w_e_up[e], w_e_down[e])
    return y.reshape(bsz, seq, dm)


def setup_inputs(seed: int = 0) -> dict:
    key = jax.random.key(seed)
    ks = jax.random.split(key, 24)
    f32 = jnp.float32

    def nrm(k, shape, scale):
        return jax.random.normal(k, shape, f32) * scale

    def gain(k, shape, noise=0.01):
        return 1.0 + noise * jax.random.normal(k, shape, f32)

    return {
        'x': nrm(ks[0], (BATCH, SEQ, D_MODEL), 1.0),
        'c': nrm(ks[1], (BATCH, D_MODEL), 1.0),
        'w_ada': nrm(ks[2], (DEPTH, D_MODEL, N_ADA * D_MODEL), 0.5 * D_MODEL ** -0.5),
        'b_ada': nrm(ks[3], (DEPTH, N_ADA * D_MODEL), 0.01),
        'g_norm1': gain(ks[4], (DEPTH, D_MODEL)),
        'g_norm2': gain(ks[5], (DEPTH, D_MODEL)),
        'w_in': nrm(ks[6], (DEPTH, D_MODEL, N_IN), D_MODEL ** -0.5),
        'g_q': gain(ks[7], (DEPTH, A_HEAD_DIM)),
        'g_k': gain(ks[8], (DEPTH, A_HEAD_DIM)),
        'g_v_b': gain(ks[9], (DEPTH, B_WIDTH)),
        'w_s': nrm(ks[10], (DEPTH, B_GROUPS, B_CHUNK, B_CHUNK), B_CHUNK ** -0.5),
        'b_s': gain(ks[11], (DEPTH, B_GROUPS, B_CHUNK), 0.1),
        'w_a2': nrm(ks[12], (DEPTH, C_GATE_RANK, C_KWIDTH), C_GATE_RANK ** -0.5),
        'b_a': nrm(ks[13], (DEPTH, C_KWIDTH), 0.01),
        'g_out': gain(ks[14], (DEPTH, D_MIX)),
        'w_out': nrm(ks[15], (DEPTH, D_MIX, D_MODEL), D_MIX ** -0.5),
        'w_ff_gate': nrm(ks[16], (N_DENSE, D_MODEL, D_FF), D_MODEL ** -0.5),
        'w_ff_up': nrm(ks[17], (N_DENSE, D_MODEL, D_FF), D_MODEL ** -0.5),
        'w_ff_down': nrm(ks[18], (N_DENSE, D_FF, D_MODEL), D_FF ** -0.5),
        'w_router': nrm(ks[19], (N_MOE, D_MODEL, N_EXPERTS), D_MODEL ** -0.5),
        'w_e_gate': nrm(ks[20], (N_MOE, N_EXPERTS, D_MODEL, D_FF), D_MODEL ** -0.5),
        'w_e_up': nrm(ks[21], (N_MOE, N_EXPERTS, D_MODEL, D_FF), D_MODEL ** -0.5),
        'w_e_down': nrm(ks[22], (N_MOE, N_EXPERTS, D_FF, D_MODEL), D_FF ** -0.5),
    }


def reference(x, c, w_ada, b_ada, g_norm1, g_norm2, w_in, g_q, g_k, g_v_b, w_s, b_s, w_a2, b_a,
              g_out, w_out, w_ff_gate, w_ff_up, w_ff_down, w_router, w_e_gate, w_e_up, w_e_down):
    cos, sin = rope_tables(x.shape[1])
    cond = jax.nn.silu(c)
    for layer in range(DEPTH):
        mod = (cond @ w_ada[layer] + b_ada[layer])[:, None, :]
        sh1, sc1, gt1, sh2, sc2, gt2 = jnp.split(mod, N_ADA, axis=-1)
        h = rms_norm(x, g_norm1[layer]) * (1 + sc1) + sh1
        x = x + gt1 * hybrid_token_mixer(h, cos, sin, w_in[layer], g_q[layer], g_k[layer],
                                         g_v_b[layer], w_s[layer], b_s[layer], w_a2[layer],
                                         b_a[layer], g_out[layer], w_out[layer])
        h = rms_norm(x, g_norm2[layer]) * (1 + sc2) + sh2
        j = layer // 2
        if layer % 2 == 0:
            y = swiglu(h, w_ff_gate[j], w_ff_up[j], w_ff_down[j])
        else:
            y = moe_swiglu(h, w_router[j], w_e_gate[j], w_e_up[j], w_e_down[j])
        x = x + gt2 * y
    return x
```

```python
import functools

import jax
import jax.numpy as jnp
from jax import lax
from jax.experimental import pallas as pl
from jax.experimental.pallas import tpu as pltpu

F32 = jnp.float32
BF16 = jnp.bfloat16

A_HEADS = 8
A_HEAD_DIM = 64
IDX_HEADS = 8
IDX_DIM = 64
TOPK_MAX = 256
B_GROUPS = 8
B_GROUP_DIM = 64
B_CHUNK = 128
C_HEADS = 8
C_VAL_DIM = 128
C_KEY_DIM = 64
C_GATE_RANK = 16
C_GATE_TAU = 16.0
C_CHUNK = 64
ROPE_THETA = 500000.0
ROPE_DIM = 16
N_EXPERTS = 8
N_ADA = 6
EPS = 1e-6

A_WIDTH = A_HEADS * A_HEAD_DIM
I_WIDTH = IDX_HEADS * IDX_DIM
B_WIDTH = B_GROUPS * B_GROUP_DIM
C_KWIDTH = C_HEADS * C_KEY_DIM
C_WIDTH = C_HEADS * C_VAL_DIM

LANES = 128
VMEM_LIMIT_BYTES = 56 * 1024 * 1024

COL_AQ = 0
COL_IQ = 512
COL_BU = 1024
COL_BV = 1536
COL_CQ = 2048
COL_CK = 2560
COL_CV = 3072
COL_CR = 4096
COL_AKV = 5120
COL_MISC = 5248
N_PROJ = 5376
MISC_IW = IDX_DIM
MISC_CA = IDX_DIM + IDX_HEADS

NEG_BIG = -1e30
INT_MIN = -(2 ** 31)

TM_MM = 512
TN_IN = 768
TF_FFN = 512
TN_ADA = 1024
TP_PREP = 512
Q_BLOCK = 128
T_GLA = 256


def _cparams(sem):
    return pltpu.CompilerParams(dimension_semantics=sem, vmem_limit_bytes=VMEM_LIMIT_BYTES)


def _dot(a, b):
    return jnp.dot(a, b, preferred_element_type=F32)


def _dot_nt(a, b):
    return lax.dot_general(a, b, (((1,), (1,)), ((), ())), preferred_element_type=F32)


def _dot_tn(a, b):
    return lax.dot_general(a, b, (((0,), (0,)), ((), ())), preferred_element_type=F32)


def _split(x):
    hi = x.astype(BF16)
    lo = (x - hi.astype(F32)).astype(BF16)
    return hi, lo


def _dot_exact_lhs(m_bf16, x):
    hi, lo = _split(x)
    return _dot(m_bf16, hi) + _dot(m_bf16, lo)


def _dot3(a, b):
    ah, al = _split(a)
    bh, bl = _split(b)
    return _dot(ah, bh) + (_dot(al, bh) + _dot(ah, bl))


def _norm_mod(x, g, scale, shift):
    ms = jnp.mean(x * x, axis=-1, keepdims=True)
    return (x * lax.rsqrt(ms + EPS) * g) * (1.0 + scale) + shift


def _group_ones(width, group):
    r = lax.broadcasted_iota(jnp.int32, (width, width), 0) // group
    c = lax.broadcasted_iota(jnp.int32, (width, width), 1) // group
    return (r == c).astype(BF16)


def _ada_kernel(c_ref, w_ref, b_ref, o_ref):
    c = c_ref[...]
    cond = (c * jax.nn.sigmoid(c)).astype(BF16)
    o_ref[0] = _dot(cond, w_ref[0].astype(BF16)) + b_ref[0]


def _ada_mod(c, w_ada, b_ada):
    depth, d, n6 = w_ada.shape
    bsz = c.shape[0]
    rows = 16
    c_pad = jnp.pad(c, ((0, rows - bsz), (0, 0)))
    tn = TN_ADA
    out = pl.pallas_call(
        _ada_kernel,
        grid=(depth, n6 // tn),
        in_specs=[
            pl.BlockSpec((rows, d), lambda l, j: (0, 0)),
            pl.BlockSpec((1, d, tn), lambda l, j: (l, 0, j)),
            pl.BlockSpec((1, 1, tn), lambda l, j: (l, 0, j)),
        ],
        out_specs=pl.BlockSpec((1, rows, tn), lambda l, j: (l, 0, j)),
        out_shape=jax.ShapeDtypeStruct((depth, rows, n6), F32),
        compiler_params=_cparams(("parallel", "parallel")),
        name="ada_mod",
    )(c_pad, w_ada, b_ada.reshape(depth, 1, n6))
    return out[:, :bsz].reshape(depth, bsz, N_ADA, d)


def _in_proj_kernel(x_ref, mod_ref, g_ref, w_ref, o_ref, h_scr):
    @pl.when(pl.program_id(2) == 0)
    def _():
        h = _norm_mod(x_ref[0], g_ref[...], mod_ref[0, 1:2, :], mod_ref[0, 0:1, :])
        h_scr[...] = h.astype(BF16)

    o_ref[0] = _dot(h_scr[...], w_ref[...])


def _in_proj(x, mod_l, g, w_p):
    bsz, seq, d = x.shape
    n = w_p.shape[1]
    tm, tn = min(TM_MM, seq), TN_IN
    return pl.pallas_call(
        _in_proj_kernel,
        grid=(bsz, seq // tm, n // tn),
        in_specs=[
            pl.BlockSpec((1, tm, d), lambda b, i, j: (b, i, 0)),
            pl.BlockSpec((1, N_ADA, d), lambda b, i, j: (b, 0, 0)),
            pl.BlockSpec((1, d), lambda b, i, j: (0, 0)),
            pl.BlockSpec((d, tn), lambda b, i, j: (0, j)),
        ],
        out_specs=pl.BlockSpec((1, tm, tn), lambda b, i, j: (b, i, j)),
        out_shape=jax.ShapeDtypeStruct((bsz, seq, n), F32),
        scratch_shapes=[pltpu.VMEM((tm, d), BF16)],
        compiler_params=_cparams(("parallel", "parallel", "arbitrary")),
        name="in_proj",
    )(x, mod_l, g.reshape(1, d), w_p)


def _rope(x, cos, s_up, s_dn):
    parts = []
    for j in range(x.shape[1] // LANES):
        xs = x[:, j * LANES:(j + 1) * LANES]
        parts.append(xs * cos + pltpu.roll(xs, ROPE_DIM // 2, 1) * s_up
                     + pltpu.roll(xs, LANES - ROPE_DIM // 2, 1) * s_dn)
    return parts[0] if len(parts) == 1 else jnp.concatenate(parts, axis=1)


def _dsa_prep_kernel(aq_ref, iq_ref, akv_ref, misc_ref, cos_ref, sup_ref, sdn_ref, gq_ref, gk_ref,
                     q_out, iq_out, kvik_out):
    cos, s_up, s_dn = cos_ref[...], sup_ref[...], sdn_ref[...]
    aq = aq_ref[0]
    ss = _dot_exact_lhs_right(aq * aq, _group_ones(A_WIDTH, A_HEAD_DIM))
    qn = aq * lax.rsqrt(ss * (1.0 / A_HEAD_DIM) + EPS) * gq_ref[...]
    q_out[0] = (_rope(qn, cos, s_up, s_dn) * (A_HEAD_DIM ** -0.5)).astype(BF16)
    iq_out[0] = (_rope(iq_ref[0], cos, s_up, s_dn) * (IDX_DIM ** -0.5)).astype(BF16)
    lane = lax.broadcasted_iota(jnp.int32, (1, LANES), 1)
    first = lane < A_HEAD_DIM
    akv = akv_ref[0]
    kss = jnp.sum(jnp.where(first, akv * akv, 0.0), axis=-1, keepdims=True)
    kn = akv * lax.rsqrt(kss * (1.0 / A_HEAD_DIM) + EPS) * gk_ref[...]
    kvik_out[0, :, 0:LANES] = jnp.where(first, _rope(kn, cos, s_up, s_dn), akv).astype(BF16)
    misc = misc_ref[0]
    kvik_out[0, :, LANES:2 * LANES] = jnp.where(first, _rope(misc, cos, s_up, s_dn), misc).astype(BF16)


def _dot_exact_lhs_right(x, m_bf16):
    hi, lo = _split(x)
    return _dot(hi, m_bf16) + _dot(lo, m_bf16)


def _dsa_prep(proj, tables, g_q, g_k):
    bsz, seq, _ = proj.shape
    tp = min(TP_PREP, seq)
    cos, s_up, s_dn = tables
    gq = jnp.tile(g_q, A_HEADS).reshape(1, A_WIDTH)
    gk = jnp.concatenate([g_k, jnp.ones((LANES - A_HEAD_DIM,), F32)]).reshape(1, LANES)
    tab_spec = pl.BlockSpec((tp, LANES), lambda b, i: (i, 0))
    return pl.pallas_call(
        _dsa_prep_kernel,
        grid=(bsz, seq // tp),
        in_specs=[
            pl.BlockSpec((1, tp, A_WIDTH), lambda b, i: (b, i, COL_AQ // A_WIDTH)),
            pl.BlockSpec((1, tp, I_WIDTH), lambda b, i: (b, i, COL_IQ // I_WIDTH)),
            pl.BlockSpec((1, tp, LANES), lambda b, i: (b, i, COL_AKV // LANES)),
            pl.BlockSpec((1, tp, LANES), lambda b, i: (b, i, COL_MISC // LANES)),
            tab_spec, tab_spec, tab_spec,
            pl.BlockSpec((1, A_WIDTH), lambda b, i: (0, 0)),
            pl.BlockSpec((1, LANES), lambda b, i: (0, 0)),
        ],
        out_specs=[
            pl.BlockSpec((1, tp, A_WIDTH), lambda b, i: (b, i, 0)),
            pl.BlockSpec((1, tp, I_WIDTH), lambda b, i: (b, i, 0)),
            pl.BlockSpec((1, tp, 2 * LANES), lambda b, i: (b, i, 0)),
        ],
        out_shape=[
            jax.ShapeDtypeStruct((bsz, seq, A_WIDTH), BF16),
            jax.ShapeDtypeStruct((bsz, seq, I_WIDTH), BF16),
            jax.ShapeDtypeStruct((bsz, seq, 2 * LANES), BF16),
        ],
        compiler_params=_cparams(("parallel", "parallel")),
        name="dsa_prep",
    )(proj, proj, proj, proj, cos, s_up, s_dn, gq, gk)


def _rope_tables(seq):
    half = ROPE_DIM // 2
    inv_freq = ROPE_THETA ** (-jnp.arange(0, ROPE_DIM, 2, dtype=F32) / ROPE_DIM)
    ang = jnp.arange(seq, dtype=F32)[:, None] * inv_freq[None, :]
    cos, sin = jnp.cos(ang), jnp.sin(ang)
    pad = A_HEAD_DIM - ROPE_DIM
    cos64 = jnp.concatenate([cos, cos, jnp.ones((seq, pad), F32)], axis=1)
    up64 = jnp.concatenate([jnp.zeros((seq, half), F32), sin, jnp.zeros((seq, pad), F32)], axis=1)
    dn64 = jnp.concatenate([-sin, jnp.zeros((seq, half + pad), F32)], axis=1)
    rep = LANES // A_HEAD_DIM
    return jnp.tile(cos64, (1, rep)), jnp.tile(up64, (1, rep)), jnp.tile(dn64, (1, rep))


def _dsa_kernel(q_ref, iq_ref, iw_ref, k_ref, v_ref, ik_ref, go_ref, o_ref,
                key_scr, bias_scr, m_scr, l_scr, acc_scr, *, topk):
    qb = Q_BLOCK
    blk = pl.program_id(1)
    n_chunks = blk + 1
    row = lax.broadcasted_iota(jnp.int32, (qb, qb), 0)
    col = lax.broadcasted_iota(jnp.int32, (qb, qb), 1)
    qpos = blk * qb + row
    iw = iw_ref[0] * (IDX_HEADS ** -0.5)
    iq_all = iq_ref[0].reshape(IDX_HEADS * qb, IDX_DIM)

    def score_chunk(kc, carry):
        ks = ik_ref[0, pl.ds(pl.multiple_of(kc * qb, qb), qb), :]
        logits = _dot_nt(iq_all, ks)
        acc = jnp.zeros((qb, qb), F32)
        for h in range(IDX_HEADS):
            acc = acc + jnp.maximum(logits[h * qb:(h + 1) * qb], 0.0) * iw[:, h:h + 1]
        acc = jnp.where(acc == 0.0, 0.0, acc)
        sc = jnp.where(kc * qb + col <= qpos, acc, -jnp.inf)
        bits = lax.bitcast_convert_type(sc, jnp.int32)
        key_scr[kc] = jnp.where(bits >= 0, bits, bits ^ 0x7FFFFFFF)
        return carry

    lax.fori_loop(0, n_chunks, score_chunk, 0)

    def count(pred_fn):
        def body(kc, c):
            return c + jnp.where(pred_fn(key_scr[kc]), 1.0, 0.0)
        c = lax.fori_loop(0, n_chunks, body, jnp.zeros((qb, qb), F32))
        return jnp.sum(c, axis=-1, keepdims=True)

    kf = float(topk)
    zero = jnp.zeros((qb, 1), jnp.int32)
    thr0 = jnp.where(count(lambda k: k >= zero) >= kf, zero, jnp.full((qb, 1), INT_MIN, jnp.int32))

    def bit_step(i, thr):
        cand = thr | jnp.left_shift(jnp.int32(1), 30 - i)
        return jnp.where(count(lambda k: k >= cand) >= kf, cand, thr)

    thr = lax.fori_loop(0, 31, bit_step, thr0)

    need = kf - count(lambda k: k > thr)
    upper = (row <= col).astype(BF16)

    def select_chunk(kc, carry):
        key = key_scr[kc]
        eq = key == thr
        eqf = jnp.where(eq, 1.0, 0.0)
        incl = _dot(eqf.astype(BF16), upper)
        sel = (key > thr) | (eq & (carry + incl - eqf < need))
        sel = sel & (kc * qb + col <= qpos)
        bias_scr[kc] = jnp.where(sel, 0.0, NEG_BIG)
        return carry + incl[:, qb - 1:qb]

    lax.fori_loop(0, n_chunks, select_chunk, jnp.zeros((qb, 1), F32))

    rows = A_HEADS * qb
    q_all = q_ref[0].reshape(rows, A_HEAD_DIM)
    m_scr[...] = jnp.full((rows, 1), NEG_BIG, F32)
    l_scr[...] = jnp.zeros((rows, 1), F32)
    acc_scr[...] = jnp.zeros((rows, A_HEAD_DIM), F32)

    def attend_chunk(kc, carry):
        off = pl.multiple_of(kc * qb, qb)
        s = _dot_nt(q_all, k_ref[0, pl.ds(off, qb), :])
        s = (s.reshape(A_HEADS, qb, qb) + bias_scr[kc][None]).reshape(rows, qb)
        m_prev = m_scr[...]
        m_new = jnp.maximum(m_prev, jnp.max(s, axis=-1, keepdims=True))
        p = jnp.exp(s - m_new)
        alpha = jnp.exp(m_prev - m_new)
        l_scr[...] = alpha * l_scr[...] + jnp.sum(p, axis=-1, keepdims=True)
        acc_scr[...] = alpha * acc_scr[...] + _dot(p.astype(BF16), v_ref[0, pl.ds(off, qb), :])
        m_scr[...] = m_new
        return carry

    lax.fori_loop(0, n_chunks, attend_chunk, 0)

    o = acc_scr[...] / l_scr[...]
    o = o * lax.rsqrt(jnp.mean(o * o, axis=-1, keepdims=True) + EPS)
    o_ref[0] = (o.reshape(A_HEADS, qb, A_HEAD_DIM) * go_ref[...]).astype(BF16)


def _dsa(q_hm, iq_hm, iw, k_r, v, ik_r, g_out_a):
    bsz, _, seq, _ = q_hm.shape
    qb = Q_BLOCK
    topk = min(TOPK_MAX, seq // 4)
    n_blk = seq // qb
    kv_spec = pl.BlockSpec((1, seq, A_HEAD_DIM), lambda b, i: (b, 0, 0))
    return pl.pallas_call(
        functools.partial(_dsa_kernel, topk=topk),
        grid=(bsz, n_blk),
        in_specs=[
            pl.BlockSpec((1, A_HEADS, qb, A_HEAD_DIM), lambda b, i: (b, 0, i, 0)),
            pl.BlockSpec((1, IDX_HEADS, qb, IDX_DIM), lambda b, i: (b, 0, i, 0)),
            pl.BlockSpec((1, qb, IDX_HEADS), lambda b, i: (b, i, 0)),
            kv_spec, kv_spec, kv_spec,
            pl.BlockSpec((A_HEADS, 1, A_HEAD_DIM), lambda b, i: (0, 0, 0)),
        ],
        out_specs=pl.BlockSpec((1, A_HEADS, qb, A_HEAD_DIM), lambda b, i: (b, 0, i, 0)),
        out_shape=jax.ShapeDtypeStruct((bsz, A_HEADS, seq, A_HEAD_DIM), BF16),
        scratch_shapes=[
            pltpu.VMEM((n_blk, qb, qb), jnp.int32),
            pltpu.VMEM((n_blk, qb, qb), F32),
            pltpu.VMEM((A_HEADS * qb, 1), F32),
            pltpu.VMEM((A_HEADS * qb, 1), F32),
            pltpu.VMEM((A_HEADS * qb, A_HEAD_DIM), F32),
        ],
        compiler_params=_cparams(("parallel", "arbitrary")),
        name="dsa_attention",
    )(q_hm, iq_hm, iw, k_r, v, ik_r, g_out_a.reshape(A_HEADS, 1, A_HEAD_DIM))


def _sgu_kernel(bu_ref, bv_ref, gv_ref, ws_ref, bst_ref, go_ref, o_ref):
    ch = B_CHUNK
    u = jax.nn.gelu(bu_ref[0])
    v = jax.nn.gelu(bv_ref[0])
    vc = v - jnp.mean(v, axis=-1, keepdims=True)
    vn = vc * lax.rsqrt(jnp.mean(vc * vc, axis=-1, keepdims=True) + EPS) * gv_ref[...]
    vb = vn.astype(BF16)
    row = lax.broadcasted_iota(jnp.int32, (ch, ch), 0)
    col = lax.broadcasted_iota(jnp.int32, (ch, ch), 1)
    causal = col <= row
    grp = lax.broadcasted_iota(jnp.int32, (1, B_WIDTH), 1) // B_GROUP_DIM
    bst = bst_ref[...]
    mixed = jnp.zeros((ch, B_WIDTH), F32)
    for g in range(B_GROUPS):
        w = jnp.where(causal, ws_ref[g], 0.0).astype(BF16)
        mixed = jnp.where(grp == g, _dot(w, vb) + bst[:, g:g + 1], mixed)
    o = u * mixed
    ss = _dot_exact_lhs_right(o * o, _group_ones(B_WIDTH, B_GROUP_DIM))
    o_ref[0] = (o * lax.rsqrt(ss * (1.0 / B_GROUP_DIM) + EPS) * go_ref[...]).astype(BF16)


def _sgu(proj, g_v, w_s, b_s, g_out_b):
    bsz, seq, _ = proj.shape
    ch = B_CHUNK
    return pl.pallas_call(
        _sgu_kernel,
        grid=(bsz, seq // ch),
        in_specs=[
            pl.BlockSpec((1, ch, B_WIDTH), lambda b, i: (b, i, COL_BU // B_WIDTH)),
            pl.BlockSpec((1, ch, B_WIDTH), lambda b, i: (b, i, COL_BV // B_WIDTH)),
            pl.BlockSpec((1, B_WIDTH), lambda b, i: (0, 0)),
            pl.BlockSpec((B_GROUPS, ch, ch), lambda b, i: (0, 0, 0)),
            pl.BlockSpec((ch, B_GROUPS), lambda b, i: (0, 0)),
            pl.BlockSpec((1, B_WIDTH), lambda b, i: (0, 0)),
        ],
        out_specs=pl.BlockSpec((1, ch, B_WIDTH), lambda b, i: (b, i, 0)),
        out_shape=jax.ShapeDtypeStruct((bsz, seq, B_WIDTH), BF16),
        compiler_params=_cparams(("parallel", "parallel")),
        name="spatial_gating",
    )(proj, proj, g_v.reshape(1, B_WIDTH), w_s, b_s.T, g_out_b.reshape(1, B_WIDTH))


def _log_sigmoid(z):
    return jnp.minimum(z, 0.0) - jnp.log1p(jnp.exp(-jnp.abs(z)))


def _gla_kernel(cq_ref, ck_ref, cv_ref, cr_ref, misc_ref, wa_ref, ba_ref, go_ref, o_ref, st_scr):
    tg, ch = cq_ref.shape[1], C_CHUNK

    @pl.when(pl.program_id(1) == 0)
    def _():
        st_scr[...] = jnp.zeros(st_scr.shape, F32)

    z = _dot3(misc_ref[0], wa_ref[...]) + ba_ref[...]
    log_a = _log_sigmoid(z) * (1.0 / C_GATE_TAU)
    r = lax.broadcasted_iota(jnp.int32, (tg, tg), 0)
    c = lax.broadcasted_iota(jnp.int32, (tg, tg), 1)
    same = (r // ch) == (c // ch)
    b = _dot_exact_lhs((same & (c <= r)).astype(BF16), log_a)
    b_last = _dot_exact_lhs(same.astype(BF16), log_a)
    ck = ck_ref[0]
    q_dec = cq_ref[0] * (C_KEY_DIM ** -0.5) * jnp.exp(b)
    k_neg = (ck * jnp.exp(-b)).astype(BF16)
    k_st = (ck * jnp.exp(b_last - b)).astype(BF16)
    cv = cv_ref[0].astype(BF16)

    lane_head = lax.broadcasted_iota(jnp.int32, (C_HEADS, 1, C_KWIDTH), 2) // C_KEY_DIM
    head_mask = lane_head == lax.broadcasted_iota(jnp.int32, (C_HEADS, 1, C_KWIDTH), 0)
    tril = (lax.broadcasted_iota(jnp.int32, (1, ch, ch), 2)
            <= lax.broadcasted_iota(jnp.int32, (1, ch, ch), 1))
    sr = lax.broadcasted_iota(jnp.int32, (C_KWIDTH, C_WIDTH), 0) // C_KEY_DIM
    sc = lax.broadcasted_iota(jnp.int32, (C_KWIDTH, C_WIDTH), 1) // C_VAL_DIM
    block_diag = sr == sc
    ones_cols = jnp.ones((ch, LANES), BF16)

    for n in range(tg // ch):
        rs = slice(n * ch, (n + 1) * ch)
        qd = q_dec[rs]
        vn = cv[rs]
        qm = jnp.where(head_mask, qd[None], 0.0).reshape(C_HEADS * ch, C_KWIDTH).astype(BF16)
        att = _dot_nt(qm, k_neg[rs]).reshape(C_HEADS, ch, ch)
        att = jnp.where(tril, att, 0.0).astype(BF16)
        o_intra = jnp.concatenate(
            [_dot(att[h], vn[:, h * C_VAL_DIM:(h + 1) * C_VAL_DIM]) for h in range(C_HEADS)], axis=1)
        state = st_scr[...]
        o = o_intra + _dot(qd.astype(BF16), state.astype(BF16))
        la_hi, la_lo = _split(log_a[rs])
        decay = jnp.exp(_dot_tn(la_hi, ones_cols) + _dot_tn(la_lo, ones_cols))[:, 0:1]
        st_scr[...] = jnp.where(block_diag, decay * state + _dot_tn(k_st[rs], vn), 0.0)
        parts = []
        for h in range(C_HEADS):
            oh = o[:, h * C_VAL_DIM:(h + 1) * C_VAL_DIM]
            parts.append(oh * lax.rsqrt(jnp.mean(oh * oh, axis=-1, keepdims=True) + EPS))
        cr = cr_ref[0, rs, :]
        o_ref[0, rs, :] = (jnp.concatenate(parts, axis=1) * (cr * jax.nn.sigmoid(cr))
                           * go_ref[...]).astype(BF16)


def _gla(proj, w_a2, b_a, g_out_c):
    bsz, seq, _ = proj.shape
    tg = min(T_GLA, seq)
    wa = jnp.zeros((LANES, C_KWIDTH), F32).at[MISC_CA:MISC_CA + C_GATE_RANK].set(w_a2)
    return pl.pallas_call(
        _gla_kernel,
        grid=(bsz, seq // tg),
        in_specs=[
            pl.BlockSpec((1, tg, C_KWIDTH), lambda b, i: (b, i, COL_CQ // C_KWIDTH)),
            pl.BlockSpec((1, tg, C_KWIDTH), lambda b, i: (b, i, COL_CK // C_KWIDTH)),
            pl.BlockSpec((1, tg, C_WIDTH), lambda b, i: (b, i, COL_CV // C_WIDTH)),
            pl.BlockSpec((1, tg, C_WIDTH), lambda b, i: (b, i, COL_CR // C_WIDTH)),
            pl.BlockSpec((1, tg, LANES), lambda b, i: (b, i, COL_MISC // LANES)),
            pl.BlockSpec((LANES, C_KWIDTH), lambda b, i: (0, 0)),
            pl.BlockSpec((1, C_KWIDTH), lambda b, i: (0, 0)),
            pl.BlockSpec((1, C_WIDTH), lambda b, i: (0, 0)),
        ],
        out_specs=pl.BlockSpec((1, tg, C_WIDTH), lambda b, i: (b, i, 0)),
        out_shape=jax.ShapeDtypeStruct((bsz, seq, C_WIDTH), BF16),
        scratch_shapes=[pltpu.VMEM((C_KWIDTH, C_WIDTH), F32)],
        compiler_params=_cparams(("parallel", "arbitrary")),
        name="gla",
    )(proj, proj, proj, proj, proj, wa, b_a.reshape(1, C_KWIDTH), g_out_c.reshape(1, C_WIDTH))


def _out_proj_kernel(oa_ref, ob_ref, oc_ref, w_ref, x_ref, mod_ref, o_ref):
    y = _dot(oa_ref[0], w_ref[0:A_WIDTH, :])
    y = y + _dot(ob_ref[0], w_ref[A_WIDTH:A_WIDTH + B_WIDTH, :])
    y = y + _dot(oc_ref[0], w_ref[A_WIDTH + B_WIDTH:, :])
    o_ref[0] = x_ref[0] + mod_ref[0, 2:3, :] * y


def _out_proj(o_a, o_b, o_c, w_out_bf16, x, mod_l):
    bsz, seq, d = x.shape
    tm = min(TM_MM, seq)
    dm = w_out_bf16.shape[0]
    return pl.pallas_call(
        _out_proj_kernel,
        grid=(bsz, seq // tm),
        in_specs=[
            pl.BlockSpec((1, tm, A_WIDTH), lambda b, i: (b, i, 0)),
            pl.BlockSpec((1, tm, B_WIDTH), lambda b, i: (b, i, 0)),
            pl.BlockSpec((1, tm, C_WIDTH), lambda b, i: (b, i, 0)),
            pl.BlockSpec((dm, d), lambda b, i: (0, 0)),
            pl.BlockSpec((1, tm, d), lambda b, i: (b, i, 0)),
            pl.BlockSpec((1, N_ADA, d), lambda b, i: (b, 0, 0)),
        ],
        out_specs=pl.BlockSpec((1, tm, d), lambda b, i: (b, i, 0)),
        out_shape=jax.ShapeDtypeStruct((bsz, seq, d), F32),
        compiler_params=_cparams(("parallel", "parallel")),
        name="out_proj",
    )(o_a, o_b, o_c, w_out_bf16, x, mod_l)


def _router_kernel(x_ref, mod_ref, g_ref, wr_ref, gates_ref):
    h = _norm_mod(x_ref[0], g_ref[...], mod_ref[0, 4:5, :], mod_ref[0, 3:4, :])
    logits = _dot3(h, wr_ref[...])
    lane = lax.broadcasted_iota(jnp.int32, logits.shape, 1).astype(F32)
    logits = jnp.where(lane < N_EXPERTS, logits, -jnp.inf)
    m1 = jnp.max(logits, axis=-1, keepdims=True)
    i1 = jnp.min(jnp.where(logits == m1, lane, float(LANES)), axis=-1, keepdims=True)
    rest = jnp.where(lane == i1, -jnp.inf, logits)
    m2 = jnp.max(rest, axis=-1, keepdims=True)
    i2 = jnp.min(jnp.where(rest == m2, lane, float(LANES)), axis=-1, keepdims=True)
    e2 = jnp.exp(m2 - m1)
    den = 1.0 + e2
    gates_ref[0] = jnp.where(lane == i1, 1.0 / den, 0.0) + jnp.where(lane == i2, e2 / den, 0.0)


def _router(x, mod_l, g, w_router):
    bsz, seq, d = x.shape
    tm = min(TM_MM, seq)
    wr = jnp.pad(w_router, ((0, 0), (0, LANES - N_EXPERTS)))
    return pl.pallas_call(
        _router_kernel,
        grid=(bsz, seq // tm),
        in_specs=[
            pl.BlockSpec((1, tm, d), lambda b, i: (b, i, 0)),
            pl.BlockSpec((1, N_ADA, d), lambda b, i: (b, 0, 0)),
            pl.BlockSpec((1, d), lambda b, i: (0, 0)),
            pl.BlockSpec((d, LANES), lambda b, i: (0, 0)),
        ],
        out_specs=pl.BlockSpec((1, tm, LANES), lambda b, i: (b, i, 0)),
        out_shape=jax.ShapeDtypeStruct((bsz, seq, LANES), F32),
        compiler_params=_cparams(("parallel", "parallel")),
        name="router",
    )(x, mod_l, g.reshape(1, d), wr)


def _ffn_kernel(*refs, moe, n_e, n_f):
    if moe:
        x_ref, mod_ref, g_ref, gates_ref, wg_ref, wu_ref, wd_ref, o_ref, h_scr, acc_scr = refs
        e, f = pl.program_id(2), pl.program_id(3)
    else:
        x_ref, mod_ref, g_ref, wg_ref, wu_ref, wd_ref, o_ref, h_scr, acc_scr = refs
        e, f = 0, pl.program_id(2)
    first = (f == 0) if not moe else (e == 0) & (f == 0)
    last = (f == n_f - 1) if not moe else (e == n_e - 1) & (f == n_f - 1)

    @pl.when(first)
    def _():
        h = _norm_mod(x_ref[0], g_ref[...], mod_ref[0, 4:5, :], mod_ref[0, 3:4, :])
        h_scr[...] = h.astype(BF16)
        acc_scr[...] = jnp.zeros(acc_scr.shape, F32)

    h = h_scr[...]
    gate = _dot(h, wg_ref[...])
    a = gate * jax.nn.sigmoid(gate) * _dot(h, wu_ref[...])
    if moe:
        gates = gates_ref[0]
        lane = lax.broadcasted_iota(jnp.int32, gates.shape, 1)
        a = a * jnp.sum(jnp.where(lane == e, gates, 0.0), axis=-1, keepdims=True)
    acc_scr[...] += _dot(a.astype(BF16), wd_ref[...])

    @pl.when(last)
    def _():
        o_ref[0] = x_ref[0] + mod_ref[0, 5:6, :] * acc_scr[...]


def _ffn(x, mod_l, g, w_gate, w_up, w_down, gates=None):
    bsz, seq, d = x.shape
    moe = gates is not None
    tm, tf = min(TM_MM, seq), TF_FFN
    fp = w_gate.shape[-1]
    n_f = fp // tf
    n_e = w_gate.shape[0] if moe else 1
    if moe:
        grid = (bsz, seq // tm, n_e, n_f)
        row = lambda b, i, e, f: (b, i, 0)
        per_b = lambda b, i, e, f: (b, 0, 0)
        const = lambda b, i, e, f: (0, 0)
        w_in_spec = pl.BlockSpec((None, d, tf), lambda b, i, e, f: (e, 0, f))
        w_dn_spec = pl.BlockSpec((None, tf, d), lambda b, i, e, f: (e, f, 0))
        sem = ("parallel", "parallel", "arbitrary", "arbitrary")
    else:
        grid = (bsz, seq // tm, n_f)
        row = lambda b, i, f: (b, i, 0)
        per_b = lambda b, i, f: (b, 0, 0)
        const = lambda b, i, f: (0, 0)
        w_in_spec = pl.BlockSpec((d, tf), lambda b, i, f: (0, f))
        w_dn_spec = pl.BlockSpec((tf, d), lambda b, i, f: (f, 0))
        sem = ("parallel", "parallel", "arbitrary")
    in_specs = [pl.BlockSpec((1, tm, d), row), pl.BlockSpec((1, N_ADA, d), per_b),
                pl.BlockSpec((1, d), const)]
    args = [x, mod_l, g.reshape(1, d)]
    if moe:
        in_specs.append(pl.BlockSpec((1, tm, LANES), row))
        args.append(gates)
    in_specs += [w_in_spec, w_in_spec, w_dn_spec]
    args += [w_gate, w_up, w_down]
    return pl.pallas_call(
        functools.partial(_ffn_kernel, moe=moe, n_e=n_e, n_f=n_f),
        grid=grid,
        in_specs=in_specs,
        out_specs=pl.BlockSpec((1, tm, d), row),
        out_shape=jax.ShapeDtypeStruct((bsz, seq, d), F32),
        scratch_shapes=[pltpu.VMEM((tm, d), BF16), pltpu.VMEM((tm, d), F32)],
        compiler_params=_cparams(sem),
        name="moe_ffn" if moe else "dense_ffn",
    )(*args)


def _reorder_w_in(w):
    sizes = (A_WIDTH, A_HEAD_DIM, A_HEAD_DIM, I_WIDTH, IDX_DIM, IDX_HEADS, B_WIDTH, B_WIDTH,
             C_KWIDTH, C_KWIDTH, C_WIDTH, C_GATE_RANK, C_WIDTH)
    offs = [0]
    for s in sizes:
        offs.append(offs[-1] + s)
    aq, ak, av, iq, ik, iw, bu, bv, cq, ck, cv, ca, cr = (
        w[:, offs[i]:offs[i + 1]] for i in range(len(sizes)))
    pad = jnp.zeros((w.shape[0], LANES - IDX_DIM - IDX_HEADS - C_GATE_RANK), w.dtype)
    return jnp.concatenate([aq, iq, bu, bv, cq, ck, cv, cr, ak, av, ik, iw, ca, pad],
                           axis=1).astype(BF16)


def _pad_ff(w_gate, w_up, w_down):
    f = w_gate.shape[-1]
    fp = -(-f // TF_FFN) * TF_FFN
    pad_c = [(0, 0)] * (w_gate.ndim - 1) + [(0, fp - f)]
    pad_r = [(0, 0)] * (w_down.ndim - 2) + [(0, fp - f), (0, 0)]
    return (jnp.pad(w_gate.astype(BF16), pad_c), jnp.pad(w_up.astype(BF16), pad_c),
            jnp.pad(w_down.astype(BF16), pad_r))


def _head_major(a, heads, dim):
    bsz, seq, _ = a.shape
    return a.reshape(bsz, seq, heads, dim).transpose(0, 2, 1, 3)


def kernel(x, c, w_ada, b_ada, g_norm1, g_norm2, w_in, g_q, g_k, g_v_b, w_s, b_s, w_a2, b_a,
           g_out, w_out, w_ff_gate, w_ff_up, w_ff_down, w_router, w_e_gate, w_e_up, w_e_down):
    bsz, seq, d = x.shape
    depth = w_in.shape[0]
    mod = _ada_mod(c, w_ada, b_ada)
    tables = _rope_tables(seq)
    for layer in range(depth):
        mod_l = mod[layer]
        proj = _in_proj(x, mod_l, g_norm1[layer], _reorder_w_in(w_in[layer]))
        q_r, iq_r, kvik = _dsa_prep(proj, tables, g_q[layer], g_k[layer])
        iw = proj[:, :, COL_MISC + MISC_IW:COL_MISC + MISC_IW + IDX_HEADS]
        o_a = _dsa(_head_major(q_r, A_HEADS, A_HEAD_DIM), _head_major(iq_r, IDX_HEADS, IDX_DIM), iw,
                   kvik[:, :, 0:A_HEAD_DIM], kvik[:, :, A_HEAD_DIM:2 * A_HEAD_DIM],
                   kvik[:, :, LANES:LANES + IDX_DIM], g_out[layer, :A_WIDTH])
        o_a = o_a.transpose(0, 2, 1, 3).reshape(bsz, seq, A_WIDTH)
        o_b = _sgu(proj, g_v_b[layer], w_s[layer], b_s[layer], g_out[layer, A_WIDTH:A_WIDTH + B_WIDTH])
        o_c = _gla(proj, w_a2[layer], b_a[layer], g_out[layer, A_WIDTH + B_WIDTH:])
        x = _out_proj(o_a, o_b, o_c, w_out[layer].astype(BF16), x, mod_l)
        j = layer // 2
        if layer % 2 == 0:
            x = _ffn(x, mod_l, g_norm2[layer], *_pad_ff(w_ff_gate[j], w_ff_up[j], w_ff_down[j]))
        else:
            gates = _router(x, mod_l, g_norm2[layer], w_router[j])
            x = _ffn(x, mod_l, g_norm2[layer], *_pad_ff(w_e_gate[j], w_e_up[j], w_e_down[j]),
                     gates=gates)
    return x
```

```python
import functools

import jax
import jax.numpy as jnp
from jax import lax
from jax.experimental import pallas as pl
from jax.experimental.pallas import tpu as pltpu

F32 = jnp.float32
BF16 = jnp.bfloat16

A_HEADS = 8
A_HEAD_DIM = 64
IDX_HEADS = 8
IDX_DIM = 64
TOPK_MAX = 256
B_GROUPS = 8
B_GROUP_DIM = 64
B_CHUNK = 128
C_HEADS = 8
C_VAL_DIM = 128
C_KEY_DIM = 64
C_GATE_RANK = 16
C_GATE_TAU = 16.0
C_CHUNK = 64
ROPE_THETA = 500000.0
ROPE_DIM = 16
N_EXPERTS = 8
N_ADA = 6
EPS = 1e-6

A_WIDTH = A_HEADS * A_HEAD_DIM
I_WIDTH = IDX_HEADS * IDX_DIM
B_WIDTH = B_GROUPS * B_GROUP_DIM
C_KWIDTH = C_HEADS * C_KEY_DIM
C_WIDTH = C_HEADS * C_VAL_DIM

LANES = 128
SUBLANES = 8
VMEM_LIMIT_BYTES = 56 * 1024 * 1024

COL_AQ = 0
COL_IQ = 512
COL_BU = 1024
COL_BV = 1536
COL_CQ = 2048
COL_CK = 2560
COL_CV = 3072
COL_CR = 4096
COL_AKV = 5120
COL_MISC = 5248
N_PROJ = 5376
MISC_IW = IDX_DIM
MISC_CA = IDX_DIM + IDX_HEADS

NEG_BIG = -1e30
INT_MIN = -(2 ** 31)

TM_MM = 512
TN_IN = 768
TF_FFN = 512
TN_ADA = 1024
TP_PREP = 512
Q_BLOCK = 128
K_CHUNK = 256
T_GLA = 256
TM_MOE = 512
G_ROWS = 256

INFO_E1, INFO_E2, INFO_R1, INFO_R2, INFO_W1, INFO_W2 = range(6)


def _cparams(sem):
    return pltpu.CompilerParams(dimension_semantics=sem, vmem_limit_bytes=VMEM_LIMIT_BYTES)


def _dot(a, b):
    return jnp.dot(a, b, preferred_element_type=F32)


def _dot_nt(a, b):
    return lax.dot_general(a, b, (((1,), (1,)), ((), ())), preferred_element_type=F32)


def _dot_tn(a, b):
    return lax.dot_general(a, b, (((0,), (0,)), ((), ())), preferred_element_type=F32)


def _split(x):
    hi = x.astype(BF16)
    lo = (x - hi.astype(F32)).astype(BF16)
    return hi, lo


def _dot_exact_lhs(m_bf16, x):
    hi, lo = _split(x)
    return _dot(m_bf16, hi) + _dot(m_bf16, lo)


def _dot_exact_rhs(x, m_bf16):
    hi, lo = _split(x)
    return _dot(hi, m_bf16) + _dot(lo, m_bf16)


def _dot3(a, b):
    ah, al = _split(a)
    bh, bl = _split(b)
    return _dot(ah, bh) + (_dot(al, bh) + _dot(ah, bl))


def _norm_mod(x, g, scale, shift):
    ms = jnp.mean(x * x, axis=-1, keepdims=True)
    return (x * lax.rsqrt(ms + EPS) * g) * (1.0 + scale) + shift


def _group_ones(width, group):
    r = lax.broadcasted_iota(jnp.int32, (width, width), 0) // group
    c = lax.broadcasted_iota(jnp.int32, (width, width), 1) // group
    return (r == c).astype(BF16)


def _ada_kernel(c_ref, w_ref, b_ref, o_ref):
    c = c_ref[...]
    cond = (c * jax.nn.sigmoid(c)).astype(BF16)
    o_ref[0] = _dot(cond, w_ref[0].astype(BF16)) + b_ref[0]


def _ada_mod(c, w_ada, b_ada):
    depth, d, n6 = w_ada.shape
    bsz = c.shape[0]
    rows = 16
    c_pad = jnp.pad(c, ((0, rows - bsz), (0, 0)))
    tn = TN_ADA
    out = pl.pallas_call(
        _ada_kernel,
        grid=(depth, n6 // tn),
        in_specs=[
            pl.BlockSpec((rows, d), lambda l, j: (0, 0)),
            pl.BlockSpec((1, d, tn), lambda l, j: (l, 0, j)),
            pl.BlockSpec((1, 1, tn), lambda l, j: (l, 0, j)),
        ],
        out_specs=pl.BlockSpec((1, rows, tn), lambda l, j: (l, 0, j)),
        out_shape=jax.ShapeDtypeStruct((depth, rows, n6), F32),
        compiler_params=_cparams(("parallel", "parallel")),
        name="ada_mod",
    )(c_pad, w_ada, b_ada.reshape(depth, 1, n6))
    return out[:, :bsz].reshape(depth, bsz, N_ADA, d)


def _in_proj_kernel(x_ref, mod_ref, g_ref, w_ref, o_ref, h_scr):
    @pl.when(pl.program_id(2) == 0)
    def _():
        h = _norm_mod(x_ref[0], g_ref[...], mod_ref[0, 1:2, :], mod_ref[0, 0:1, :])
        h_scr[...] = h.astype(BF16)

    o_ref[0] = _dot(h_scr[...], w_ref[...])


def _in_proj(x, mod_l, g, w_p):
    bsz, seq, d = x.shape
    n = w_p.shape[1]
    tm, tn = min(TM_MM, seq), TN_IN
    return pl.pallas_call(
        _in_proj_kernel,
        grid=(bsz, seq // tm, n // tn),
        in_specs=[
            pl.BlockSpec((1, tm, d), lambda b, i, j: (b, i, 0)),
            pl.BlockSpec((1, N_ADA, d), lambda b, i, j: (b, 0, 0)),
            pl.BlockSpec((1, d), lambda b, i, j: (0, 0)),
            pl.BlockSpec((d, tn), lambda b, i, j: (0, j)),
        ],
        out_specs=pl.BlockSpec((1, tm, tn), lambda b, i, j: (b, i, j)),
        out_shape=jax.ShapeDtypeStruct((bsz, seq, n), F32),
        scratch_shapes=[pltpu.VMEM((tm, d), BF16)],
        compiler_params=_cparams(("parallel", "parallel", "arbitrary")),
        name="in_proj",
    )(x, mod_l, g.reshape(1, d), w_p)


def _rope(x, cos, s_up, s_dn):
    parts = []
    for j in range(x.shape[1] // LANES):
        xs = x[:, j * LANES:(j + 1) * LANES]
        parts.append(xs * cos + pltpu.roll(xs, ROPE_DIM // 2, 1) * s_up
                     + pltpu.roll(xs, LANES - ROPE_DIM // 2, 1) * s_dn)
    return parts[0] if len(parts) == 1 else jnp.concatenate(parts, axis=1)


def _dsa_prep_kernel(aq_ref, iq_ref, akv_ref, misc_ref, cos_ref, sup_ref, sdn_ref, gq_ref, gk_ref,
                     q_out, iq_out, kvik_out):
    cos, s_up, s_dn = cos_ref[...], sup_ref[...], sdn_ref[...]
    aq = aq_ref[0]
    ss = _dot_exact_rhs(aq * aq, _group_ones(A_WIDTH, A_HEAD_DIM))
    qn = aq * lax.rsqrt(ss * (1.0 / A_HEAD_DIM) + EPS) * gq_ref[...]
    q_out[0] = (_rope(qn, cos, s_up, s_dn) * (A_HEAD_DIM ** -0.5)).astype(BF16)
    iq_out[0] = (_rope(iq_ref[0], cos, s_up, s_dn) * (IDX_DIM ** -0.5)).astype(BF16)
    lane = lax.broadcasted_iota(jnp.int32, (1, LANES), 1)
    first = lane < A_HEAD_DIM
    akv = akv_ref[0]
    kss = jnp.sum(jnp.where(first, akv * akv, 0.0), axis=-1, keepdims=True)
    kn = akv * lax.rsqrt(kss * (1.0 / A_HEAD_DIM) + EPS) * gk_ref[...]
    kvik_out[0, :, 0:LANES] = jnp.where(first, _rope(kn, cos, s_up, s_dn), akv).astype(BF16)
    misc = misc_ref[0]
    kvik_out[0, :, LANES:2 * LANES] = jnp.where(first, _rope(misc, cos, s_up, s_dn), misc).astype(BF16)


def _dsa_prep(proj, tables, g_q, g_k):
    bsz, seq, _ = proj.shape
    tp = min(TP_PREP, seq)
    cos, s_up, s_dn = tables
    gq = jnp.tile(g_q, A_HEADS).reshape(1, A_WIDTH)
    gk = jnp.concatenate([g_k, jnp.ones((LANES - A_HEAD_DIM,), F32)]).reshape(1, LANES)
    tab_spec = pl.BlockSpec((tp, LANES), lambda b, i: (i, 0))
    return pl.pallas_call(
        _dsa_prep_kernel,
        grid=(bsz, seq // tp),
        in_specs=[
            pl.BlockSpec((1, tp, A_WIDTH), lambda b, i: (b, i, COL_AQ // A_WIDTH)),
            pl.BlockSpec((1, tp, I_WIDTH), lambda b, i: (b, i, COL_IQ // I_WIDTH)),
            pl.BlockSpec((1, tp, LANES), lambda b, i: (b, i, COL_AKV // LANES)),
            pl.BlockSpec((1, tp, LANES), lambda b, i: (b, i, COL_MISC // LANES)),
            tab_spec, tab_spec, tab_spec,
            pl.BlockSpec((1, A_WIDTH), lambda b, i: (0, 0)),
            pl.BlockSpec((1, LANES), lambda b, i: (0, 0)),
        ],
        out_specs=[
            pl.BlockSpec((1, tp, A_WIDTH), lambda b, i: (b, i, 0)),
            pl.BlockSpec((1, tp, I_WIDTH), lambda b, i: (b, i, 0)),
            pl.BlockSpec((1, tp, 2 * LANES), lambda b, i: (b, i, 0)),
        ],
        out_shape=[
            jax.ShapeDtypeStruct((bsz, seq, A_WIDTH), BF16),
            jax.ShapeDtypeStruct((bsz, seq, I_WIDTH), BF16),
            jax.ShapeDtypeStruct((bsz, seq, 2 * LANES), BF16),
        ],
        compiler_params=_cparams(("parallel", "parallel")),
        name="dsa_prep",
    )(proj, proj, proj, proj, cos, s_up, s_dn, gq, gk)


def _rope_tables(seq):
    half = ROPE_DIM // 2
    inv_freq = ROPE_THETA ** (-jnp.arange(0, ROPE_DIM, 2, dtype=F32) / ROPE_DIM)
    ang = jnp.arange(seq, dtype=F32)[:, None] * inv_freq[None, :]
    cos, sin = jnp.cos(ang), jnp.sin(ang)
    pad = A_HEAD_DIM - ROPE_DIM
    cos64 = jnp.concatenate([cos, cos, jnp.ones((seq, pad), F32)], axis=1)
    up64 = jnp.concatenate([jnp.zeros((seq, half), F32), sin, jnp.zeros((seq, pad), F32)], axis=1)
    dn64 = jnp.concatenate([-sin, jnp.zeros((seq, half + pad), F32)], axis=1)
    rep = LANES // A_HEAD_DIM
    return jnp.tile(cos64, (1, rep)), jnp.tile(up64, (1, rep)), jnp.tile(dn64, (1, rep))


def _dsa_kernel(qt_ref, iqt_ref, iwt_ref, k_ref, vt_ref, ik_ref, got_ref, o_ref,
                key_scr, bias_scr, m_scr, l_scr, acc_scr, *, topk):
    qb, kcs = Q_BLOCK, key_scr.shape[1]
    blk = pl.program_id(1)
    n_chunks = (blk * qb) // kcs + 1
    krow = lax.broadcasted_iota(jnp.int32, (kcs, qb), 0)
    qcol = lax.broadcasted_iota(jnp.int32, (kcs, qb), 1)
    qpos = blk * qb + qcol
    iw = iwt_ref[0, 0] * (IDX_HEADS ** -0.5)

    def key_rows(kc):
        return pl.ds(pl.multiple_of(kc * kcs, kcs), kcs)

    def score_chunk(kc, carry):
        ks = ik_ref[0, key_rows(kc), :]
        acc = jnp.zeros((kcs, qb), F32)
        for hp in range(IDX_HEADS // 2):
            logits = _dot(ks, iqt_ref[0, 0, :, 2 * hp * qb:2 * (hp + 1) * qb])
            acc = acc + jnp.maximum(logits[:, :qb], 0.0) * iw[2 * hp:2 * hp + 1, :]
            acc = acc + jnp.maximum(logits[:, qb:], 0.0) * iw[2 * hp + 1:2 * hp + 2, :]
        acc = jnp.where(acc == 0.0, 0.0, acc)
        sc = jnp.where(kc * kcs + krow <= qpos, acc, -jnp.inf)
        bits = lax.bitcast_convert_type(sc, jnp.int32)
        key_scr[kc] = jnp.where(bits >= 0, bits, bits ^ 0x7FFFFFFF)
        return carry

    lax.fori_loop(0, n_chunks, score_chunk, 0)

    def count(pred_fn):
        def body(kc, c):
            return c + jnp.where(pred_fn(key_scr[kc]), 1.0, 0.0)
        c = lax.fori_loop(0, n_chunks, body, jnp.zeros((kcs, qb), F32))
        return jnp.sum(c, axis=0, keepdims=True)

    kf = float(topk)
    zero = jnp.zeros((1, qb), jnp.int32)
    thr0 = jnp.where(count(lambda k: k >= zero) >= kf, zero, jnp.full((1, qb), INT_MIN, jnp.int32))

    def bit_step(i, thr):
        cand = thr | jnp.left_shift(jnp.int32(1), 30 - i)
        return jnp.where(count(lambda k: k >= cand) >= kf, cand, thr)

    thr = lax.fori_loop(0, 31, bit_step, thr0)

    tie_overflow = jnp.max(count(lambda k: k >= thr)) > kf

    @pl.when(jnp.logical_not(tie_overflow))
    def _():
        def select_chunk(kc, carry):
            sel = (key_scr[kc] >= thr) & (kc * kcs + krow <= qpos)
            bias_scr[kc] = jnp.where(sel, 0.0, NEG_BIG)
            return carry

        lax.fori_loop(0, n_chunks, select_chunk, 0)

    @pl.when(tie_overflow)
    def _():
        need = kf - count(lambda k: k > thr)
        lower = (lax.broadcasted_iota(jnp.int32, (kcs, kcs), 1)
                 <= lax.broadcasted_iota(jnp.int32, (kcs, kcs), 0)).astype(BF16)

        def select_chunk(kc, carry):
            key = key_scr[kc]
            eq = key == thr
            eqf = jnp.where(eq, 1.0, 0.0)
            incl = _dot(lower, eqf.astype(BF16))
            sel = (key > thr) | (eq & (carry + incl - eqf < need))
            sel = sel & (kc * kcs + krow <= qpos)
            bias_scr[kc] = jnp.where(sel, 0.0, NEG_BIG)
            return carry + incl[kcs - 1:kcs, :]

        lax.fori_loop(0, n_chunks, select_chunk, jnp.zeros((1, qb), F32))

    m_scr[...] = jnp.full(m_scr.shape, NEG_BIG, F32)
    l_scr[...] = jnp.zeros(l_scr.shape, F32)
    acc_scr[...] = jnp.zeros(acc_scr.shape, F32)

    def attend_chunk(kc, carry):
        kk = k_ref[0, key_rows(kc), :]
        vt = vt_ref[0, kc]
        bias = bias_scr[kc]
        for h in range(A_HEADS):
            cols = slice(h * qb, (h + 1) * qb)
            s = _dot(kk, qt_ref[0, 0, :, cols]) + bias
            m_prev = m_scr[0:1, cols]
            m_new = jnp.maximum(m_prev, jnp.max(s, axis=0, keepdims=True))
            p = jnp.exp(s - m_new)
            alpha = jnp.exp(m_prev - m_new)
            l_scr[0:1, cols] = alpha * l_scr[0:1, cols] + jnp.sum(p, axis=0, keepdims=True)
            acc_scr[:, cols] = alpha * acc_scr[:, cols] + _dot(vt, p.astype(BF16))
            m_scr[0:1, cols] = m_new
        return carry

    lax.fori_loop(0, n_chunks, attend_chunk, 0)

    o = acc_scr[...] / l_scr[0:1, :]
    o = o * lax.rsqrt(jnp.mean(o * o, axis=0, keepdims=True) + EPS)
    o_ref[0, 0] = (o * got_ref[...]).astype(BF16)


def _dsa(q_r, iq_r, iw, k_r, v, ik_r, g_out_a):
    bsz, seq, _ = q_r.shape
    qb = Q_BLOCK
    topk = min(TOPK_MAX, seq // 4)
    n_blk = seq // qb
    lanes = A_HEADS * qb

    def lanes_hq(a, heads, dim):
        a = a.reshape(bsz, n_blk, qb, heads, dim).transpose(0, 1, 4, 3, 2)
        return a.reshape(bsz, n_blk, dim, heads * qb)

    qt = lanes_hq(q_r, A_HEADS, A_HEAD_DIM)
    iqt = lanes_hq(iq_r, IDX_HEADS, IDX_DIM)
    iwt = iw.reshape(bsz, n_blk, qb, IDX_HEADS).transpose(0, 1, 3, 2)
    kcs = min(K_CHUNK, seq)
    n_kc = seq // kcs
    vt = v.reshape(bsz, n_kc, kcs, A_HEAD_DIM).transpose(0, 1, 3, 2)
    got = jnp.repeat(g_out_a.reshape(A_HEADS, A_HEAD_DIM).T, qb, axis=1)
    k_spec = pl.BlockSpec((1, seq, A_HEAD_DIM), lambda b, i: (b, 0, 0))
    q_spec = pl.BlockSpec((1, 1, A_HEAD_DIM, lanes), lambda b, i: (b, i, 0, 0))
    ot = pl.pallas_call(
        functools.partial(_dsa_kernel, topk=topk),
        grid=(bsz, n_blk),
        in_specs=[
            q_spec, q_spec,
            pl.BlockSpec((1, 1, IDX_HEADS, qb), lambda b, i: (b, i, 0, 0)),
            k_spec,
            pl.BlockSpec((1, n_kc, A_HEAD_DIM, kcs), lambda b, i: (b, 0, 0, 0)),
            k_spec,
            pl.BlockSpec((A_HEAD_DIM, lanes), lambda b, i: (0, 0)),
        ],
        out_specs=q_spec,
        out_shape=jax.ShapeDtypeStruct((bsz, n_blk, A_HEAD_DIM, lanes), BF16),
        scratch_shapes=[
            pltpu.VMEM((n_kc, kcs, qb), jnp.int32),
            pltpu.VMEM((n_kc, kcs, qb), F32),
            pltpu.VMEM((SUBLANES, lanes), F32),
            pltpu.VMEM((SUBLANES, lanes), F32),
            pltpu.VMEM((A_HEAD_DIM, lanes), F32),
        ],
        compiler_params=_cparams(("parallel", "arbitrary")),
        name="dsa_attention",
    )(qt, iqt, iwt, k_r, vt, ik_r, got)
    ot = ot.reshape(bsz, n_blk, A_HEAD_DIM, A_HEADS, qb).transpose(0, 1, 4, 3, 2)
    return ot.reshape(bsz, seq, A_WIDTH)


def _sgu_kernel(bu_ref, bv_ref, gv_ref, ws_ref, bst_ref, go_ref, o_ref):
    ch = B_CHUNK
    u = jax.nn.gelu(bu_ref[0])
    v = jax.nn.gelu(bv_ref[0])
    vc = v - jnp.mean(v, axis=-1, keepdims=True)
    vn = vc * lax.rsqrt(jnp.mean(vc * vc, axis=-1, keepdims=True) + EPS) * gv_ref[...]
    vb = vn.astype(BF16)
    row = lax.broadcasted_iota(jnp.int32, (ch, ch), 0)
    col = lax.broadcasted_iota(jnp.int32, (ch, ch), 1)
    causal = col <= row
    grp = lax.broadcasted_iota(jnp.int32, (1, B_WIDTH), 1) // B_GROUP_DIM
    bst = bst_ref[...]
    mixed = jnp.zeros((ch, B_WIDTH), F32)
    for g in range(B_GROUPS):
        w = jnp.where(causal, ws_ref[g], 0.0).astype(BF16)
        mixed = jnp.where(grp == g, _dot(w, vb) + bst[:, g:g + 1], mixed)
    o = u * mixed
    ss = _dot_exact_rhs(o * o, _group_ones(B_WIDTH, B_GROUP_DIM))
    o_ref[0] = (o * lax.rsqrt(ss * (1.0 / B_GROUP_DIM) + EPS) * go_ref[...]).astype(BF16)


def _sgu(proj, g_v, w_s, b_s, g_out_b):
    bsz, seq, _ = proj.shape
    ch = B_CHUNK
    return pl.pallas_call(
        _sgu_kernel,
        grid=(bsz, seq // ch),
        in_specs=[
            pl.BlockSpec((1, ch, B_WIDTH), lambda b, i: (b, i, COL_BU // B_WIDTH)),
            pl.BlockSpec((1, ch, B_WIDTH), lambda b, i: (b, i, COL_BV // B_WIDTH)),
            pl.BlockSpec((1, B_WIDTH), lambda b, i: (0, 0)),
            pl.BlockSpec((B_GROUPS, ch, ch), lambda b, i: (0, 0, 0)),
            pl.BlockSpec((ch, B_GROUPS), lambda b, i: (0, 0)),
            pl.BlockSpec((1, B_WIDTH), lambda b, i: (0, 0)),
        ],
        out_specs=pl.BlockSpec((1, ch, B_WIDTH), lambda b, i: (b, i, 0)),
        out_shape=jax.ShapeDtypeStruct((bsz, seq, B_WIDTH), BF16),
        compiler_params=_cparams(("parallel", "parallel")),
        name="spatial_gating",
    )(proj, proj, g_v.reshape(1, B_WIDTH), w_s, b_s.T, g_out_b.reshape(1, B_WIDTH))


def _log_sigmoid(z):
    return jnp.minimum(z, 0.0) - jnp.log1p(jnp.exp(-jnp.abs(z)))


def _gla_kernel(cq_ref, ck_ref, cv_ref, cr_ref, misc_ref, wa_ref, ba_ref, go_ref, o_ref, st_scr):
    tg, ch = cq_ref.shape[1], C_CHUNK

    @pl.when(pl.program_id(1) == 0)
    def _():
        st_scr[...] = jnp.zeros(st_scr.shape, F32)

    z = _dot3(misc_ref[0], wa_ref[...]) + ba_ref[...]
    log_a = _log_sigmoid(z) * (1.0 / C_GATE_TAU)
    r = lax.broadcasted_iota(jnp.int32, (tg, tg), 0)
    c = lax.broadcasted_iota(jnp.int32, (tg, tg), 1)
    same = (r // ch) == (c // ch)
    b = _dot_exact_lhs((same & (c <= r)).astype(BF16), log_a)
    b_last = _dot_exact_lhs(same.astype(BF16), log_a)
    ck = ck_ref[0]
    q_dec = cq_ref[0] * (C_KEY_DIM ** -0.5) * jnp.exp(b)
    k_neg = (ck * jnp.exp(-b)).astype(BF16)
    k_st = (ck * jnp.exp(b_last - b)).astype(BF16)
    cv = cv_ref[0].astype(BF16)

    lane_head = lax.broadcasted_iota(jnp.int32, (C_HEADS, 1, C_KWIDTH), 2) // C_KEY_DIM
    head_mask = lane_head == lax.broadcasted_iota(jnp.int32, (C_HEADS, 1, C_KWIDTH), 0)
    tril = (lax.broadcasted_iota(jnp.int32, (1, ch, ch), 2)
            <= lax.broadcasted_iota(jnp.int32, (1, ch, ch), 1))
    sr = lax.broadcasted_iota(jnp.int32, (C_KWIDTH, C_WIDTH), 0) // C_KEY_DIM
    sc = lax.broadcasted_iota(jnp.int32, (C_KWIDTH, C_WIDTH), 1) // C_VAL_DIM
    block_diag = sr == sc
    ones_cols = jnp.ones((ch, LANES), BF16)

    for n in range(tg // ch):
        rs = slice(n * ch, (n + 1) * ch)
        qd = q_dec[rs]
        vn = cv[rs]
        qm = jnp.where(head_mask, qd[None], 0.0).reshape(C_HEADS * ch, C_KWIDTH).astype(BF16)
        att = _dot_nt(qm, k_neg[rs]).reshape(C_HEADS, ch, ch)
        att = jnp.where(tril, att, 0.0).astype(BF16)
        o_intra = jnp.concatenate(
            [_dot(att[h], vn[:, h * C_VAL_DIM:(h + 1) * C_VAL_DIM]) for h in range(C_HEADS)], axis=1)
        state = st_scr[...]
        o = o_intra + _dot(qd.astype(BF16), state.astype(BF16))
        la_hi, la_lo = _split(log_a[rs])
        decay = jnp.exp(_dot_tn(la_hi, ones_cols) + _dot_tn(la_lo, ones_cols))[:, 0:1]
        st_scr[...] = jnp.where(block_diag, decay * state + _dot_tn(k_st[rs], vn), 0.0)
        parts = []
        for h in range(C_HEADS):
            oh = o[:, h * C_VAL_DIM:(h + 1) * C_VAL_DIM]
            parts.append(oh * lax.rsqrt(jnp.mean(oh * oh, axis=-1, keepdims=True) + EPS))
        cr = cr_ref[0, rs, :]
        o_ref[0, rs, :] = (jnp.concatenate(parts, axis=1) * (cr * jax.nn.sigmoid(cr))
                           * go_ref[...]).astype(BF16)


def _gla(proj, w_a2, b_a, g_out_c):
    bsz, seq, _ = proj.shape
    tg = min(T_GLA, seq)
    wa = jnp.zeros((LANES, C_KWIDTH), F32).at[MISC_CA:MISC_CA + C_GATE_RANK].set(w_a2)
    return pl.pallas_call(
        _gla_kernel,
        grid=(bsz, seq // tg),
        in_specs=[
            pl.BlockSpec((1, tg, C_KWIDTH), lambda b, i: (b, i, COL_CQ // C_KWIDTH)),
            pl.BlockSpec((1, tg, C_KWIDTH), lambda b, i: (b, i, COL_CK // C_KWIDTH)),
            pl.BlockSpec((1, tg, C_WIDTH), lambda b, i: (b, i, COL_CV // C_WIDTH)),
            pl.BlockSpec((1, tg, C_WIDTH), lambda b, i: (b, i, COL_CR // C_WIDTH)),
            pl.BlockSpec((1, tg, LANES), lambda b, i: (b, i, COL_MISC // LANES)),
            pl.BlockSpec((LANES, C_KWIDTH), lambda b, i: (0, 0)),
            pl.BlockSpec((1, C_KWIDTH), lambda b, i: (0, 0)),
            pl.BlockSpec((1, C_WIDTH), lambda b, i: (0, 0)),
        ],
        out_specs=pl.BlockSpec((1, tg, C_WIDTH), lambda b, i: (b, i, 0)),
        out_shape=jax.ShapeDtypeStruct((bsz, seq, C_WIDTH), BF16),
        scratch_shapes=[pltpu.VMEM((C_KWIDTH, C_WIDTH), F32)],
        compiler_params=_cparams(("parallel", "arbitrary")),
        name="gla",
    )(proj, proj, proj, proj, proj, wa, b_a.reshape(1, C_KWIDTH), g_out_c.reshape(1, C_WIDTH))


def _out_proj_kernel(oa_ref, ob_ref, oc_ref, w_ref, x_ref, mod_ref, o_ref):
    y = _dot(oa_ref[0], w_ref[0:A_WIDTH, :])
    y = y + _dot(ob_ref[0], w_ref[A_WIDTH:A_WIDTH + B_WIDTH, :])
    y = y + _dot(oc_ref[0], w_ref[A_WIDTH + B_WIDTH:, :])
    o_ref[0] = x_ref[0] + mod_ref[0, 2:3, :] * y


def _out_proj(o_a, o_b, o_c, w_out_bf16, x, mod_l):
    bsz, seq, d = x.shape
    tm = min(TM_MM, seq)
    dm = w_out_bf16.shape[0]
    return pl.pallas_call(
        _out_proj_kernel,
        grid=(bsz, seq // tm),
        in_specs=[
            pl.BlockSpec((1, tm, A_WIDTH), lambda b, i: (b, i, 0)),
            pl.BlockSpec((1, tm, B_WIDTH), lambda b, i: (b, i, 0)),
            pl.BlockSpec((1, tm, C_WIDTH), lambda b, i: (b, i, 0)),
            pl.BlockSpec((dm, d), lambda b, i: (0, 0)),
            pl.BlockSpec((1, tm, d), lambda b, i: (b, i, 0)),
            pl.BlockSpec((1, N_ADA, d), lambda b, i: (b, 0, 0)),
        ],
        out_specs=pl.BlockSpec((1, tm, d), lambda b, i: (b, i, 0)),
        out_shape=jax.ShapeDtypeStruct((bsz, seq, d), F32),
        compiler_params=_cparams(("parallel", "parallel")),
        name="out_proj",
    )(o_a, o_b, o_c, w_out_bf16, x, mod_l)


def _ffn_kernel(x_ref, mod_ref, g_ref, wg_ref, wu_ref, wd_ref, o_ref, h_scr, acc_scr, *, n_f):
    f = pl.program_id(2)

    @pl.when(f == 0)
    def _():
        h = _norm_mod(x_ref[0], g_ref[...], mod_ref[0, 4:5, :], mod_ref[0, 3:4, :])
        h_scr[...] = h.astype(BF16)
        acc_scr[...] = jnp.zeros(acc_scr.shape, F32)

    h = h_scr[...]
    gate = _dot(h, wg_ref[...])
    a = gate * jax.nn.sigmoid(gate) * _dot(h, wu_ref[...])
    acc_scr[...] += _dot(a.astype(BF16), wd_ref[...])

    @pl.when(f == n_f - 1)
    def _():
        o_ref[0] = x_ref[0] + mod_ref[0, 5:6, :] * acc_scr[...]


def _ffn(x, mod_l, g, w_gate, w_up, w_down):
    bsz, seq, d = x.shape
    tm, tf = min(TM_MM, seq), TF_FFN
    n_f = w_gate.shape[-1] // tf
    row = lambda b, i, f: (b, i, 0)
    return pl.pallas_call(
        functools.partial(_ffn_kernel, n_f=n_f),
        grid=(bsz, seq // tm, n_f),
        in_specs=[
            pl.BlockSpec((1, tm, d), row),
            pl.BlockSpec((1, N_ADA, d), lambda b, i, f: (b, 0, 0)),
            pl.BlockSpec((1, d), lambda b, i, f: (0, 0)),
            pl.BlockSpec((d, tf), lambda b, i, f: (0, f)),
            pl.BlockSpec((d, tf), lambda b, i, f: (0, f)),
            pl.BlockSpec((tf, d), lambda b, i, f: (f, 0)),
        ],
        out_specs=pl.BlockSpec((1, tm, d), row),
        out_shape=jax.ShapeDtypeStruct((bsz, seq, d), F32),
        scratch_shapes=[pltpu.VMEM((tm, d), BF16), pltpu.VMEM((tm, d), F32)],
        compiler_params=_cparams(("parallel", "parallel", "arbitrary")),
        name="dense_ffn",
    )(x, mod_l, g.reshape(1, d), w_gate, w_up, w_down)


def _router_kernel(x_ref, mod_ref, g_ref, wr_ref, h_ref, info_ref, cnt_ref, run_scr):
    @pl.when((pl.program_id(0) == 0) & (pl.program_id(1) == 0))
    def _():
        run_scr[...] = jnp.zeros(run_scr.shape, F32)

    h = _norm_mod(x_ref[0], g_ref[...], mod_ref[0, 4:5, :], mod_ref[0, 3:4, :])
    h_ref[0] = h
    logits = _dot3(h, wr_ref[...])
    tm = logits.shape[0]
    lane = lax.broadcasted_iota(jnp.int32, logits.shape, 1).astype(F32)
    logits = jnp.where(lane < N_EXPERTS, logits, -jnp.inf)
    m1 = jnp.max(logits, axis=-1, keepdims=True)
    i1 = jnp.min(jnp.where(logits == m1, lane, float(LANES)), axis=-1, keepdims=True)
    rest = jnp.where(lane == i1, -jnp.inf, logits)
    m2 = jnp.max(rest, axis=-1, keepdims=True)
    i2 = jnp.min(jnp.where(rest == m2, lane, float(LANES)), axis=-1, keepdims=True)
    e2 = jnp.exp(m2 - m1)
    den = 1.0 + e2
    hit1, hit2 = lane == i1, lane == i2
    hits = jnp.where(hit1 | hit2, 1.0, 0.0)
    earlier = (lax.broadcasted_iota(jnp.int32, (tm, tm), 1)
               < lax.broadcasted_iota(jnp.int32, (tm, tm), 0)).astype(BF16)
    rank = run_scr[0:1, :] + _dot(earlier, hits.astype(BF16))
    r1 = jnp.sum(jnp.where(hit1, rank, 0.0), axis=-1, keepdims=True)
    r2 = jnp.sum(jnp.where(hit2, rank, 0.0), axis=-1, keepdims=True)
    run_scr[0:1, :] = run_scr[0:1, :] + jnp.sum(hits, axis=0, keepdims=True)
    info = jnp.zeros(logits.shape, F32)
    for k, val in ((INFO_E1, i1), (INFO_E2, i2), (INFO_R1, r1), (INFO_R2, r2),
                   (INFO_W1, 1.0 / den), (INFO_W2, e2 / den)):
        info = jnp.where(lane == float(k), val, info)
    info_ref[0] = info
    cnt_ref[...] = jnp.broadcast_to(run_scr[0:1, :], cnt_ref.shape)


def _router(x, mod_l, g, w_router):
    bsz, seq, d = x.shape
    tm = min(TM_MM, seq)
    wr = jnp.pad(w_router, ((0, 0), (0, LANES - N_EXPERTS)))
    return pl.pallas_call(
        _router_kernel,
        grid=(bsz, seq // tm),
        in_specs=[
            pl.BlockSpec((1, tm, d), lambda b, i: (b, i, 0)),
            pl.BlockSpec((1, N_ADA, d), lambda b, i: (b, 0, 0)),
            pl.BlockSpec((1, d), lambda b, i: (0, 0)),
            pl.BlockSpec((d, LANES), lambda b, i: (0, 0)),
        ],
        out_specs=[
            pl.BlockSpec((1, tm, d), lambda b, i: (b, i, 0)),
            pl.BlockSpec((1, tm, LANES), lambda b, i: (b, i, 0)),
            pl.BlockSpec((SUBLANES, LANES), lambda b, i: (0, 0)),
        ],
        out_shape=[
            jax.ShapeDtypeStruct((bsz, seq, d), F32),
            jax.ShapeDtypeStruct((bsz, seq, LANES), F32),
            jax.ShapeDtypeStruct((SUBLANES, LANES), F32),
        ],
        scratch_shapes=[pltpu.VMEM((SUBLANES, LANES), F32)],
        compiler_params=_cparams(("arbitrary", "arbitrary")),
        name="router",
    )(x, mod_l, g.reshape(1, d), wr)


def _row_copy(src, dst, sem):
    return pltpu.make_async_copy(src, dst, sem)


def _dispatch_kernel(p1_ref, p2_ref, h_ref, zero_hbm, out_hbm, sem):
    del zero_hbm
    g = h_ref.shape[0]
    base = pl.program_id(0) * g

    def issue(r, carry):
        row = h_ref.at[pl.ds(r, 1)]
        _row_copy(row, out_hbm.at[pl.ds(p1_ref[base + r], 1)], sem).start()
        _row_copy(row, out_hbm.at[pl.ds(p2_ref[base + r], 1)], sem).start()
        return carry

    lax.fori_loop(0, g, issue, 0)

    def drain(r, carry):
        _row_copy(h_ref.at[pl.ds(0, 1)], out_hbm.at[pl.ds(0, 1)], sem).wait()
        return carry

    lax.fori_loop(0, 2 * g, drain, 0)


def _dispatch(h2d, pos1, pos2, n_rows):
    n, d = h2d.shape
    g = min(G_ROWS, n)
    grid_spec = pltpu.PrefetchScalarGridSpec(
        num_scalar_prefetch=2,
        grid=(n // g,),
        in_specs=[
            pl.BlockSpec((g, d), lambda i, p1, p2: (i, 0)),
            pl.BlockSpec(memory_space=pl.ANY),
        ],
        out_specs=pl.BlockSpec(memory_space=pl.ANY),
        scratch_shapes=[pltpu.SemaphoreType.DMA],
    )
    return pl.pallas_call(
        _dispatch_kernel,
        grid_spec=grid_spec,
        out_shape=jax.ShapeDtypeStruct((n_rows, d), F32),
        input_output_aliases={3: 0},
        compiler_params=_cparams(("arbitrary",)),
        name="moe_dispatch",
    )(pos1, pos2, h2d, jnp.zeros((n_rows, d), F32))


def _moe_ffn_kernel(te_ref, nu_ref, hs_ref, wg_ref, wu_ref, wd_ref, o_ref, h_scr, acc_scr, *, n_f):
    del te_ref
    j, f = pl.program_id(0), pl.program_id(1)
    used = j < nu_ref[0]

    @pl.when(used & (f == 0))
    def _():
        h_scr[...] = hs_ref[...].astype(BF16)
        acc_scr[...] = jnp.zeros(acc_scr.shape, F32)

    @pl.when(used)
    def _():
        h = h_scr[...]
        gate = _dot(h, wg_ref[...])
        a = gate * jax.nn.sigmoid(gate) * _dot(h, wu_ref[...])
        acc_scr[...] += _dot(a.astype(BF16), wd_ref[...])

    @pl.when(used & (f == n_f - 1))
    def _():
        o_ref[...] = acc_scr[...]

    @pl.when(jnp.logical_not(used) & (f == n_f - 1))
    def _():
        o_ref[...] = jnp.zeros(o_ref.shape, F32)


def _moe_ffn(h_sorted, tile_expert, n_used, w_gate, w_up, w_down):
    n_rows, d = h_sorted.shape
    tm, tf = TM_MOE, TF_FFN
    n_f = w_gate.shape[-1] // tf
    n_tiles = n_rows // tm

    def live(j, nu):
        return jnp.minimum(j, nu[0] - 1)

    def fcol(j, f, nu):
        return jnp.where(j < nu[0], f, n_f - 1)

    grid_spec = pltpu.PrefetchScalarGridSpec(
        num_scalar_prefetch=2,
        grid=(n_tiles, n_f),
        in_specs=[
            pl.BlockSpec((tm, d), lambda j, f, te, nu: (live(j, nu), 0)),
            pl.BlockSpec((None, d, tf), lambda j, f, te, nu: (te[live(j, nu)], 0, fcol(j, f, nu))),
            pl.BlockSpec((None, d, tf), lambda j, f, te, nu: (te[live(j, nu)], 0, fcol(j, f, nu))),
            pl.BlockSpec((None, tf, d), lambda j, f, te, nu: (te[live(j, nu)], fcol(j, f, nu), 0)),
        ],
        out_specs=pl.BlockSpec((tm, d), lambda j, f, te, nu: (j, 0)),
        scratch_shapes=[pltpu.VMEM((tm, d), BF16), pltpu.VMEM((tm, d), F32)],
    )
    return pl.pallas_call(
        functools.partial(_moe_ffn_kernel, n_f=n_f),
        grid_spec=grid_spec,
        out_shape=jax.ShapeDtypeStruct((n_rows, d), F32),
        compiler_params=_cparams(("arbitrary", "arbitrary")),
        name="moe_ffn",
    )(tile_expert, n_used, h_sorted, w_gate, w_up, w_down)


def _combine_kernel(p1_ref, p2_ref, x_ref, mod_ref, info_ref, y_hbm, o_ref, buf, sem):
    g = x_ref.shape[1]
    base = (pl.program_id(0) * pl.num_programs(1) + pl.program_id(1)) * g

    def issue(r, carry):
        _row_copy(y_hbm.at[pl.ds(p1_ref[base + r], 1)], buf.at[0, pl.ds(r, 1)], sem).start()
        _row_copy(y_hbm.at[pl.ds(p2_ref[base + r], 1)], buf.at[1, pl.ds(r, 1)], sem).start()
        return carry

    lax.fori_loop(0, g, issue, 0)

    def drain(r, carry):
        _row_copy(y_hbm.at[pl.ds(0, 1)], buf.at[0, pl.ds(0, 1)], sem).wait()
        return carry

    lax.fori_loop(0, 2 * g, drain, 0)
    info = info_ref[0]
    y = info[:, INFO_W1:INFO_W1 + 1] * buf[0] + info[:, INFO_W2:INFO_W2 + 1] * buf[1]
    o_ref[0] = x_ref[0] + mod_ref[0, 5:6, :] * y


def _combine(x, mod_l, info, y_sorted, pos1, pos2):
    bsz, seq, d = x.shape
    g = min(G_ROWS, seq)
    grid_spec = pltpu.PrefetchScalarGridSpec(
        num_scalar_prefetch=2,
        grid=(bsz, seq // g),
        in_specs=[
            pl.BlockSpec((1, g, d), lambda b, i, p1, p2: (b, i, 0)),
            pl.BlockSpec((1, N_ADA, d), lambda b, i, p1, p2: (b, 0, 0)),
            pl.BlockSpec((1, g, LANES), lambda b, i, p1, p2: (b, i, 0)),
            pl.BlockSpec(memory_space=pl.ANY),
        ],
        out_specs=pl.BlockSpec((1, g, d), lambda b, i, p1, p2: (b, i, 0)),
        scratch_shapes=[pltpu.VMEM((2, g, d), F32), pltpu.SemaphoreType.DMA],
    )
    return pl.pallas_call(
        _combine_kernel,
        grid_spec=grid_spec,
        out_shape=jax.ShapeDtypeStruct((bsz, seq, d), F32),
        compiler_params=_cparams(("arbitrary", "arbitrary")),
        name="moe_combine",
    )(pos1, pos2, x, mod_l, info, y_sorted)


def _moe(x, mod_l, g, w_router, w_gate, w_up, w_down):
    bsz, seq, d = x.shape
    n = bsz * seq
    tm = TM_MOE
    h, info, counts = _router(x, mod_l, g, w_router)
    cnt = counts[0, :N_EXPERTS].astype(jnp.int32)
    padded = (cnt + tm - 1) // tm * tm
    ends = jnp.cumsum(padded)
    starts = ends - padded
    n_rows = 2 * n + N_EXPERTS * tm
    tile_start = jnp.arange(n_rows // tm, dtype=jnp.int32) * tm
    tile_expert = jnp.minimum(jnp.sum(tile_start[:, None] >= ends[None, :], axis=1), N_EXPERTS - 1)
    n_used = (ends[-1] // tm).reshape(1).astype(jnp.int32)
    rec = info.reshape(n, LANES)
    e1, e2 = rec[:, INFO_E1].astype(jnp.int32), rec[:, INFO_E2].astype(jnp.int32)
    experts = jnp.arange(N_EXPERTS, dtype=jnp.int32)[None, :]
    pos1 = jnp.sum(jnp.where(e1[:, None] == experts, starts[None, :], 0), axis=1) + rec[:, INFO_R1].astype(jnp.int32)
    pos2 = jnp.sum(jnp.where(e2[:, None] == experts, starts[None, :], 0), axis=1) + rec[:, INFO_R2].astype(jnp.int32)
    h_sorted = _dispatch(h.reshape(n, d), pos1, pos2, n_rows)
    y_sorted = _moe_ffn(h_sorted, tile_expert.astype(jnp.int32), n_used, w_gate, w_up, w_down)
    return _combine(x, mod_l, info, y_sorted, pos1, pos2)


def _reorder_w_in(w):
    sizes = (A_WIDTH, A_HEAD_DIM, A_HEAD_DIM, I_WIDTH, IDX_DIM, IDX_HEADS, B_WIDTH, B_WIDTH,
             C_KWIDTH, C_KWIDTH, C_WIDTH, C_GATE_RANK, C_WIDTH)
    offs = [0]
    for s in sizes:
        offs.append(offs[-1] + s)
    aq, ak, av, iq, ik, iw, bu, bv, cq, ck, cv, ca, cr = (
        w[:, offs[i]:offs[i + 1]] for i in range(len(sizes)))
    pad = jnp.zeros((w.shape[0], LANES - IDX_DIM - IDX_HEADS - C_GATE_RANK), w.dtype)
    return jnp.concatenate([aq, iq, bu, bv, cq, ck, cv, cr, ak, av, ik, iw, ca, pad],
                           axis=1).astype(BF16)


def _pad_ff(w_gate, w_up, w_down):
    f = w_gate.shape[-1]
    fp = -(-f // TF_FFN) * TF_FFN
    pad_c = [(0, 0)] * (w_gate.ndim - 1) + [(0, fp - f)]
    pad_r = [(0, 0)] * (w_down.ndim - 2) + [(0, fp - f), (0, 0)]
    return (jnp.pad(w_gate.astype(BF16), pad_c), jnp.pad(w_up.astype(BF16), pad_c),
            jnp.pad(w_down.astype(BF16), pad_r))


def kernel(x, c, w_ada, b_ada, g_norm1, g_norm2, w_in, g_q, g_k, g_v_b, w_s, b_s, w_a2, b_a,
           g_out, w_out, w_ff_gate, w_ff_up, w_ff_down, w_router, w_e_gate, w_e_up, w_e_down):
    depth = w_in.shape[0]
    mod = _ada_mod(c, w_ada, b_ada)
    tables = _rope_tables(x.shape[1])
    for layer in range(depth):
        mod_l = mod[layer]
        proj = _in_proj(x, mod_l, g_norm1[layer], _reorder_w_in(w_in[layer]))
        q_r, iq_r, kvik = _dsa_prep(proj, tables, g_q[layer], g_k[layer])
        iw = proj[:, :, COL_MISC + MISC_IW:COL_MISC + MISC_IW + IDX_HEADS]
        o_a = _dsa(q_r, iq_r, iw, kvik[:, :, 0:A_HEAD_DIM], kvik[:, :, A_HEAD_DIM:2 * A_HEAD_DIM],
                   kvik[:, :, LANES:LANES + IDX_DIM], g_out[layer, :A_WIDTH])
        o_b = _sgu(proj, g_v_b[layer], w_s[layer], b_s[layer], g_out[layer, A_WIDTH:A_WIDTH + B_WIDTH])
        o_c = _gla(proj, w_a2[layer], b_a[layer], g_out[layer, A_WIDTH + B_WIDTH:])
        x = _out_proj(o_a, o_b, o_c, w_out[layer].astype(BF16), x, mod_l)
        j = layer // 2
        if layer % 2 == 0:
            x = _ffn(x, mod_l, g_norm2[layer], *_pad_ff(w_ff_gate[j], w_ff_up[j], w_ff_down[j]))
        else:
            x = _moe(x, mod_l, g_norm2[layer], w_router[j],
                     *_pad_ff(w_e_gate[j], w_e_up[j], w_e_down[j]))
    return x
```

```python
import functools

import jax
import jax.numpy as jnp
from jax import lax
from jax.experimental import pallas as pl
from jax.experimental.pallas import tpu as pltpu

F32 = jnp.float32
BF16 = jnp.bfloat16

A_HEADS = 8
A_HEAD_DIM = 64
IDX_HEADS = 8
IDX_DIM = 64
TOPK_MAX = 256
B_GROUPS = 8
B_GROUP_DIM = 64
B_CHUNK = 128
C_HEADS = 8
C_VAL_DIM = 128
C_KEY_DIM = 64
C_GATE_RANK = 16
C_GATE_TAU = 16.0
C_CHUNK = 64
ROPE_THETA = 500000.0
ROPE_DIM = 16
N_EXPERTS = 8
N_ADA = 6
EPS = 1e-6

A_WIDTH = A_HEADS * A_HEAD_DIM
I_WIDTH = IDX_HEADS * IDX_DIM
B_WIDTH = B_GROUPS * B_GROUP_DIM
C_KWIDTH = C_HEADS * C_KEY_DIM
C_WIDTH = C_HEADS * C_VAL_DIM

LANES = 128
SUBLANES = 8
VMEM_LIMIT_BYTES = 56 * 1024 * 1024

COL_AQ = 0
COL_IQ = 512
COL_BU = 1024
COL_BV = 1536
COL_CQ = 2048
COL_CK = 2560
COL_CV = 3072
COL_CR = 4096
COL_AKV = 5120
COL_MISC = 5248
N_PROJ = 5376
MISC_IW = IDX_DIM
MISC_CA = IDX_DIM + IDX_HEADS

NEG_BIG = -1e30
SAFE_LOGIT = 60.0
INT_MIN = -(2 ** 31)
I16_MIN = -(2 ** 15)

TM_MM = 512
TN_IN = 1792
DMA_UNROLL = 8
TF_FFN = 512
TN_ADA = 1024
TP_PREP = 512
Q_BLOCK = 128
K_CHUNK = 512
T_GLA = 256
TM_MOE = 512
G_ROWS = 256

INFO_E1, INFO_E2, INFO_R1, INFO_R2, INFO_W1, INFO_W2 = range(6)


def _cparams(sem):
    return pltpu.CompilerParams(dimension_semantics=sem, vmem_limit_bytes=VMEM_LIMIT_BYTES)


def _dot(a, b):
    return jnp.dot(a, b, preferred_element_type=F32)


def _dot_nt(a, b):
    return lax.dot_general(a, b, (((1,), (1,)), ((), ())), preferred_element_type=F32)


def _dot_tn(a, b):
    return lax.dot_general(a, b, (((0,), (0,)), ((), ())), preferred_element_type=F32)


def _split(x):
    hi = x.astype(BF16)
    lo = (x - hi.astype(F32)).astype(BF16)
    return hi, lo


def _dot_exact_lhs(m_bf16, x):
    hi, lo = _split(x)
    return _dot(m_bf16, hi) + _dot(m_bf16, lo)


def _dot_exact_rhs(x, m_bf16):
    hi, lo = _split(x)
    return _dot(hi, m_bf16) + _dot(lo, m_bf16)


def _dot3(a, b):
    ah, al = _split(a)
    bh, bl = _split(b)
    return _dot(ah, bh) + (_dot(al, bh) + _dot(ah, bl))


def _norm_mod(x, g, scale, shift):
    ms = jnp.mean(x * x, axis=-1, keepdims=True)
    return (x * lax.rsqrt(ms + EPS) * g) * (1.0 + scale) + shift


def _group_ones(width, group):
    r = lax.broadcasted_iota(jnp.int32, (width, width), 0) // group
    c = lax.broadcasted_iota(jnp.int32, (width, width), 1) // group
    return (r == c).astype(BF16)


def _ada_kernel(c_ref, w_ref, b_ref, o_ref):
    c = c_ref[...]
    cond = (c * jax.nn.sigmoid(c)).astype(BF16)
    o_ref[0] = _dot(cond, w_ref[0].astype(BF16)) + b_ref[0]


def _ada_mod(c, w_ada, b_ada):
    depth, d, n6 = w_ada.shape
    bsz = c.shape[0]
    rows = 16
    c_pad = jnp.pad(c, ((0, rows - bsz), (0, 0)))
    tn = TN_ADA
    out = pl.pallas_call(
        _ada_kernel,
        grid=(depth, n6 // tn),
        in_specs=[
            pl.BlockSpec((rows, d), lambda l, j: (0, 0)),
            pl.BlockSpec((1, d, tn), lambda l, j: (l, 0, j)),
            pl.BlockSpec((1, 1, tn), lambda l, j: (l, 0, j)),
        ],
        out_specs=pl.BlockSpec((1, rows, tn), lambda l, j: (l, 0, j)),
        out_shape=jax.ShapeDtypeStruct((depth, rows, n6), F32),
        compiler_params=_cparams(("parallel", "parallel")),
        name="ada_mod",
    )(c_pad, w_ada, b_ada.reshape(depth, 1, n6))
    return out[:, :bsz].reshape(depth, bsz, N_ADA, d)


def _in_proj_kernel(x_ref, mod_ref, g_ref, w_ref, o_ref, h_scr):
    @pl.when(pl.program_id(2) == 0)
    def _():
        h = _norm_mod(x_ref[0], g_ref[...], mod_ref[0, 1:2, :], mod_ref[0, 0:1, :])
        h_scr[...] = h.astype(BF16)

    o_ref[0] = _dot(h_scr[...], w_ref[...])


def _in_proj(x, mod_l, g, w_p):
    bsz, seq, d = x.shape
    n = w_p.shape[1]
    tm, tn = min(TM_MM, seq), TN_IN
    return pl.pallas_call(
        _in_proj_kernel,
        grid=(bsz, seq // tm, n // tn),
        in_specs=[
            pl.BlockSpec((1, tm, d), lambda b, i, j: (b, i, 0)),
            pl.BlockSpec((1, N_ADA, d), lambda b, i, j: (b, 0, 0)),
            pl.BlockSpec((1, d), lambda b, i, j: (0, 0)),
            pl.BlockSpec((d, tn), lambda b, i, j: (0, j)),
        ],
        out_specs=pl.BlockSpec((1, tm, tn), lambda b, i, j: (b, i, j)),
        out_shape=jax.ShapeDtypeStruct((bsz, seq, n), F32),
        scratch_shapes=[pltpu.VMEM((tm, d), BF16)],
        compiler_params=_cparams(("parallel", "parallel", "arbitrary")),
        name="in_proj",
    )(x, mod_l, g.reshape(1, d), w_p)


def _rope(x, cos, s_up, s_dn):
    parts = []
    for j in range(x.shape[1] // LANES):
        xs = x[:, j * LANES:(j + 1) * LANES]
        parts.append(xs * cos + pltpu.roll(xs, ROPE_DIM // 2, 1) * s_up
                     + pltpu.roll(xs, LANES - ROPE_DIM // 2, 1) * s_dn)
    return parts[0] if len(parts) == 1 else jnp.concatenate(parts, axis=1)


def _dsa_prep_kernel(aq_ref, iq_ref, akv_ref, misc_ref, cos_ref, sup_ref, sdn_ref, gq_ref, gk_ref,
                     q_out, iq_out, kvik_out):
    cos, s_up, s_dn = cos_ref[...], sup_ref[...], sdn_ref[...]
    aq = aq_ref[0]
    ss = _dot_exact_rhs(aq * aq, _group_ones(A_WIDTH, A_HEAD_DIM))
    qn = aq * lax.rsqrt(ss * (1.0 / A_HEAD_DIM) + EPS) * gq_ref[...]
    q_out[0] = (_rope(qn, cos, s_up, s_dn) * (A_HEAD_DIM ** -0.5)).astype(BF16)
    iq_out[0] = (_rope(iq_ref[0], cos, s_up, s_dn) * (IDX_DIM ** -0.5)).astype(BF16)
    lane = lax.broadcasted_iota(jnp.int32, (1, LANES), 1)
    first = lane < A_HEAD_DIM
    akv = akv_ref[0]
    kss = jnp.sum(jnp.where(first, akv * akv, 0.0), axis=-1, keepdims=True)
    kn = akv * lax.rsqrt(kss * (1.0 / A_HEAD_DIM) + EPS) * gk_ref[...]
    kvik_out[0, :, 0:LANES] = jnp.where(first, _rope(kn, cos, s_up, s_dn), akv).astype(BF16)
    misc = misc_ref[0]
    kvik_out[0, :, LANES:2 * LANES] = jnp.where(first, _rope(misc, cos, s_up, s_dn), misc).astype(BF16)


def _dsa_prep(proj, tables, g_q, g_k):
    bsz, seq, _ = proj.shape
    tp = min(TP_PREP, seq)
    cos, s_up, s_dn = tables
    gq = jnp.tile(g_q, A_HEADS).reshape(1, A_WIDTH)
    gk = jnp.concatenate([g_k, jnp.ones((LANES - A_HEAD_DIM,), F32)]).reshape(1, LANES)
    tab_spec = pl.BlockSpec((tp, LANES), lambda b, i: (i, 0))
    return pl.pallas_call(
        _dsa_prep_kernel,
        grid=(bsz, seq // tp),
        in_specs=[
            pl.BlockSpec((1, tp, A_WIDTH), lambda b, i: (b, i, COL_AQ // A_WIDTH)),
            pl.BlockSpec((1, tp, I_WIDTH), lambda b, i: (b, i, COL_IQ // I_WIDTH)),
            pl.BlockSpec((1, tp, LANES), lambda b, i: (b, i, COL_AKV // LANES)),
            pl.BlockSpec((1, tp, LANES), lambda b, i: (b, i, COL_MISC // LANES)),
            tab_spec, tab_spec, tab_spec,
            pl.BlockSpec((1, A_WIDTH), lambda b, i: (0, 0)),
            pl.BlockSpec((1, LANES), lambda b, i: (0, 0)),
        ],
        out_specs=[
            pl.BlockSpec((1, tp, A_WIDTH), lambda b, i: (b, i, 0)),
            pl.BlockSpec((1, tp, I_WIDTH), lambda b, i: (b, i, 0)),
            pl.BlockSpec((1, tp, 2 * LANES), lambda b, i: (b, i, 0)),
        ],
        out_shape=[
            jax.ShapeDtypeStruct((bsz, seq, A_WIDTH), BF16),
            jax.ShapeDtypeStruct((bsz, seq, I_WIDTH), BF16),
            jax.ShapeDtypeStruct((bsz, seq, 2 * LANES), BF16),
        ],
        compiler_params=_cparams(("parallel", "parallel")),
        name="dsa_prep",
    )(proj, proj, proj, proj, cos, s_up, s_dn, gq, gk)


def _rope_tables(seq):
    half = ROPE_DIM // 2
    inv_freq = ROPE_THETA ** (-jnp.arange(0, ROPE_DIM, 2, dtype=F32) / ROPE_DIM)
    ang = jnp.arange(seq, dtype=F32)[:, None] * inv_freq[None, :]
    cos, sin = jnp.cos(ang), jnp.sin(ang)
    pad = A_HEAD_DIM - ROPE_DIM
    cos64 = jnp.concatenate([cos, cos, jnp.ones((seq, pad), F32)], axis=1)
    up64 = jnp.concatenate([jnp.zeros((seq, half), F32), sin, jnp.zeros((seq, pad), F32)], axis=1)
    dn64 = jnp.concatenate([-sin, jnp.zeros((seq, half + pad), F32)], axis=1)
    rep = LANES // A_HEAD_DIM
    return jnp.tile(cos64, (1, rep)), jnp.tile(up64, (1, rep)), jnp.tile(dn64, (1, rep))


def _dsa_kernel(qt_ref, iqt_ref, iwt_ref, k_ref, vt_ref, ik_ref, got_ref, o_ref,
                key_scr, hi_scr, lo_scr, bias_scr, m_scr, l_scr, acc_scr, kn_scr, *, topk):
    qb, kcs = Q_BLOCK, key_scr.shape[1]
    blk = pl.program_id(1)
    n_chunks = (blk * qb) // kcs + 1
    krow = lax.broadcasted_iota(jnp.int32, (kcs, qb), 0)
    qcol = lax.broadcasted_iota(jnp.int32, (kcs, qb), 1)
    qpos = blk * qb + qcol
    iw = iwt_ref[0, 0] * (IDX_HEADS ** -0.5)

    def key_rows(kc):
        return pl.ds(pl.multiple_of(kc * kcs, kcs), kcs)

    def score_chunk(kc, carry):
        ks = ik_ref[0, key_rows(kc), :]
        logits = _dot(ks, iqt_ref[0, 0])
        acc = jnp.zeros((kcs, qb), F32)
        for h in range(IDX_HEADS):
            acc = acc + jnp.maximum(logits[:, h * qb:(h + 1) * qb], 0.0) * iw[h:h + 1, :]
        acc = jnp.where(acc == 0.0, 0.0, acc)
        sc = jnp.where(kc * kcs + krow <= qpos, acc, -jnp.inf)
        bits = lax.bitcast_convert_type(sc, jnp.int32)
        key = jnp.where(bits >= 0, bits, bits ^ 0x7FFFFFFF)
        key_scr[kc] = key
        hi_scr[kc] = jnp.right_shift(key, 16).astype(jnp.int16)
        return carry

    lax.fori_loop(0, n_chunks, score_chunk, 0)

    def count(pred_fn):
        def body(kc, c):
            return c + jnp.where(pred_fn(key_scr[kc]), 1.0, 0.0)
        c = lax.fori_loop(0, n_chunks, body, jnp.zeros((kcs, qb), F32))
        return jnp.sum(c, axis=0, keepdims=True)

    one16, zero16 = jnp.ones((kcs, qb), jnp.int16), jnp.zeros((kcs, qb), jnp.int16)

    def count16(ref, pred_fn, bound):
        bound16 = bound.astype(jnp.int16)

        def body(kc, c):
            return c + jnp.where(pred_fn(ref[kc], bound16), one16, zero16)
        c = lax.fori_loop(0, n_chunks, body, zero16)
        return jnp.sum(c.astype(F32), axis=0, keepdims=True)

    def search16(ref, wanted):
        ge = lambda k, b: k >= b
        zero = jnp.zeros((1, qb), jnp.int32)
        t0 = jnp.where(count16(ref, ge, zero) >= wanted, zero, jnp.full((1, qb), I16_MIN, jnp.int32))

        def bit_step(i, t):
            cand = t | jnp.left_shift(jnp.int32(1), 14 - i)
            return jnp.where(count16(ref, ge, cand) >= wanted, cand, t)

        return lax.fori_loop(0, 15, bit_step, t0)

    kf = float(topk)
    t_hi = search16(hi_scr, kf)

    def low_bits_chunk(kc, carry):
        key = key_scr[kc]
        low = (key & 0xFFFF) - 32768
        lo_scr[kc] = jnp.where(jnp.right_shift(key, 16) == t_hi, low, I16_MIN).astype(jnp.int16)
        return carry

    lax.fori_loop(0, n_chunks, low_bits_chunk, 0)
    t_lo = search16(lo_scr, kf - count16(hi_scr, lambda k, b: k > b, t_hi))
    thr = jnp.left_shift(t_hi, 16) | ((t_lo + 32768) & 0xFFFF)

    tie_overflow = jnp.max(count(lambda k: k >= thr)) > kf

    @pl.when(jnp.logical_not(tie_overflow))
    def _():
        def select_chunk(kc, carry):
            sel = (key_scr[kc] >= thr) & (kc * kcs + krow <= qpos)
            bias_scr[kc] = jnp.where(sel, 0.0, NEG_BIG)
            return carry

        lax.fori_loop(0, n_chunks, select_chunk, 0)

    @pl.when(tie_overflow)
    def _():
        need = kf - count(lambda k: k > thr)
        lower = (lax.broadcasted_iota(jnp.int32, (kcs, kcs), 1)
                 <= lax.broadcasted_iota(jnp.int32, (kcs, kcs), 0)).astype(BF16)

        def select_chunk(kc, carry):
            key = key_scr[kc]
            eq = key == thr
            eqf = jnp.where(eq, 1.0, 0.0)
            incl = _dot(lower, eqf.astype(BF16))
            sel = (key > thr) | (eq & (carry + incl - eqf < need))
            sel = sel & (kc * kcs + krow <= qpos)
            bias_scr[kc] = jnp.where(sel, 0.0, NEG_BIG)
            return carry + incl[kcs - 1:kcs, :]

        lax.fori_loop(0, n_chunks, select_chunk, jnp.zeros((1, qb), F32))

    l_scr[...] = jnp.zeros(l_scr.shape, F32)
    acc_scr[...] = jnp.zeros(acc_scr.shape, F32)

    @pl.when(blk == 0)
    def _():
        kf32 = k_ref[0].astype(F32)
        kn_scr[...] = jnp.full(kn_scr.shape, jnp.max(jnp.sum(kf32 * kf32, axis=-1, keepdims=True)))

    qf32 = qt_ref[0, 0].astype(F32)
    logit_bound_sq = jnp.max(jnp.sum(qf32 * qf32, axis=0, keepdims=True)) * kn_scr[0, 0]
    unshifted = logit_bound_sq < SAFE_LOGIT * SAFE_LOGIT

    @pl.when(unshifted)
    def _():
        def attend_chunk(kc, carry):
            kk = k_ref[0, key_rows(kc), :]
            vt = vt_ref[0, kc]
            bias = bias_scr[kc]
            s = _dot(kk, qt_ref[0, 0])
            p = jnp.concatenate([jnp.exp(s[:, h * qb:(h + 1) * qb] + bias) for h in range(A_HEADS)],
                                axis=1)
            l_scr[...] += jnp.sum(p.reshape(kcs // SUBLANES, SUBLANES, A_HEADS * qb), axis=0)
            acc_scr[...] += _dot(vt, p.astype(BF16))
            return carry

        lax.fori_loop(0, n_chunks, attend_chunk, 0)

    @pl.when(jnp.logical_not(unshifted))
    def _():
        m_scr[...] = jnp.full(m_scr.shape, NEG_BIG, F32)

        def attend_chunk(kc, carry):
            kk = k_ref[0, key_rows(kc), :]
            vt = vt_ref[0, kc]
            bias = bias_scr[kc]
            for h in range(A_HEADS):
                cols = slice(h * qb, (h + 1) * qb)
                s = _dot(kk, qt_ref[0, 0, :, cols]) + bias
                m_prev = m_scr[0:1, cols]
                m_new = jnp.maximum(m_prev, jnp.max(s, axis=0, keepdims=True))
                p = jnp.exp(s - m_new)
                alpha = jnp.exp(m_prev - m_new)
                l_scr[0:1, cols] = alpha * l_scr[0:1, cols] + jnp.sum(p, axis=0, keepdims=True)
                acc_scr[:, cols] = alpha * acc_scr[:, cols] + _dot(vt, p.astype(BF16))
                m_scr[0:1, cols] = m_new
            return carry

        lax.fori_loop(0, n_chunks, attend_chunk, 0)

    o = acc_scr[...] / jnp.sum(l_scr[...], axis=0, keepdims=True)
    o = o * lax.rsqrt(jnp.mean(o * o, axis=0, keepdims=True) + EPS)
    o_ref[0, 0] = (o * got_ref[...]).astype(BF16)


def _dsa(q_r, iq_r, iw, k_r, v, ik_r, g_out_a):
    bsz, seq, _ = q_r.shape
    qb = Q_BLOCK
    topk = min(TOPK_MAX, seq // 4)
    n_blk = seq // qb
    lanes = A_HEADS * qb

    def lanes_hq(a, heads, dim):
        a = a.reshape(bsz, n_blk, qb, heads, dim).transpose(0, 1, 4, 3, 2)
        return a.reshape(bsz, n_blk, dim, heads * qb)

    qt = lanes_hq(q_r, A_HEADS, A_HEAD_DIM)
    iqt = lanes_hq(iq_r, IDX_HEADS, IDX_DIM)
    iwt = iw.reshape(bsz, n_blk, qb, IDX_HEADS).transpose(0, 1, 3, 2)
    kcs = min(K_CHUNK, seq)
    n_kc = seq // kcs
    vt = v.reshape(bsz, n_kc, kcs, A_HEAD_DIM).transpose(0, 1, 3, 2)
    got = jnp.repeat(g_out_a.reshape(A_HEADS, A_HEAD_DIM).T, qb, axis=1)
    k_spec = pl.BlockSpec((1, seq, A_HEAD_DIM), lambda b, i: (b, 0, 0))
    q_spec = pl.BlockSpec((1, 1, A_HEAD_DIM, lanes), lambda b, i: (b, i, 0, 0))
    ot = pl.pallas_call(
        functools.partial(_dsa_kernel, topk=topk),
        grid=(bsz, n_blk),
        in_specs=[
            q_spec, q_spec,
            pl.BlockSpec((1, 1, IDX_HEADS, qb), lambda b, i: (b, i, 0, 0)),
            k_spec,
            pl.BlockSpec((1, n_kc, A_HEAD_DIM, kcs), lambda b, i: (b, 0, 0, 0)),
            k_spec,
            pl.BlockSpec((A_HEAD_DIM, lanes), lambda b, i: (0, 0)),
        ],
        out_specs=q_spec,
        out_shape=jax.ShapeDtypeStruct((bsz, n_blk, A_HEAD_DIM, lanes), BF16),
        scratch_shapes=[
            pltpu.VMEM((n_kc, kcs, qb), jnp.int32),
            pltpu.VMEM((n_kc, kcs, qb), jnp.int16),
            pltpu.VMEM((n_kc, kcs, qb), jnp.int16),
            pltpu.VMEM((n_kc, kcs, qb), F32),
            pltpu.VMEM((SUBLANES, lanes), F32),
            pltpu.VMEM((SUBLANES, lanes), F32),
            pltpu.VMEM((A_HEAD_DIM, lanes), F32),
            pltpu.VMEM((SUBLANES, LANES), F32),
        ],
        compiler_params=_cparams(("parallel", "arbitrary")),
        name="dsa_attention",
    )(qt, iqt, iwt, k_r, vt, ik_r, got)
    ot = ot.reshape(bsz, n_blk, A_HEAD_DIM, A_HEADS, qb).transpose(0, 1, 4, 3, 2)
    return ot.reshape(bsz, seq, A_WIDTH)


def _sgu_kernel(bu_ref, bv_ref, gv_ref, ws_ref, bst_ref, go_ref, o_ref):
    ch = B_CHUNK
    u = jax.nn.gelu(bu_ref[0])
    v = jax.nn.gelu(bv_ref[0])
    vc = v - jnp.mean(v, axis=-1, keepdims=True)
    vn = vc * lax.rsqrt(jnp.mean(vc * vc, axis=-1, keepdims=True) + EPS) * gv_ref[...]
    vb = vn.astype(BF16)
    row = lax.broadcasted_iota(jnp.int32, (ch, ch), 0)
    col = lax.broadcasted_iota(jnp.int32, (ch, ch), 1)
    causal = col <= row
    grp = lax.broadcasted_iota(jnp.int32, (1, B_WIDTH), 1) // B_GROUP_DIM
    bst = bst_ref[...]
    mixed = jnp.zeros((ch, B_WIDTH), F32)
    for g in range(B_GROUPS):
        w = jnp.where(causal, ws_ref[g], 0.0).astype(BF16)
        mixed = jnp.where(grp == g, _dot(w, vb) + bst[:, g:g + 1], mixed)
    o = u * mixed
    ss = _dot_exact_rhs(o * o, _group_ones(B_WIDTH, B_GROUP_DIM))
    o_ref[0] = (o * lax.rsqrt(ss * (1.0 / B_GROUP_DIM) + EPS) * go_ref[...]).astype(BF16)


def _sgu(proj, g_v, w_s, b_s, g_out_b):
    bsz, seq, _ = proj.shape
    ch = B_CHUNK
    return pl.pallas_call(
        _sgu_kernel,
        grid=(bsz, seq // ch),
        in_specs=[
            pl.BlockSpec((1, ch, B_WIDTH), lambda b, i: (b, i, COL_BU // B_WIDTH)),
            pl.BlockSpec((1, ch, B_WIDTH), lambda b, i: (b, i, COL_BV // B_WIDTH)),
            pl.BlockSpec((1, B_WIDTH), lambda b, i: (0, 0)),
            pl.BlockSpec((B_GROUPS, ch, ch), lambda b, i: (0, 0, 0)),
            pl.BlockSpec((ch, B_GROUPS), lambda b, i: (0, 0)),
            pl.BlockSpec((1, B_WIDTH), lambda b, i: (0, 0)),
        ],
        out_specs=pl.BlockSpec((1, ch, B_WIDTH), lambda b, i: (b, i, 0)),
        out_shape=jax.ShapeDtypeStruct((bsz, seq, B_WIDTH), BF16),
        compiler_params=_cparams(("parallel", "parallel")),
        name="spatial_gating",
    )(proj, proj, g_v.reshape(1, B_WIDTH), w_s, b_s.T, g_out_b.reshape(1, B_WIDTH))


def _log_sigmoid(z):
    return jnp.minimum(z, 0.0) - jnp.log1p(jnp.exp(-jnp.abs(z)))


def _gla_kernel(cq_ref, ck_ref, cv_ref, cr_ref, misc_ref, wa_ref, ba_ref, go_ref, o_ref, st_scr):
    tg, ch = cq_ref.shape[1], C_CHUNK

    @pl.when(pl.program_id(1) == 0)
    def _():
        st_scr[...] = jnp.zeros(st_scr.shape, F32)

    z = _dot3(misc_ref[0], wa_ref[...]) + ba_ref[...]
    log_a = _log_sigmoid(z) * (1.0 / C_GATE_TAU)
    r = lax.broadcasted_iota(jnp.int32, (tg, tg), 0)
    c = lax.broadcasted_iota(jnp.int32, (tg, tg), 1)
    same = (r // ch) == (c // ch)
    b = _dot_exact_lhs((same & (c <= r)).astype(BF16), log_a)
    b_last = _dot_exact_lhs(same.astype(BF16), log_a)
    ck = ck_ref[0]
    q_dec = cq_ref[0] * (C_KEY_DIM ** -0.5) * jnp.exp(b)
    k_neg = (ck * jnp.exp(-b)).astype(BF16)
    k_st = (ck * jnp.exp(b_last - b)).astype(BF16)
    cv = cv_ref[0].astype(BF16)

    lane_head = lax.broadcasted_iota(jnp.int32, (C_HEADS, 1, C_KWIDTH), 2) // C_KEY_DIM
    head_mask = lane_head == lax.broadcasted_iota(jnp.int32, (C_HEADS, 1, C_KWIDTH), 0)
    tril = (lax.broadcasted_iota(jnp.int32, (1, ch, ch), 2)
            <= lax.broadcasted_iota(jnp.int32, (1, ch, ch), 1))
    sr = lax.broadcasted_iota(jnp.int32, (C_KWIDTH, C_WIDTH), 0) // C_KEY_DIM
    sc = lax.broadcasted_iota(jnp.int32, (C_KWIDTH, C_WIDTH), 1) // C_VAL_DIM
    block_diag = sr == sc
    ones_cols = jnp.ones((ch, LANES), BF16)

    for n in range(tg // ch):
        rs = slice(n * ch, (n + 1) * ch)
        qd = q_dec[rs]
        vn = cv[rs]
        qm = jnp.where(head_mask, qd[None], 0.0).reshape(C_HEADS * ch, C_KWIDTH).astype(BF16)
        att = _dot_nt(qm, k_neg[rs]).reshape(C_HEADS, ch, ch)
        att = jnp.where(tril, att, 0.0).astype(BF16)
        o_intra = jnp.concatenate(
            [_dot(att[h], vn[:, h * C_VAL_DIM:(h + 1) * C_VAL_DIM]) for h in range(C_HEADS)], axis=1)
        state = st_scr[...]
        o = o_intra + _dot(qd.astype(BF16), state.astype(BF16))
        la_hi, la_lo = _split(log_a[rs])
        decay = jnp.exp(_dot_tn(la_hi, ones_cols) + _dot_tn(la_lo, ones_cols))[:, 0:1]
        st_scr[...] = jnp.where(block_diag, decay * state + _dot_tn(k_st[rs], vn), 0.0)
        parts = []
        for h in range(C_HEADS):
            oh = o[:, h * C_VAL_DIM:(h + 1) * C_VAL_DIM]
            parts.append(oh * lax.rsqrt(jnp.mean(oh * oh, axis=-1, keepdims=True) + EPS))
        cr = cr_ref[0, rs, :]
        o_ref[0, rs, :] = (jnp.concatenate(parts, axis=1) * (cr * jax.nn.sigmoid(cr))
                           * go_ref[...]).astype(BF16)


def _gla(proj, w_a2, b_a, g_out_c):
    bsz, seq, _ = proj.shape
    tg = min(T_GLA, seq)
    wa = jnp.zeros((LANES, C_KWIDTH), F32).at[MISC_CA:MISC_CA + C_GATE_RANK].set(w_a2)
    return pl.pallas_call(
        _gla_kernel,
        grid=(bsz, seq // tg),
        in_specs=[
            pl.BlockSpec((1, tg, C_KWIDTH), lambda b, i: (b, i, COL_CQ // C_KWIDTH)),
            pl.BlockSpec((1, tg, C_KWIDTH), lambda b, i: (b, i, COL_CK // C_KWIDTH)),
            pl.BlockSpec((1, tg, C_WIDTH), lambda b, i: (b, i, COL_CV // C_WIDTH)),
            pl.BlockSpec((1, tg, C_WIDTH), lambda b, i: (b, i, COL_CR // C_WIDTH)),
            pl.BlockSpec((1, tg, LANES), lambda b, i: (b, i, COL_MISC // LANES)),
            pl.BlockSpec((LANES, C_KWIDTH), lambda b, i: (0, 0)),
            pl.BlockSpec((1, C_KWIDTH), lambda b, i: (0, 0)),
            pl.BlockSpec((1, C_WIDTH), lambda b, i: (0, 0)),
        ],
        out_specs=pl.BlockSpec((1, tg, C_WIDTH), lambda b, i: (b, i, 0)),
        out_shape=jax.ShapeDtypeStruct((bsz, seq, C_WIDTH), BF16),
        scratch_shapes=[pltpu.VMEM((C_KWIDTH, C_WIDTH), F32)],
        compiler_params=_cparams(("parallel", "arbitrary")),
        name="gla",
    )(proj, proj, proj, proj, proj, wa, b_a.reshape(1, C_KWIDTH), g_out_c.reshape(1, C_WIDTH))


def _out_proj_kernel(oa_ref, ob_ref, oc_ref, w_ref, x_ref, mod_ref, o_ref):
    y = _dot(oa_ref[0], w_ref[0:A_WIDTH, :])
    y = y + _dot(ob_ref[0], w_ref[A_WIDTH:A_WIDTH + B_WIDTH, :])
    y = y + _dot(oc_ref[0], w_ref[A_WIDTH + B_WIDTH:, :])
    o_ref[0] = x_ref[0] + mod_ref[0, 2:3, :] * y


def _out_proj(o_a, o_b, o_c, w_out_bf16, x, mod_l):
    bsz, seq, d = x.shape
    tm = min(TM_MM, seq)
    dm = w_out_bf16.shape[0]
    return pl.pallas_call(
        _out_proj_kernel,
        grid=(bsz, seq // tm),
        in_specs=[
            pl.BlockSpec((1, tm, A_WIDTH), lambda b, i: (b, i, 0)),
            pl.BlockSpec((1, tm, B_WIDTH), lambda b, i: (b, i, 0)),
            pl.BlockSpec((1, tm, C_WIDTH), lambda b, i: (b, i, 0)),
            pl.BlockSpec((dm, d), lambda b, i: (0, 0)),
            pl.BlockSpec((1, tm, d), lambda b, i: (b, i, 0)),
            pl.BlockSpec((1, N_ADA, d), lambda b, i: (b, 0, 0)),
        ],
        out_specs=pl.BlockSpec((1, tm, d), lambda b, i: (b, i, 0)),
        out_shape=jax.ShapeDtypeStruct((bsz, seq, d), F32),
        compiler_params=_cparams(("parallel", "parallel")),
        name="out_proj",
    )(o_a, o_b, o_c, w_out_bf16, x, mod_l)


def _ffn_kernel(x_ref, mod_ref, g_ref, wg_ref, wu_ref, wd_ref, o_ref, h_scr, acc_scr, *, n_f):
    f = pl.program_id(2)

    @pl.when(f == 0)
    def _():
        h = _norm_mod(x_ref[0], g_ref[...], mod_ref[0, 4:5, :], mod_ref[0, 3:4, :])
        h_scr[...] = h.astype(BF16)
        acc_scr[...] = jnp.zeros(acc_scr.shape, F32)

    h = h_scr[...]
    gate = _dot(h, wg_ref[...])
    a = gate * jax.nn.sigmoid(gate) * _dot(h, wu_ref[...])
    acc_scr[...] += _dot(a.astype(BF16), wd_ref[...])

    @pl.when(f == n_f - 1)
    def _():
        o_ref[0] = x_ref[0] + mod_ref[0, 5:6, :] * acc_scr[...]


def _ffn(x, mod_l, g, w_gate, w_up, w_down):
    bsz, seq, d = x.shape
    tm, tf = min(TM_MM, seq), TF_FFN
    n_f = w_gate.shape[-1] // tf
    row = lambda b, i, f: (b, i, 0)
    return pl.pallas_call(
        functools.partial(_ffn_kernel, n_f=n_f),
        grid=(bsz, seq // tm, n_f),
        in_specs=[
            pl.BlockSpec((1, tm, d), row),
            pl.BlockSpec((1, N_ADA, d), lambda b, i, f: (b, 0, 0)),
            pl.BlockSpec((1, d), lambda b, i, f: (0, 0)),
            pl.BlockSpec((d, tf), lambda b, i, f: (0, f)),
            pl.BlockSpec((d, tf), lambda b, i, f: (0, f)),
            pl.BlockSpec((tf, d), lambda b, i, f: (f, 0)),
        ],
        out_specs=pl.BlockSpec((1, tm, d), row),
        out_shape=jax.ShapeDtypeStruct((bsz, seq, d), F32),
        scratch_shapes=[pltpu.VMEM((tm, d), BF16), pltpu.VMEM((tm, d), F32)],
        compiler_params=_cparams(("parallel", "parallel", "arbitrary")),
        name="dense_ffn",
    )(x, mod_l, g.reshape(1, d), w_gate, w_up, w_down)


def _router_kernel(x_ref, mod_ref, g_ref, wr_ref, h_ref, info_ref, cnt_ref, run_scr):
    @pl.when((pl.program_id(0) == 0) & (pl.program_id(1) == 0))
    def _():
        run_scr[...] = jnp.zeros(run_scr.shape, F32)

    h = _norm_mod(x_ref[0], g_ref[...], mod_ref[0, 4:5, :], mod_ref[0, 3:4, :])
    h_ref[0] = h
    logits = _dot3(h, wr_ref[...])
    tm = logits.shape[0]
    lane = lax.broadcasted_iota(jnp.int32, logits.shape, 1).astype(F32)
    logits = jnp.where(lane < N_EXPERTS, logits, -jnp.inf)
    m1 = jnp.max(logits, axis=-1, keepdims=True)
    i1 = jnp.min(jnp.where(logits == m1, lane, float(LANES)), axis=-1, keepdims=True)
    rest = jnp.where(lane == i1, -jnp.inf, logits)
    m2 = jnp.max(rest, axis=-1, keepdims=True)
    i2 = jnp.min(jnp.where(rest == m2, lane, float(LANES)), axis=-1, keepdims=True)
    e2 = jnp.exp(m2 - m1)
    den = 1.0 + e2
    hit1, hit2 = lane == i1, lane == i2
    hits = jnp.where(hit1 | hit2, 1.0, 0.0)
    earlier = (lax.broadcasted_iota(jnp.int32, (tm, tm), 1)
               < lax.broadcasted_iota(jnp.int32, (tm, tm), 0)).astype(BF16)
    rank = run_scr[0:1, :] + _dot(earlier, hits.astype(BF16))
    r1 = jnp.sum(jnp.where(hit1, rank, 0.0), axis=-1, keepdims=True)
    r2 = jnp.sum(jnp.where(hit2, rank, 0.0), axis=-1, keepdims=True)
    run_scr[0:1, :] = run_scr[0:1, :] + jnp.sum(hits, axis=0, keepdims=True)
    info = jnp.zeros(logits.shape, F32)
    for k, val in ((INFO_E1, i1), (INFO_E2, i2), (INFO_R1, r1), (INFO_R2, r2),
                   (INFO_W1, 1.0 / den), (INFO_W2, e2 / den)):
        info = jnp.where(lane == float(k), val, info)
    info_ref[0] = info
    cnt_ref[...] = jnp.broadcast_to(run_scr[0:1, :], cnt_ref.shape)


def _router(x, mod_l, g, w_router):
    bsz, seq, d = x.shape
    tm = min(TM_MM, seq)
    wr = jnp.pad(w_router, ((0, 0), (0, LANES - N_EXPERTS)))
    return pl.pallas_call(
        _router_kernel,
        grid=(bsz, seq // tm),
        in_specs=[
            pl.BlockSpec((1, tm, d), lambda b, i: (b, i, 0)),
            pl.BlockSpec((1, N_ADA, d), lambda b, i: (b, 0, 0)),
            pl.BlockSpec((1, d), lambda b, i: (0, 0)),
            pl.BlockSpec((d, LANES), lambda b, i: (0, 0)),
        ],
        out_specs=[
            pl.BlockSpec((1, tm, d), lambda b, i: (b, i, 0)),
            pl.BlockSpec((1, tm, LANES), lambda b, i: (b, i, 0)),
            pl.BlockSpec((SUBLANES, LANES), lambda b, i: (0, 0)),
        ],
        out_shape=[
            jax.ShapeDtypeStruct((bsz, seq, d), F32),
            jax.ShapeDtypeStruct((bsz, seq, LANES), F32),
            jax.ShapeDtypeStruct((SUBLANES, LANES), F32),
        ],
        scratch_shapes=[pltpu.VMEM((SUBLANES, LANES), F32)],
        compiler_params=_cparams(("arbitrary", "arbitrary")),
        name="router",
    )(x, mod_l, g.reshape(1, d), wr)


def _row_copy(src, dst, sem):
    return pltpu.make_async_copy(src, dst, sem)


def _dispatch_kernel(p1_ref, p2_ref, h_ref, zero_hbm, out_hbm, sem):
    del zero_hbm
    g = h_ref.shape[0]
    base = pl.program_id(0) * g

    def issue(r, carry):
        row = h_ref.at[pl.ds(r, 1)]
        _row_copy(row, out_hbm.at[pl.ds(p1_ref[base + r], 1)], sem).start()
        _row_copy(row, out_hbm.at[pl.ds(p2_ref[base + r], 1)], sem).start()
        return carry

    lax.fori_loop(0, g, issue, 0, unroll=DMA_UNROLL)

    def drain(r, carry):
        _row_copy(h_ref.at[pl.ds(0, 1)], out_hbm.at[pl.ds(0, 1)], sem).wait()
        return carry

    lax.fori_loop(0, 2 * g, drain, 0, unroll=DMA_UNROLL)


def _dispatch(h2d, pos1, pos2, n_rows):
    n, d = h2d.shape
    g = min(G_ROWS, n)
    grid_spec = pltpu.PrefetchScalarGridSpec(
        num_scalar_prefetch=2,
        grid=(n // g,),
        in_specs=[
            pl.BlockSpec((g, d), lambda i, p1, p2: (i, 0)),
            pl.BlockSpec(memory_space=pl.ANY),
        ],
        out_specs=pl.BlockSpec(memory_space=pl.ANY),
        scratch_shapes=[pltpu.SemaphoreType.DMA],
    )
    return pl.pallas_call(
        _dispatch_kernel,
        grid_spec=grid_spec,
        out_shape=jax.ShapeDtypeStruct((n_rows, d), F32),
        input_output_aliases={3: 0},
        compiler_params=_cparams(("arbitrary",)),
        name="moe_dispatch",
    )(pos1, pos2, h2d, jnp.zeros((n_rows, d), F32))


def _moe_ffn_kernel(te_ref, nu_ref, hs_ref, wg_ref, wu_ref, wd_ref, o_ref, h_scr, acc_scr, *, n_f):
    del te_ref
    j, f = pl.program_id(0), pl.program_id(1)
    used = j < nu_ref[0]

    @pl.when(used & (f == 0))
    def _():
        h_scr[...] = hs_ref[...].astype(BF16)
        acc_scr[...] = jnp.zeros(acc_scr.shape, F32)

    @pl.when(used)
    def _():
        h = h_scr[...]
        gate = _dot(h, wg_ref[...])
        a = gate * jax.nn.sigmoid(gate) * _dot(h, wu_ref[...])
        acc_scr[...] += _dot(a.astype(BF16), wd_ref[...])

    @pl.when(used & (f == n_f - 1))
    def _():
        o_ref[...] = acc_scr[...]

    @pl.when(jnp.logical_not(used) & (f == n_f - 1))
    def _():
        o_ref[...] = jnp.zeros(o_ref.shape, F32)


def _moe_ffn(h_sorted, tile_expert, n_used, w_gate, w_up, w_down):
    n_rows, d = h_sorted.shape
    tm, tf = TM_MOE, TF_FFN
    n_f = w_gate.shape[-1] // tf
    n_tiles = n_rows // tm

    def live(j, nu):
        return jnp.minimum(j, nu[0] - 1)

    def fcol(j, f, nu):
        return jnp.where(j < nu[0], f, n_f - 1)

    grid_spec = pltpu.PrefetchScalarGridSpec(
        num_scalar_prefetch=2,
        grid=(n_tiles, n_f),
        in_specs=[
            pl.BlockSpec((tm, d), lambda j, f, te, nu: (live(j, nu), 0)),
            pl.BlockSpec((None, d, tf), lambda j, f, te, nu: (te[live(j, nu)], 0, fcol(j, f, nu))),
            pl.BlockSpec((None, d, tf), lambda j, f, te, nu: (te[live(j, nu)], 0, fcol(j, f, nu))),
            pl.BlockSpec((None, tf, d), lambda j, f, te, nu: (te[live(j, nu)], fcol(j, f, nu), 0)),
        ],
        out_specs=pl.BlockSpec((tm, d), lambda j, f, te, nu: (j, 0)),
        scratch_shapes=[pltpu.VMEM((tm, d), BF16), pltpu.VMEM((tm, d), F32)],
    )
    return pl.pallas_call(
        functools.partial(_moe_ffn_kernel, n_f=n_f),
        grid_spec=grid_spec,
        out_shape=jax.ShapeDtypeStruct((n_rows, d), F32),
        compiler_params=_cparams(("arbitrary", "arbitrary")),
        name="moe_ffn",
    )(tile_expert, n_used, h_sorted, w_gate, w_up, w_down)


def _combine_kernel(p1_ref, p2_ref, x_ref, mod_ref, info_ref, y_hbm, o_ref, buf, sem):
    g = x_ref.shape[1]
    base = (pl.program_id(0) * pl.num_programs(1) + pl.program_id(1)) * g

    def issue(r, carry):
        _row_copy(y_hbm.at[pl.ds(p1_ref[base + r], 1)], buf.at[0, pl.ds(r, 1)], sem).start()
        _row_copy(y_hbm.at[pl.ds(p2_ref[base + r], 1)], buf.at[1, pl.ds(r, 1)], sem).start()
        return carry

    lax.fori_loop(0, g, issue, 0, unroll=DMA_UNROLL)

    def drain(r, carry):
        _row_copy(y_hbm.at[pl.ds(0, 1)], buf.at[0, pl.ds(0, 1)], sem).wait()
        return carry

    lax.fori_loop(0, 2 * g, drain, 0, unroll=DMA_UNROLL)
    info = info_ref[0]
    y = info[:, INFO_W1:INFO_W1 + 1] * buf[0] + info[:, INFO_W2:INFO_W2 + 1] * buf[1]
    o_ref[0] = x_ref[0] + mod_ref[0, 5:6, :] * y


def _combine(x, mod_l, info, y_sorted, pos1, pos2):
    bsz, seq, d = x.shape
    g = min(G_ROWS, seq)
    grid_spec = pltpu.PrefetchScalarGridSpec(
        num_scalar_prefetch=2,
        grid=(bsz, seq // g),
        in_specs=[
            pl.BlockSpec((1, g, d), lambda b, i, p1, p2: (b, i, 0)),
            pl.BlockSpec((1, N_ADA, d), lambda b, i, p1, p2: (b, 0, 0)),
            pl.BlockSpec((1, g, LANES), lambda b, i, p1, p2: (b, i, 0)),
            pl.BlockSpec(memory_space=pl.ANY),
        ],
        out_specs=pl.BlockSpec((1, g, d), lambda b, i, p1, p2: (b, i, 0)),
        scratch_shapes=[pltpu.VMEM((2, g, d), F32), pltpu.SemaphoreType.DMA],
    )
    return pl.pallas_call(
        _combine_kernel,
        grid_spec=grid_spec,
        out_shape=jax.ShapeDtypeStruct((bsz, seq, d), F32),
        compiler_params=_cparams(("arbitrary", "arbitrary")),
        name="moe_combine",
    )(pos1, pos2, x, mod_l, info, y_sorted)


def _moe(x, mod_l, g, w_router, w_gate, w_up, w_down):
    bsz, seq, d = x.shape
    n = bsz * seq
    tm = TM_MOE
    h, info, counts = _router(x, mod_l, g, w_router)
    cnt = counts[0, :N_EXPERTS].astype(jnp.int32)
    padded = (cnt + tm - 1) // tm * tm
    ends = jnp.cumsum(padded)
    starts = ends - padded
    n_rows = 2 * n + N_EXPERTS * tm
    tile_start = jnp.arange(n_rows // tm, dtype=jnp.int32) * tm
    tile_expert = jnp.minimum(jnp.sum(tile_start[:, None] >= ends[None, :], axis=1), N_EXPERTS - 1)
    n_used = (ends[-1] // tm).reshape(1).astype(jnp.int32)
    rec = info.reshape(n, LANES)
    e1, e2 = rec[:, INFO_E1].astype(jnp.int32), rec[:, INFO_E2].astype(jnp.int32)
    experts = jnp.arange(N_EXPERTS, dtype=jnp.int32)[None, :]
    pos1 = jnp.sum(jnp.where(e1[:, None] == experts, starts[None, :], 0), axis=1) + rec[:, INFO_R1].astype(jnp.int32)
    pos2 = jnp.sum(jnp.where(e2[:, None] == experts, starts[None, :], 0), axis=1) + rec[:, INFO_R2].astype(jnp.int32)
    h_sorted = _dispatch(h.reshape(n, d), pos1, pos2, n_rows)
    y_sorted = _moe_ffn(h_sorted, tile_expert.astype(jnp.int32), n_used, w_gate, w_up, w_down)
    return _combine(x, mod_l, info, y_sorted, pos1, pos2)


def _reorder_w_in(w):
    sizes = (A_WIDTH, A_HEAD_DIM, A_HEAD_DIM, I_WIDTH, IDX_DIM, IDX_HEADS, B_WIDTH, B_WIDTH,
             C_KWIDTH, C_KWIDTH, C_WIDTH, C_GATE_RANK, C_WIDTH)
    offs = [0]
    for s in sizes:
        offs.append(offs[-1] + s)
    aq, ak, av, iq, ik, iw, bu, bv, cq, ck, cv, ca, cr = (
        w[:, offs[i]:offs[i + 1]] for i in range(len(sizes)))
    pad = jnp.zeros((w.shape[0], LANES - IDX_DIM - IDX_HEADS - C_GATE_RANK), w.dtype)
    return jnp.concatenate([aq, iq, bu, bv, cq, ck, cv, cr, ak, av, ik, iw, ca, pad],
                           axis=1).astype(BF16)


def _pad_ff(w_gate, w_up, w_down):
    f = w_gate.shape[-1]
    fp = -(-f // TF_FFN) * TF_FFN
    pad_c = [(0, 0)] * (w_gate.ndim - 1) + [(0, fp - f)]
    pad_r = [(0, 0)] * (w_down.ndim - 2) + [(0, fp - f), (0, 0)]
    return (jnp.pad(w_gate.astype(BF16), pad_c), jnp.pad(w_up.astype(BF16), pad_c),
            jnp.pad(w_down.astype(BF16), pad_r))


def kernel(x, c, w_ada, b_ada, g_norm1, g_norm2, w_in, g_q, g_k, g_v_b, w_s, b_s, w_a2, b_a,
           g_out, w_out, w_ff_gate, w_ff_up, w_ff_down, w_router, w_e_gate, w_e_up, w_e_down):
    depth = w_in.shape[0]
    mod = _ada_mod(c, w_ada, b_ada)
    tables = _rope_tables(x.shape[1])
    for layer in range(depth):
        mod_l = mod[layer]
        proj = _in_proj(x, mod_l, g_norm1[layer], _reorder_w_in(w_in[layer]))
        q_r, iq_r, kvik = _dsa_prep(proj, tables, g_q[layer], g_k[layer])
        iw = proj[:, :, COL_MISC + MISC_IW:COL_MISC + MISC_IW + IDX_HEADS]
        o_a = _dsa(q_r, iq_r, iw, kvik[:, :, 0:A_HEAD_DIM], kvik[:, :, A_HEAD_DIM:2 * A_HEAD_DIM],
                   kvik[:, :, LANES:LANES + IDX_DIM], g_out[layer, :A_WIDTH])
        o_b = _sgu(proj, g_v_b[layer], w_s[layer], b_s[layer], g_out[layer, A_WIDTH:A_WIDTH + B_WIDTH])
        o_c = _gla(proj, w_a2[layer], b_a[layer], g_out[layer, A_WIDTH + B_WIDTH:])
        x = _out_proj(o_a, o_b, o_c, w_out[layer].astype(BF16), x, mod_l)
        j = layer // 2
        if layer % 2 == 0:
            x = _ffn(x, mod_l, g_norm2[layer], *_pad_ff(w_ff_gate[j], w_ff_up[j], w_ff_down[j]))
        else:
            x = _moe(x, mod_l, g_norm2[layer], w_router[j],
                     *_pad_ff(w_e_gate[j], w_e_up[j], w_e_down[j]))
    return x
```

```python
import functools

import jax
import jax.numpy as jnp
from jax import lax
from jax.experimental import pallas as pl
from jax.experimental.pallas import tpu as pltpu

F32 = jnp.float32
BF16 = jnp.bfloat16

A_HEADS = 8
A_HEAD_DIM = 64
IDX_HEADS = 8
IDX_DIM = 64
TOPK_MAX = 256
B_GROUPS = 8
B_GROUP_DIM = 64
B_CHUNK = 128
C_HEADS = 8
C_VAL_DIM = 128
C_KEY_DIM = 64
C_GATE_RANK = 16
C_GATE_TAU = 16.0
C_CHUNK = 64
ROPE_THETA = 500000.0
ROPE_DIM = 16
N_EXPERTS = 8
N_ADA = 6
EPS = 1e-6

A_WIDTH = A_HEADS * A_HEAD_DIM
I_WIDTH = IDX_HEADS * IDX_DIM
B_WIDTH = B_GROUPS * B_GROUP_DIM
C_KWIDTH = C_HEADS * C_KEY_DIM
C_WIDTH = C_HEADS * C_VAL_DIM

LANES = 128
SUBLANES = 8
VMEM_LIMIT_BYTES = 56 * 1024 * 1024

COL_AQ = 0
COL_IQ = 512
COL_BU = 1024
COL_BV = 1536
COL_CQ = 2048
COL_CK = 2560
COL_CV = 3072
COL_CR = 4096
COL_AKV = 5120
COL_MISC = 5248
N_PROJ = 5376
MISC_IW = IDX_DIM
MISC_CA = IDX_DIM + IDX_HEADS

NEG_BIG = -1e30
SAFE_LOGIT = 60.0
INT_MIN = -(2 ** 31)
COUNT_ROWS = 64

TM_MM = 512
TN_IN = 1792
DMA_UNROLL = 8
TF_FFN = 512
TN_ADA = 1024
TP_PREP = 512
Q_BLOCK = 128
K_CHUNK = 512
T_GLA = 256
TM_MOE = 512
G_ROWS = 256

INFO_E1, INFO_E2, INFO_R1, INFO_R2, INFO_W1, INFO_W2 = range(6)


def _cparams(sem):
    return pltpu.CompilerParams(dimension_semantics=sem, vmem_limit_bytes=VMEM_LIMIT_BYTES)


def _dot(a, b):
    return jnp.dot(a, b, preferred_element_type=F32)


def _dot_nt(a, b):
    return lax.dot_general(a, b, (((1,), (1,)), ((), ())), preferred_element_type=F32)


def _dot_tn(a, b):
    return lax.dot_general(a, b, (((0,), (0,)), ((), ())), preferred_element_type=F32)


def _split(x):
    hi = x.astype(BF16)
    lo = (x - hi.astype(F32)).astype(BF16)
    return hi, lo


def _dot_exact_lhs(m_bf16, x):
    hi, lo = _split(x)
    return _dot(m_bf16, hi) + _dot(m_bf16, lo)


def _dot_exact_rhs(x, m_bf16):
    hi, lo = _split(x)
    return _dot(hi, m_bf16) + _dot(lo, m_bf16)


def _dot3(a, b):
    ah, al = _split(a)
    bh, bl = _split(b)
    return _dot(ah, bh) + (_dot(al, bh) + _dot(ah, bl))


def _norm_mod(x, g, scale, shift):
    ms = jnp.mean(x * x, axis=-1, keepdims=True)
    return (x * lax.rsqrt(ms + EPS) * g) * (1.0 + scale) + shift


def _group_ones(width, group):
    r = lax.broadcasted_iota(jnp.int32, (width, width), 0) // group
    c = lax.broadcasted_iota(jnp.int32, (width, width), 1) // group
    return (r == c).astype(BF16)


def _ada_kernel(c_ref, w_ref, b_ref, o_ref):
    c = c_ref[...]
    cond = (c * jax.nn.sigmoid(c)).astype(BF16)
    o_ref[0] = _dot(cond, w_ref[0].astype(BF16)) + b_ref[0]


def _ada_mod(c, w_ada, b_ada):
    depth, d, n6 = w_ada.shape
    bsz = c.shape[0]
    rows = 16
    c_pad = jnp.pad(c, ((0, rows - bsz), (0, 0)))
    tn = TN_ADA
    out = pl.pallas_call(
        _ada_kernel,
        grid=(depth, n6 // tn),
        in_specs=[
            pl.BlockSpec((rows, d), lambda l, j: (0, 0)),
            pl.BlockSpec((1, d, tn), lambda l, j: (l, 0, j)),
            pl.BlockSpec((1, 1, tn), lambda l, j: (l, 0, j)),
        ],
        out_specs=pl.BlockSpec((1, rows, tn), lambda l, j: (l, 0, j)),
        out_shape=jax.ShapeDtypeStruct((depth, rows, n6), F32),
        compiler_params=_cparams(("parallel", "parallel")),
        name="ada_mod",
    )(c_pad, w_ada, b_ada.reshape(depth, 1, n6))
    return out[:, :bsz].reshape(depth, bsz, N_ADA, d)


def _in_proj_kernel(x_ref, mod_ref, g_ref, w_ref, o_ref, h_scr):
    @pl.when(pl.program_id(2) == 0)
    def _():
        h = _norm_mod(x_ref[0], g_ref[...], mod_ref[0, 1:2, :], mod_ref[0, 0:1, :])
        h_scr[...] = h.astype(BF16)

    o_ref[0] = _dot(h_scr[...], w_ref[...])


def _in_proj(x, mod_l, g, w_p):
    bsz, seq, d = x.shape
    n = w_p.shape[1]
    tm, tn = min(TM_MM, seq), TN_IN
    return pl.pallas_call(
        _in_proj_kernel,
        grid=(bsz, seq // tm, n // tn),
        in_specs=[
            pl.BlockSpec((1, tm, d), lambda b, i, j: (b, i, 0)),
            pl.BlockSpec((1, N_ADA, d), lambda b, i, j: (b, 0, 0)),
            pl.BlockSpec((1, d), lambda b, i, j: (0, 0)),
            pl.BlockSpec((d, tn), lambda b, i, j: (0, j)),
        ],
        out_specs=pl.BlockSpec((1, tm, tn), lambda b, i, j: (b, i, j)),
        out_shape=jax.ShapeDtypeStruct((bsz, seq, n), F32),
        scratch_shapes=[pltpu.VMEM((tm, d), BF16)],
        compiler_params=_cparams(("parallel", "parallel", "arbitrary")),
        name="in_proj",
    )(x, mod_l, g.reshape(1, d), w_p)


def _rope(x, cos, s_up, s_dn):
    parts = []
    for j in range(x.shape[1] // LANES):
        xs = x[:, j * LANES:(j + 1) * LANES]
        parts.append(xs * cos + pltpu.roll(xs, ROPE_DIM // 2, 1) * s_up
                     + pltpu.roll(xs, LANES - ROPE_DIM // 2, 1) * s_dn)
    return parts[0] if len(parts) == 1 else jnp.concatenate(parts, axis=1)


def _heads_to_lanes(x, heads):
    parts = []
    for p in range(heads // 2):
        t = x[:, p * LANES:(p + 1) * LANES].T
        parts += [t[:A_HEAD_DIM], t[A_HEAD_DIM:]]
    return jnp.concatenate(parts, axis=1)


def _dsa_prep_kernel(aq_ref, iq_ref, akv_ref, misc_ref, cos_ref, sup_ref, sdn_ref, gq_ref, gk_ref,
                     qt_out, iqt_out, k_out, ik_out, vt_out, iwt_out):
    qb = Q_BLOCK
    cos, s_up, s_dn = cos_ref[...], sup_ref[...], sdn_ref[...]
    aq = aq_ref[0]
    ss = _dot_exact_rhs(aq * aq, _group_ones(A_WIDTH, A_HEAD_DIM))
    qn = aq * lax.rsqrt(ss * (1.0 / A_HEAD_DIM) + EPS) * gq_ref[...]
    q = _rope(qn, cos, s_up, s_dn) * (A_HEAD_DIM ** -0.5)
    iq = _rope(iq_ref[0], cos, s_up, s_dn) * (IDX_DIM ** -0.5)
    lane = lax.broadcasted_iota(jnp.int32, (1, LANES), 1)
    first = lane < A_HEAD_DIM
    akv = akv_ref[0]
    kss = jnp.sum(jnp.where(first, akv * akv, 0.0), axis=-1, keepdims=True)
    kn = akv * lax.rsqrt(kss * (1.0 / A_HEAD_DIM) + EPS) * gk_ref[...]
    k_out[0] = _rope(kn, cos, s_up, s_dn)[:, :A_HEAD_DIM].astype(BF16)
    misc = misc_ref[0]
    ik_out[0] = _rope(misc, cos, s_up, s_dn)[:, :IDX_DIM].astype(BF16)
    vts = []
    for j in range(aq.shape[0] // qb):
        rows = slice(j * qb, (j + 1) * qb)
        qt_out[0, j] = _heads_to_lanes(q[rows], A_HEADS).astype(BF16)
        iqt_out[0, j] = _heads_to_lanes(iq[rows], IDX_HEADS).astype(BF16)
        vts.append(akv[rows].T[A_HEAD_DIM:])
        iwt_out[0, j] = misc[rows].T[MISC_IW:MISC_IW + IDX_HEADS]
    vt_out[0, 0] = jnp.concatenate(vts, axis=1).astype(BF16)


def _dsa_prep(proj, tables, g_q, g_k):
    bsz, seq, _ = proj.shape
    qb = Q_BLOCK
    tp = min(K_CHUNK, seq)
    nq = tp // qb
    cos, s_up, s_dn = tables
    gq = jnp.tile(g_q, A_HEADS).reshape(1, A_WIDTH)
    gk = jnp.concatenate([g_k, jnp.ones((LANES - A_HEAD_DIM,), F32)]).reshape(1, LANES)
    tab_spec = pl.BlockSpec((tp, LANES), lambda b, i: (i, 0))
    hq_spec = pl.BlockSpec((1, nq, A_HEAD_DIM, A_HEADS * qb), lambda b, i: (b, i, 0, 0))
    tok_spec = pl.BlockSpec((1, tp, A_HEAD_DIM), lambda b, i: (b, i, 0))
    return pl.pallas_call(
        _dsa_prep_kernel,
        grid=(bsz, seq // tp),
        in_specs=[
            pl.BlockSpec((1, tp, A_WIDTH), lambda b, i: (b, i, COL_AQ // A_WIDTH)),
            pl.BlockSpec((1, tp, I_WIDTH), lambda b, i: (b, i, COL_IQ // I_WIDTH)),
            pl.BlockSpec((1, tp, LANES), lambda b, i: (b, i, COL_AKV // LANES)),
            pl.BlockSpec((1, tp, LANES), lambda b, i: (b, i, COL_MISC // LANES)),
            tab_spec, tab_spec, tab_spec,
            pl.BlockSpec((1, A_WIDTH), lambda b, i: (0, 0)),
            pl.BlockSpec((1, LANES), lambda b, i: (0, 0)),
        ],
        out_specs=[
            hq_spec, hq_spec, tok_spec, tok_spec,
            pl.BlockSpec((1, 1, A_HEAD_DIM, tp), lambda b, i: (b, i, 0, 0)),
            pl.BlockSpec((1, nq, IDX_HEADS, qb), lambda b, i: (b, i, 0, 0)),
        ],
        out_shape=[
            jax.ShapeDtypeStruct((bsz, seq // qb, A_HEAD_DIM, A_HEADS * qb), BF16),
            jax.ShapeDtypeStruct((bsz, seq // qb, IDX_DIM, IDX_HEADS * qb), BF16),
            jax.ShapeDtypeStruct((bsz, seq, A_HEAD_DIM), BF16),
            jax.ShapeDtypeStruct((bsz, seq, IDX_DIM), BF16),
            jax.ShapeDtypeStruct((bsz, seq // tp, A_HEAD_DIM, tp), BF16),
            jax.ShapeDtypeStruct((bsz, seq // qb, IDX_HEADS, qb), F32),
        ],
        compiler_params=_cparams(("parallel", "parallel")),
        name="dsa_prep",
    )(proj, proj, proj, proj, cos, s_up, s_dn, gq, gk)


def _rope_tables(seq):
    half = ROPE_DIM // 2
    inv_freq = ROPE_THETA ** (-jnp.arange(0, ROPE_DIM, 2, dtype=F32) / ROPE_DIM)
    ang = jnp.arange(seq, dtype=F32)[:, None] * inv_freq[None, :]
    cos, sin = jnp.cos(ang), jnp.sin(ang)
    pad = A_HEAD_DIM - ROPE_DIM
    cos64 = jnp.concatenate([cos, cos, jnp.ones((seq, pad), F32)], axis=1)
    up64 = jnp.concatenate([jnp.zeros((seq, half), F32), sin, jnp.zeros((seq, pad), F32)], axis=1)
    dn64 = jnp.concatenate([-sin, jnp.zeros((seq, half + pad), F32)], axis=1)
    rep = LANES // A_HEAD_DIM
    return jnp.tile(cos64, (1, rep)), jnp.tile(up64, (1, rep)), jnp.tile(dn64, (1, rep))


def _dsa_kernel(qt_ref, iqt_ref, iwt_ref, k_ref, vt_ref, ik_ref, got_ref, o_ref,
                key_scr, bias_scr, m_scr, l_scr, acc_scr, kn_scr, *, topk):
    qb, kcs = Q_BLOCK, key_scr.shape[1]
    blk = pl.program_id(1)
    n_chunks = (blk * qb) // kcs + 1
    krow = lax.broadcasted_iota(jnp.int32, (kcs, qb), 0)
    qcol = lax.broadcasted_iota(jnp.int32, (kcs, qb), 1)
    qpos = blk * qb + qcol
    iw = iwt_ref[0, 0] * (IDX_HEADS ** -0.5)

    def key_rows(kc):
        return pl.ds(pl.multiple_of(kc * kcs, kcs), kcs)

    def score_chunk(kc, carry):
        ks = ik_ref[0, key_rows(kc), :]
        logits = _dot(ks, iqt_ref[0, 0])
        acc = jnp.zeros((kcs, qb), F32)
        for h in range(IDX_HEADS):
            acc = acc + jnp.maximum(logits[:, h * qb:(h + 1) * qb], 0.0) * iw[h:h + 1, :]
        acc = jnp.where(acc == 0.0, 0.0, acc)
        sc = jnp.where(kc * kcs + krow <= qpos, acc, -jnp.inf)
        bits = lax.bitcast_convert_type(sc, jnp.int32)
        key_scr[kc] = jnp.where(bits >= 0, bits, bits ^ 0x7FFFFFFF)
        return carry

    lax.fori_loop(0, n_chunks, score_chunk, 0)

    fold = min(COUNT_ROWS, kcs)

    def count(pred_fn):
        def body(kc, c):
            hit = jnp.where(pred_fn(key_scr[kc]), 1.0, 0.0)
            return c + jnp.sum(hit.reshape(kcs // fold, fold, qb), axis=0)
        c = lax.fori_loop(0, n_chunks, body, jnp.zeros((fold, qb), F32))
        return jnp.sum(c, axis=0, keepdims=True)

    kf = float(topk)
    zero = jnp.zeros((1, qb), jnp.int32)
    thr0 = jnp.where(count(lambda k: k >= zero) >= kf, zero, jnp.full((1, qb), INT_MIN, jnp.int32))

    def bit_step(i, thr):
        cand = thr | jnp.left_shift(jnp.int32(1), 30 - i)
        return jnp.where(count(lambda k: k >= cand) >= kf, cand, thr)

    thr = lax.fori_loop(0, 31, bit_step, thr0)

    tie_overflow = jnp.max(count(lambda k: k >= thr)) > kf

    @pl.when(jnp.logical_not(tie_overflow))
    def _():
        def select_chunk(kc, carry):
            sel = (key_scr[kc] >= thr) & (kc * kcs + krow <= qpos)
            bias_scr[kc] = jnp.where(sel, 0.0, NEG_BIG)
            return carry

        lax.fori_loop(0, n_chunks, select_chunk, 0)

    @pl.when(tie_overflow)
    def _():
        need = kf - count(lambda k: k > thr)
        lower = (lax.broadcasted_iota(jnp.int32, (kcs, kcs), 1)
                 <= lax.broadcasted_iota(jnp.int32, (kcs, kcs), 0)).astype(BF16)

        def select_chunk(kc, carry):
            key = key_scr[kc]
            eq = key == thr
            eqf = jnp.where(eq, 1.0, 0.0)
            incl = _dot(lower, eqf.astype(BF16))
            sel = (key > thr) | (eq & (carry + incl - eqf < need))
            sel = sel & (kc * kcs + krow <= qpos)
            bias_scr[kc] = jnp.where(sel, 0.0, NEG_BIG)
            return carry + incl[kcs - 1:kcs, :]

        lax.fori_loop(0, n_chunks, select_chunk, jnp.zeros((1, qb), F32))

    l_scr[...] = jnp.zeros(l_scr.shape, F32)
    acc_scr[...] = jnp.zeros(acc_scr.shape, F32)

    @pl.when(blk == 0)
    def _():
        kf32 = k_ref[0].astype(F32)
        kn_scr[...] = jnp.full(kn_scr.shape, jnp.max(jnp.sum(kf32 * kf32, axis=-1, keepdims=True)))

    qf32 = qt_ref[0, 0].astype(F32)
    logit_bound_sq = jnp.max(jnp.sum(qf32 * qf32, axis=0, keepdims=True)) * kn_scr[0, 0]
    unshifted = logit_bound_sq < SAFE_LOGIT * SAFE_LOGIT

    @pl.when(unshifted)
    def _():
        def attend_chunk(kc, carry):
            kk = k_ref[0, key_rows(kc), :]
            vt = vt_ref[0, kc]
            bias = bias_scr[kc]
            s = _dot(kk, qt_ref[0, 0])
            p = jnp.concatenate([jnp.exp(s[:, h * qb:(h + 1) * qb] + bias) for h in range(A_HEADS)],
                                axis=1)
            l_scr[...] += jnp.sum(p.reshape(kcs // SUBLANES, SUBLANES, A_HEADS * qb), axis=0)
            acc_scr[...] += _dot(vt, p.astype(BF16))
            return carry

        lax.fori_loop(0, n_chunks, attend_chunk, 0)

    @pl.when(jnp.logical_not(unshifted))
    def _():
        m_scr[...] = jnp.full(m_scr.shape, NEG_BIG, F32)

        def attend_chunk(kc, carry):
            kk = k_ref[0, key_rows(kc), :]
            vt = vt_ref[0, kc]
            bias = bias_scr[kc]
            for h in range(A_HEADS):
                cols = slice(h * qb, (h + 1) * qb)
                s = _dot(kk, qt_ref[0, 0, :, cols]) + bias
                m_prev = m_scr[0:1, cols]
                m_new = jnp.maximum(m_prev, jnp.max(s, axis=0, keepdims=True))
                p = jnp.exp(s - m_new)
                alpha = jnp.exp(m_prev - m_new)
                l_scr[0:1, cols] = alpha * l_scr[0:1, cols] + jnp.sum(p, axis=0, keepdims=True)
                acc_scr[:, cols] = alpha * acc_scr[:, cols] + _dot(vt, p.astype(BF16))
                m_scr[0:1, cols] = m_new
            return carry

        lax.fori_loop(0, n_chunks, attend_chunk, 0)

    o = acc_scr[...] / jnp.sum(l_scr[...], axis=0, keepdims=True)
    o = o * lax.rsqrt(jnp.mean(o * o, axis=0, keepdims=True) + EPS) * got_ref[...]
    pairs = [jnp.concatenate([o[:, 2 * p * qb:(2 * p + 1) * qb], o[:, (2 * p + 1) * qb:(2 * p + 2) * qb]],
                             axis=0).T for p in range(A_HEADS // 2)]
    o_ref[0] = jnp.concatenate(pairs, axis=1).astype(BF16)


def _dsa(qt, iqt, iwt, k_r, vt, ik_r, g_out_a):
    bsz, seq, _ = k_r.shape
    qb = Q_BLOCK
    topk = min(TOPK_MAX, seq // 4)
    n_blk = seq // qb
    lanes = A_HEADS * qb
    n_kc, kcs = vt.shape[1], vt.shape[3]
    got = jnp.repeat(g_out_a.reshape(A_HEADS, A_HEAD_DIM).T, qb, axis=1)
    k_spec = pl.BlockSpec((1, seq, A_HEAD_DIM), lambda b, i: (b, 0, 0))
    q_spec = pl.BlockSpec((1, 1, A_HEAD_DIM, lanes), lambda b, i: (b, i, 0, 0))
    return pl.pallas_call(
        functools.partial(_dsa_kernel, topk=topk),
        grid=(bsz, n_blk),
        in_specs=[
            q_spec, q_spec,
            pl.BlockSpec((1, 1, IDX_HEADS, qb), lambda b, i: (b, i, 0, 0)),
            k_spec,
            pl.BlockSpec((1, n_kc, A_HEAD_DIM, kcs), lambda b, i: (b, 0, 0, 0)),
            k_spec,
            pl.BlockSpec((A_HEAD_DIM, lanes), lambda b, i: (0, 0)),
        ],
        out_specs=pl.BlockSpec((1, qb, A_WIDTH), lambda b, i: (b, i, 0)),
        out_shape=jax.ShapeDtypeStruct((bsz, seq, A_WIDTH), BF16),
        scratch_shapes=[
            pltpu.VMEM((n_kc, kcs, qb), jnp.int32),
            pltpu.VMEM((n_kc, kcs, qb), F32),
            pltpu.VMEM((SUBLANES, lanes), F32),
            pltpu.VMEM((SUBLANES, lanes), F32),
            pltpu.VMEM((A_HEAD_DIM, lanes), F32),
            pltpu.VMEM((SUBLANES, LANES), F32),
        ],
        compiler_params=_cparams(("parallel", "arbitrary")),
        name="dsa_attention",
    )(qt, iqt, iwt, k_r, vt, ik_r, got)


def _sgu_kernel(bu_ref, bv_ref, gv_ref, ws_ref, bst_ref, go_ref, o_ref):
    ch = B_CHUNK
    u = jax.nn.gelu(bu_ref[0])
    v = jax.nn.gelu(bv_ref[0])
    vc = v - jnp.mean(v, axis=-1, keepdims=True)
    vn = vc * lax.rsqrt(jnp.mean(vc * vc, axis=-1, keepdims=True) + EPS) * gv_ref[...]
    vb = vn.astype(BF16)
    row = lax.broadcasted_iota(jnp.int32, (ch, ch), 0)
    col = lax.broadcasted_iota(jnp.int32, (ch, ch), 1)
    causal = col <= row
    grp = lax.broadcasted_iota(jnp.int32, (1, B_WIDTH), 1) // B_GROUP_DIM
    bst = bst_ref[...]
    mixed = jnp.zeros((ch, B_WIDTH), F32)
    for g in range(B_GROUPS):
        w = jnp.where(causal, ws_ref[g], 0.0).astype(BF16)
        mixed = jnp.where(grp == g, _dot(w, vb) + bst[:, g:g + 1], mixed)
    o = u * mixed
    ss = _dot_exact_rhs(o * o, _group_ones(B_WIDTH, B_GROUP_DIM))
    o_ref[0] = (o * lax.rsqrt(ss * (1.0 / B_GROUP_DIM) + EPS) * go_ref[...]).astype(BF16)


def _sgu(proj, g_v, w_s, b_s, g_out_b):
    bsz, seq, _ = proj.shape
    ch = B_CHUNK
    return pl.pallas_call(
        _sgu_kernel,
        grid=(bsz, seq // ch),
        in_specs=[
            pl.BlockSpec((1, ch, B_WIDTH), lambda b, i: (b, i, COL_BU // B_WIDTH)),
            pl.BlockSpec((1, ch, B_WIDTH), lambda b, i: (b, i, COL_BV // B_WIDTH)),
            pl.BlockSpec((1, B_WIDTH), lambda b, i: (0, 0)),
            pl.BlockSpec((B_GROUPS, ch, ch), lambda b, i: (0, 0, 0)),
            pl.BlockSpec((ch, B_GROUPS), lambda b, i: (0, 0)),
            pl.BlockSpec((1, B_WIDTH), lambda b, i: (0, 0)),
        ],
        out_specs=pl.BlockSpec((1, ch, B_WIDTH), lambda b, i: (b, i, 0)),
        out_shape=jax.ShapeDtypeStruct((bsz, seq, B_WIDTH), BF16),
        compiler_params=_cparams(("parallel", "parallel")),
        name="spatial_gating",
    )(proj, proj, g_v.reshape(1, B_WIDTH), w_s, b_s.T, g_out_b.reshape(1, B_WIDTH))


def _log_sigmoid(z):
    return jnp.minimum(z, 0.0) - jnp.log1p(jnp.exp(-jnp.abs(z)))


def _gla_kernel(cq_ref, ck_ref, cv_ref, cr_ref, misc_ref, wa_ref, ba_ref, go_ref, o_ref, st_scr):
    tg, ch = cq_ref.shape[1], C_CHUNK

    @pl.when(pl.program_id(1) == 0)
    def _():
        st_scr[...] = jnp.zeros(st_scr.shape, F32)

    z = _dot3(misc_ref[0], wa_ref[...]) + ba_ref[...]
    log_a = _log_sigmoid(z) * (1.0 / C_GATE_TAU)
    r = lax.broadcasted_iota(jnp.int32, (tg, tg), 0)
    c = lax.broadcasted_iota(jnp.int32, (tg, tg), 1)
    same = (r // ch) == (c // ch)
    b = _dot_exact_lhs((same & (c <= r)).astype(BF16), log_a)
    b_last = _dot_exact_lhs(same.astype(BF16), log_a)
    ck = ck_ref[0]
    q_dec = cq_ref[0] * (C_KEY_DIM ** -0.5) * jnp.exp(b)
    k_neg = (ck * jnp.exp(-b)).astype(BF16)
    k_st = (ck * jnp.exp(b_last - b)).astype(BF16)
    cv = cv_ref[0].astype(BF16)

    lane_head = lax.broadcasted_iota(jnp.int32, (C_HEADS, 1, C_KWIDTH), 2) // C_KEY_DIM
    head_mask = lane_head == lax.broadcasted_iota(jnp.int32, (C_HEADS, 1, C_KWIDTH), 0)
    tril = (lax.broadcasted_iota(jnp.int32, (1, ch, ch), 2)
            <= lax.broadcasted_iota(jnp.int32, (1, ch, ch), 1))
    sr = lax.broadcasted_iota(jnp.int32, (C_KWIDTH, C_WIDTH), 0) // C_KEY_DIM
    sc = lax.broadcasted_iota(jnp.int32, (C_KWIDTH, C_WIDTH), 1) // C_VAL_DIM
    block_diag = sr == sc
    ones_cols = jnp.ones((ch, LANES), BF16)

    for n in range(tg // ch):
        rs = slice(n * ch, (n + 1) * ch)
        qd = q_dec[rs]
        vn = cv[rs]
        qm = jnp.where(head_mask, qd[None], 0.0).reshape(C_HEADS * ch, C_KWIDTH).astype(BF16)
        att = _dot_nt(qm, k_neg[rs]).reshape(C_HEADS, ch, ch)
        att = jnp.where(tril, att, 0.0).astype(BF16)
        o_intra = jnp.concatenate(
            [_dot(att[h], vn[:, h * C_VAL_DIM:(h + 1) * C_VAL_DIM]) for h in range(C_HEADS)], axis=1)
        state = st_scr[...]
        o = o_intra + _dot(qd.astype(BF16), state.astype(BF16))
        la_hi, la_lo = _split(log_a[rs])
        decay = jnp.exp(_dot_tn(la_hi, ones_cols) + _dot_tn(la_lo, ones_cols))[:, 0:1]
        st_scr[...] = jnp.where(block_diag, decay * state + _dot_tn(k_st[rs], vn), 0.0)
        parts = []
        for h in range(C_HEADS):
            oh = o[:, h * C_VAL_DIM:(h + 1) * C_VAL_DIM]
            parts.append(oh * lax.rsqrt(jnp.mean(oh * oh, axis=-1, keepdims=True) + EPS))
        cr = cr_ref[0, rs, :]
        o_ref[0, rs, :] = (jnp.concatenate(parts, axis=1) * (cr * jax.nn.sigmoid(cr))
                           * go_ref[...]).astype(BF16)


def _gla(proj, w_a2, b_a, g_out_c):
    bsz, seq, _ = proj.shape
    tg = min(T_GLA, seq)
    wa = jnp.zeros((LANES, C_KWIDTH), F32).at[MISC_CA:MISC_CA + C_GATE_RANK].set(w_a2)
    return pl.pallas_call(
        _gla_kernel,
        grid=(bsz, seq // tg),
        in_specs=[
            pl.BlockSpec((1, tg, C_KWIDTH), lambda b, i: (b, i, COL_CQ // C_KWIDTH)),
            pl.BlockSpec((1, tg, C_KWIDTH), lambda b, i: (b, i, COL_CK // C_KWIDTH)),
            pl.BlockSpec((1, tg, C_WIDTH), lambda b, i: (b, i, COL_CV // C_WIDTH)),
            pl.BlockSpec((1, tg, C_WIDTH), lambda b, i: (b, i, COL_CR // C_WIDTH)),
            pl.BlockSpec((1, tg, LANES), lambda b, i: (b, i, COL_MISC // LANES)),
            pl.BlockSpec((LANES, C_KWIDTH), lambda b, i: (0, 0)),
            pl.BlockSpec((1, C_KWIDTH), lambda b, i: (0, 0)),
            pl.BlockSpec((1, C_WIDTH), lambda b, i: (0, 0)),
        ],
        out_specs=pl.BlockSpec((1, tg, C_WIDTH), lambda b, i: (b, i, 0)),
        out_shape=jax.ShapeDtypeStruct((bsz, seq, C_WIDTH), BF16),
        scratch_shapes=[pltpu.VMEM((C_KWIDTH, C_WIDTH), F32)],
        compiler_params=_cparams(("parallel", "arbitrary")),
        name="gla",
    )(proj, proj, proj, proj, proj, wa, b_a.reshape(1, C_KWIDTH), g_out_c.reshape(1, C_WIDTH))


def _out_proj_kernel(oa_ref, ob_ref, oc_ref, w_ref, x_ref, mod_ref, o_ref):
    y = _dot(oa_ref[0], w_ref[0:A_WIDTH, :])
    y = y + _dot(ob_ref[0], w_ref[A_WIDTH:A_WIDTH + B_WIDTH, :])
    y = y + _dot(oc_ref[0], w_ref[A_WIDTH + B_WIDTH:, :])
    o_ref[0] = x_ref[0] + mod_ref[0, 2:3, :] * y


def _out_proj(o_a, o_b, o_c, w_out_bf16, x, mod_l):
    bsz, seq, d = x.shape
    tm = min(TM_MM, seq)
    dm = w_out_bf16.shape[0]
    return pl.pallas_call(
        _out_proj_kernel,
        grid=(bsz, seq // tm),
        in_specs=[
            pl.BlockSpec((1, tm, A_WIDTH), lambda b, i: (b, i, 0)),
            pl.BlockSpec((1, tm, B_WIDTH), lambda b, i: (b, i, 0)),
            pl.BlockSpec((1, tm, C_WIDTH), lambda b, i: (b, i, 0)),
            pl.BlockSpec((dm, d), lambda b, i: (0, 0)),
            pl.BlockSpec((1, tm, d), lambda b, i: (b, i, 0)),
            pl.BlockSpec((1, N_ADA, d), lambda b, i: (b, 0, 0)),
        ],
        out_specs=pl.BlockSpec((1, tm, d), lambda b, i: (b, i, 0)),
        out_shape=jax.ShapeDtypeStruct((bsz, seq, d), F32),
        compiler_params=_cparams(("parallel", "parallel")),
        name="out_proj",
    )(o_a, o_b, o_c, w_out_bf16, x, mod_l)


def _swiglu_step(h, wg_ref, wu_ref, wd_ref, acc_scr, f, n_f, f_tail):
    gate = _dot(h, wg_ref[...])
    a = gate * jax.nn.sigmoid(gate) * _dot(h, wu_ref[...])
    tf = a.shape[1]
    if f_tail == tf:
        acc_scr[...] += _dot(a.astype(BF16), wd_ref[...])
        return

    @pl.when(f < n_f - 1)
    def _():
        acc_scr[...] += _dot(a.astype(BF16), wd_ref[...])

    @pl.when(f == n_f - 1)
    def _():
        col = lax.broadcasted_iota(jnp.int32, (1, tf), 1)
        row = lax.broadcasted_iota(jnp.int32, (tf, 1), 0)
        wd = wd_ref[...]
        acc_scr[...] += _dot(jnp.where(col < f_tail, a, 0.0).astype(BF16),
                             jnp.where(row < f_tail, wd, jnp.zeros_like(wd)))


def _hidden_tiles(f_width):
    n_f = pl.cdiv(f_width, TF_FFN)
    return n_f, f_width - (n_f - 1) * TF_FFN


def _ffn_kernel(x_ref, mod_ref, g_ref, wg_ref, wu_ref, wd_ref, o_ref, h_scr, acc_scr, *, n_f, f_tail):
    f = pl.program_id(2)

    @pl.when(f == 0)
    def _():
        h = _norm_mod(x_ref[0], g_ref[...], mod_ref[0, 4:5, :], mod_ref[0, 3:4, :])
        h_scr[...] = h.astype(BF16)
        acc_scr[...] = jnp.zeros(acc_scr.shape, F32)

    _swiglu_step(h_scr[...], wg_ref, wu_ref, wd_ref, acc_scr, f, n_f, f_tail)

    @pl.when(f == n_f - 1)
    def _():
        o_ref[0] = x_ref[0] + mod_ref[0, 5:6, :] * acc_scr[...]


def _ffn(x, mod_l, g, w_gate, w_up, w_down):
    bsz, seq, d = x.shape
    tm, tf = min(TM_MM, seq), TF_FFN
    n_f, f_tail = _hidden_tiles(w_gate.shape[-1])
    row = lambda b, i, f: (b, i, 0)
    return pl.pallas_call(
        functools.partial(_ffn_kernel, n_f=n_f, f_tail=f_tail),
        grid=(bsz, seq // tm, n_f),
        in_specs=[
            pl.BlockSpec((1, tm, d), row),
            pl.BlockSpec((1, N_ADA, d), lambda b, i, f: (b, 0, 0)),
            pl.BlockSpec((1, d), lambda b, i, f: (0, 0)),
            pl.BlockSpec((d, tf), lambda b, i, f: (0, f)),
            pl.BlockSpec((d, tf), lambda b, i, f: (0, f)),
            pl.BlockSpec((tf, d), lambda b, i, f: (f, 0)),
        ],
        out_specs=pl.BlockSpec((1, tm, d), row),
        out_shape=jax.ShapeDtypeStruct((bsz, seq, d), F32),
        scratch_shapes=[pltpu.VMEM((tm, d), BF16), pltpu.VMEM((tm, d), F32)],
        compiler_params=_cparams(("parallel", "parallel", "arbitrary")),
        name="dense_ffn",
    )(x, mod_l, g.reshape(1, d), w_gate, w_up, w_down)


def _router_kernel(x_ref, mod_ref, g_ref, wr_ref, h_ref, info_ref, cnt_ref, run_scr):
    @pl.when((pl.program_id(0) == 0) & (pl.program_id(1) == 0))
    def _():
        run_scr[...] = jnp.zeros(run_scr.shape, F32)

    h = _norm_mod(x_ref[0], g_ref[...], mod_ref[0, 4:5, :], mod_ref[0, 3:4, :])
    h_ref[0] = h
    logits = _dot3(h, wr_ref[...])
    tm = logits.shape[0]
    lane = lax.broadcasted_iota(jnp.int32, logits.shape, 1).astype(F32)
    logits = jnp.where(lane < N_EXPERTS, logits, -jnp.inf)
    m1 = jnp.max(logits, axis=-1, keepdims=True)
    i1 = jnp.min(jnp.where(logits == m1, lane, float(LANES)), axis=-1, keepdims=True)
    rest = jnp.where(lane == i1, -jnp.inf, logits)
    m2 = jnp.max(rest, axis=-1, keepdims=True)
    i2 = jnp.min(jnp.where(rest == m2, lane, float(LANES)), axis=-1, keepdims=True)
    e2 = jnp.exp(m2 - m1)
    den = 1.0 + e2
    hit1, hit2 = lane == i1, lane == i2
    hits = jnp.where(hit1 | hit2, 1.0, 0.0)
    earlier = (lax.broadcasted_iota(jnp.int32, (tm, tm), 1)
               < lax.broadcasted_iota(jnp.int32, (tm, tm), 0)).astype(BF16)
    rank = run_scr[0:1, :] + _dot(earlier, hits.astype(BF16))
    r1 = jnp.sum(jnp.where(hit1, rank, 0.0), axis=-1, keepdims=True)
    r2 = jnp.sum(jnp.where(hit2, rank, 0.0), axis=-1, keepdims=True)
    run_scr[0:1, :] = run_scr[0:1, :] + jnp.sum(hits, axis=0, keepdims=True)
    info = jnp.zeros(logits.shape, F32)
    for k, val in ((INFO_E1, i1), (INFO_E2, i2), (INFO_R1, r1), (INFO_R2, r2),
                   (INFO_W1, 1.0 / den), (INFO_W2, e2 / den)):
        info = jnp.where(lane == float(k), val, info)
    info_ref[0] = info
    cnt_ref[...] = jnp.broadcast_to(run_scr[0:1, :], cnt_ref.shape)


def _router(x, mod_l, g, w_router):
    bsz, seq, d = x.shape
    tm = min(TM_MM, seq)
    wr = jnp.pad(w_router, ((0, 0), (0, LANES - N_EXPERTS)))
    return pl.pallas_call(
        _router_kernel,
        grid=(bsz, seq // tm),
        in_specs=[
            pl.BlockSpec((1, tm, d), lambda b, i: (b, i, 0)),
            pl.BlockSpec((1, N_ADA, d), lambda b, i: (b, 0, 0)),
            pl.BlockSpec((1, d), lambda b, i: (0, 0)),
            pl.BlockSpec((d, LANES), lambda b, i: (0, 0)),
        ],
        out_specs=[
            pl.BlockSpec((1, tm, d), lambda b, i: (b, i, 0)),
            pl.BlockSpec((1, tm, LANES), lambda b, i: (b, i, 0)),
            pl.BlockSpec((SUBLANES, LANES), lambda b, i: (0, 0)),
        ],
        out_shape=[
            jax.ShapeDtypeStruct((bsz, seq, d), F32),
            jax.ShapeDtypeStruct((bsz, seq, LANES), F32),
            jax.ShapeDtypeStruct((SUBLANES, LANES), F32),
        ],
        scratch_shapes=[pltpu.VMEM((SUBLANES, LANES), F32)],
        compiler_params=_cparams(("arbitrary", "arbitrary")),
        name="router",
    )(x, mod_l, g.reshape(1, d), wr)


def _row_copy(src, dst, sem):
    return pltpu.make_async_copy(src, dst, sem)


def _dispatch_kernel(p1_ref, p2_ref, h_ref, zero_hbm, out_hbm, sem):
    del zero_hbm
    g = h_ref.shape[0]
    base = pl.program_id(0) * g

    def issue(r, carry):
        row = h_ref.at[pl.ds(r, 1)]
        _row_copy(row, out_hbm.at[pl.ds(p1_ref[base + r], 1)], sem).start()
        _row_copy(row, out_hbm.at[pl.ds(p2_ref[base + r], 1)], sem).start()
        return carry

    lax.fori_loop(0, g, issue, 0, unroll=DMA_UNROLL)

    def drain(r, carry):
        _row_copy(h_ref.at[pl.ds(0, 1)], out_hbm.at[pl.ds(0, 1)], sem).wait()
        return carry

    lax.fori_loop(0, 2 * g, drain, 0, unroll=DMA_UNROLL)


def _dispatch(h2d, pos1, pos2, n_rows):
    n, d = h2d.shape
    g = min(G_ROWS, n)
    grid_spec = pltpu.PrefetchScalarGridSpec(
        num_scalar_prefetch=2,
        grid=(n // g,),
        in_specs=[
            pl.BlockSpec((g, d), lambda i, p1, p2: (i, 0)),
            pl.BlockSpec(memory_space=pl.ANY),
        ],
        out_specs=pl.BlockSpec(memory_space=pl.ANY),
        scratch_shapes=[pltpu.SemaphoreType.DMA],
    )
    return pl.pallas_call(
        _dispatch_kernel,
        grid_spec=grid_spec,
        out_shape=jax.ShapeDtypeStruct((n_rows, d), F32),
        input_output_aliases={3: 0},
        compiler_params=_cparams(("arbitrary",)),
        name="moe_dispatch",
    )(pos1, pos2, h2d, jnp.zeros((n_rows, d), F32))


def _moe_ffn_kernel(te_ref, nu_ref, hs_ref, wg_ref, wu_ref, wd_ref, o_ref, h_scr, acc_scr, *, n_f,
                    f_tail):
    del te_ref
    j, f = pl.program_id(0), pl.program_id(1)
    used = j < nu_ref[0]

    @pl.when(used & (f == 0))
    def _():
        h_scr[...] = hs_ref[...].astype(BF16)
        acc_scr[...] = jnp.zeros(acc_scr.shape, F32)

    @pl.when(used)
    def _():
        _swiglu_step(h_scr[...], wg_ref, wu_ref, wd_ref, acc_scr, f, n_f, f_tail)

    @pl.when(used & (f == n_f - 1))
    def _():
        o_ref[...] = acc_scr[...]

    @pl.when(jnp.logical_not(used) & (f == n_f - 1))
    def _():
        o_ref[...] = jnp.zeros(o_ref.shape, F32)


def _moe_ffn(h_sorted, tile_expert, n_used, w_gate, w_up, w_down):
    n_rows, d = h_sorted.shape
    tm, tf = TM_MOE, TF_FFN
    n_f, f_tail = _hidden_tiles(w_gate.shape[-1])
    n_tiles = n_rows // tm

    def live(j, nu):
        return jnp.minimum(j, nu[0] - 1)

    def fcol(j, f, nu):
        return jnp.where(j < nu[0], f, n_f - 1)

    grid_spec = pltpu.PrefetchScalarGridSpec(
        num_scalar_prefetch=2,
        grid=(n_tiles, n_f),
        in_specs=[
            pl.BlockSpec((tm, d), lambda j, f, te, nu: (live(j, nu), 0)),
            pl.BlockSpec((None, d, tf), lambda j, f, te, nu: (te[live(j, nu)], 0, fcol(j, f, nu))),
            pl.BlockSpec((None, d, tf), lambda j, f, te, nu: (te[live(j, nu)], 0, fcol(j, f, nu))),
            pl.BlockSpec((None, tf, d), lambda j, f, te, nu: (te[live(j, nu)], fcol(j, f, nu), 0)),
        ],
        out_specs=pl.BlockSpec((tm, d), lambda j, f, te, nu: (j, 0)),
        scratch_shapes=[pltpu.VMEM((tm, d), BF16), pltpu.VMEM((tm, d), F32)],
    )
    return pl.pallas_call(
        functools.partial(_moe_ffn_kernel, n_f=n_f, f_tail=f_tail),
        grid_spec=grid_spec,
        out_shape=jax.ShapeDtypeStruct((n_rows, d), F32),
        compiler_params=_cparams(("arbitrary", "arbitrary")),
        name="moe_ffn",
    )(tile_expert, n_used, h_sorted, w_gate, w_up, w_down)


def _combine_kernel(p1_ref, p2_ref, x_ref, mod_ref, info_ref, y_hbm, o_ref, buf, sem):
    g = x_ref.shape[1]
    base = (pl.program_id(0) * pl.num_programs(1) + pl.program_id(1)) * g

    def issue(r, carry):
        _row_copy(y_hbm.at[pl.ds(p1_ref[base + r], 1)], buf.at[0, pl.ds(r, 1)], sem).start()
        _row_copy(y_hbm.at[pl.ds(p2_ref[base + r], 1)], buf.at[1, pl.ds(r, 1)], sem).start()
        return carry

    lax.fori_loop(0, g, issue, 0, unroll=DMA_UNROLL)

    def drain(r, carry):
        _row_copy(y_hbm.at[pl.ds(0, 1)], buf.at[0, pl.ds(0, 1)], sem).wait()
        return carry

    lax.fori_loop(0, 2 * g, drain, 0, unroll=DMA_UNROLL)
    info = info_ref[0]
    y = info[:, INFO_W1:INFO_W1 + 1] * buf[0] + info[:, INFO_W2:INFO_W2 + 1] * buf[1]
    o_ref[0] = x_ref[0] + mod_ref[0, 5:6, :] * y


def _combine(x, mod_l, info, y_sorted, pos1, pos2):
    bsz, seq, d = x.shape
    g = min(G_ROWS, seq)
    grid_spec = pltpu.PrefetchScalarGridSpec(
        num_scalar_prefetch=2,
        grid=(bsz, seq // g),
        in_specs=[
            pl.BlockSpec((1, g, d), lambda b, i, p1, p2: (b, i, 0)),
            pl.BlockSpec((1, N_ADA, d), lambda b, i, p1, p2: (b, 0, 0)),
            pl.BlockSpec((1, g, LANES), lambda b, i, p1, p2: (b, i, 0)),
            pl.BlockSpec(memory_space=pl.ANY),
        ],
        out_specs=pl.BlockSpec((1, g, d), lambda b, i, p1, p2: (b, i, 0)),
        scratch_shapes=[pltpu.VMEM((2, g, d), F32), pltpu.SemaphoreType.DMA],
    )
    return pl.pallas_call(
        _combine_kernel,
        grid_spec=grid_spec,
        out_shape=jax.ShapeDtypeStruct((bsz, seq, d), F32),
        compiler_params=_cparams(("arbitrary", "arbitrary")),
        name="moe_combine",
    )(pos1, pos2, x, mod_l, info, y_sorted)


def _moe(x, mod_l, g, w_router, first_expert, w_gate, w_up, w_down):
    bsz, seq, d = x.shape
    n = bsz * seq
    tm = TM_MOE
    h, info, counts = _router(x, mod_l, g, w_router)
    cnt = counts[0, :N_EXPERTS].astype(jnp.int32)
    padded = (cnt + tm - 1) // tm * tm
    ends = jnp.cumsum(padded)
    starts = ends - padded
    n_rows = 2 * n + N_EXPERTS * tm
    tile_start = jnp.arange(n_rows // tm, dtype=jnp.int32) * tm
    tile_expert = jnp.minimum(jnp.sum(tile_start[:, None] >= ends[None, :], axis=1), N_EXPERTS - 1)
    n_used = (ends[-1] // tm).reshape(1).astype(jnp.int32)
    rec = info.reshape(n, LANES)
    e1, e2 = rec[:, INFO_E1].astype(jnp.int32), rec[:, INFO_E2].astype(jnp.int32)
    experts = jnp.arange(N_EXPERTS, dtype=jnp.int32)[None, :]
    pos1 = jnp.sum(jnp.where(e1[:, None] == experts, starts[None, :], 0), axis=1) + rec[:, INFO_R1].astype(jnp.int32)
    pos2 = jnp.sum(jnp.where(e2[:, None] == experts, starts[None, :], 0), axis=1) + rec[:, INFO_R2].astype(jnp.int32)
    h_sorted = _dispatch(h.reshape(n, d), pos1, pos2, n_rows)
    y_sorted = _moe_ffn(h_sorted, (tile_expert + first_expert).astype(jnp.int32), n_used,
                        w_gate, w_up, w_down)
    return _combine(x, mod_l, info, y_sorted, pos1, pos2)


def _reorder_w_in(w):
    sizes = (A_WIDTH, A_HEAD_DIM, A_HEAD_DIM, I_WIDTH, IDX_DIM, IDX_HEADS, B_WIDTH, B_WIDTH,
             C_KWIDTH, C_KWIDTH, C_WIDTH, C_GATE_RANK, C_WIDTH)
    offs = [0]
    for s in sizes:
        offs.append(offs[-1] + s)
    aq, ak, av, iq, ik, iw, bu, bv, cq, ck, cv, ca, cr = (
        w[:, offs[i]:offs[i + 1]] for i in range(len(sizes)))
    pad = jnp.zeros((w.shape[0], LANES - IDX_DIM - IDX_HEADS - C_GATE_RANK), w.dtype)
    return jnp.concatenate([aq, iq, bu, bv, cq, ck, cv, cr, ak, av, ik, iw, ca, pad],
                           axis=1).astype(BF16)


def _stack_experts(w):
    return w.astype(BF16).reshape((w.shape[0] * w.shape[1],) + w.shape[2:])


def kernel(x, c, w_ada, b_ada, g_norm1, g_norm2, w_in, g_q, g_k, g_v_b, w_s, b_s, w_a2, b_a,
           g_out, w_out, w_ff_gate, w_ff_up, w_ff_down, w_router, w_e_gate, w_e_up, w_e_down):
    depth = w_in.shape[0]
    mod = _ada_mod(c, w_ada, b_ada)
    tables = _rope_tables(x.shape[1])
    expert_w = tuple(_stack_experts(w) for w in (w_e_gate, w_e_up, w_e_down))
    for layer in range(depth):
        mod_l = mod[layer]
        proj = _in_proj(x, mod_l, g_norm1[layer], _reorder_w_in(w_in[layer]))
        qt, iqt, k_r, ik_r, vt, iwt = _dsa_prep(proj, tables, g_q[layer], g_k[layer])
        o_a = _dsa(qt, iqt, iwt, k_r, vt, ik_r, g_out[layer, :A_WIDTH])
        o_b = _sgu(proj, g_v_b[layer], w_s[layer], b_s[layer], g_out[layer, A_WIDTH:A_WIDTH + B_WIDTH])
        o_c = _gla(proj, w_a2[layer], b_a[layer], g_out[layer, A_WIDTH + B_WIDTH:])
        x = _out_proj(o_a, o_b, o_c, w_out[layer].astype(BF16), x, mod_l)
        j = layer // 2
        if layer % 2 == 0:
            x = _ffn(x, mod_l, g_norm2[layer], w_ff_gate[j].astype(BF16), w_ff_up[j].astype(BF16),
                     w_ff_down[j].astype(BF16))
        else:
            x = _moe(x, mod_l, g_norm2[layer], w_router[j], j * N_EXPERTS, *expert_w)
    return x
```

```python
import functools

import jax
import jax.numpy as jnp
from jax import lax
from jax.experimental import pallas as pl
from jax.experimental.pallas import tpu as pltpu

F32 = jnp.float32
BF16 = jnp.bfloat16

A_HEADS = 8
A_HEAD_DIM = 64
IDX_HEADS = 8
IDX_DIM = 64
TOPK_MAX = 256
B_GROUPS = 8
B_GROUP_DIM = 64
B_CHUNK = 128
C_HEADS = 8
C_VAL_DIM = 128
C_KEY_DIM = 64
C_GATE_RANK = 16
C_GATE_TAU = 16.0
C_CHUNK = 64
ROPE_THETA = 500000.0
ROPE_DIM = 16
N_EXPERTS = 8
N_ADA = 6
EPS = 1e-6

A_WIDTH = A_HEADS * A_HEAD_DIM
I_WIDTH = IDX_HEADS * IDX_DIM
B_WIDTH = B_GROUPS * B_GROUP_DIM
C_KWIDTH = C_HEADS * C_KEY_DIM
C_WIDTH = C_HEADS * C_VAL_DIM

LANES = 128
SUBLANES = 8
VMEM_LIMIT_BYTES = 56 * 1024 * 1024

COL_AQ = 0
COL_IQ = 512
COL_BU = 1024
COL_BV = 1536
COL_CQ = 2048
COL_CK = 2560
COL_CV = 3072
COL_CR = 4096
COL_AKV = 5120
COL_MISC = 5248
N_PROJ = 5376
MISC_IW = IDX_DIM
MISC_CA = IDX_DIM + IDX_HEADS

NEG_BIG = -1e30
SAFE_LOGIT = 60.0
INT_MIN = -(2 ** 31)
COUNT_ROWS = 64

TM_MM = 512
TN_IN = 1792
DMA_UNROLL = 8
TF_FFN = 1024
TN_ADA = 1024
TP_PREP = 512
Q_BLOCK = 128
K_CHUNK = 512
T_GLA = 256
TM_MOE = 512
G_ROWS = 256

INFO_E1, INFO_E2, INFO_R1, INFO_R2, INFO_W1, INFO_W2 = range(6)


def _cparams(sem):
    return pltpu.CompilerParams(dimension_semantics=sem, vmem_limit_bytes=VMEM_LIMIT_BYTES)


def _dot(a, b):
    return jnp.dot(a, b, preferred_element_type=F32)


def _dot_nt(a, b):
    return lax.dot_general(a, b, (((1,), (1,)), ((), ())), preferred_element_type=F32)


def _dot_tn(a, b):
    return lax.dot_general(a, b, (((0,), (0,)), ((), ())), preferred_element_type=F32)


def _split(x):
    hi = x.astype(BF16)
    lo = (x - hi.astype(F32)).astype(BF16)
    return hi, lo


def _dot_exact_lhs(m_bf16, x):
    hi, lo = _split(x)
    return _dot(m_bf16, hi) + _dot(m_bf16, lo)


def _dot_exact_rhs(x, m_bf16):
    hi, lo = _split(x)
    return _dot(hi, m_bf16) + _dot(lo, m_bf16)


def _dot3(a, b):
    ah, al = _split(a)
    bh, bl = _split(b)
    return _dot(ah, bh) + (_dot(al, bh) + _dot(ah, bl))


def _norm_mod(x, g, scale, shift):
    ms = jnp.mean(x * x, axis=-1, keepdims=True)
    return (x * lax.rsqrt(ms + EPS) * g) * (1.0 + scale) + shift


def _group_ones(width, group):
    r = lax.broadcasted_iota(jnp.int32, (width, width), 0) // group
    c = lax.broadcasted_iota(jnp.int32, (width, width), 1) // group
    return (r == c).astype(BF16)


def _ada_kernel(c_ref, w_ref, b_ref, o_ref):
    c = c_ref[...]
    cond = (c * jax.nn.sigmoid(c)).astype(BF16)
    o_ref[0] = _dot(cond, w_ref[0].astype(BF16)) + b_ref[0]


def _ada_mod(c, w_ada, b_ada):
    depth, d, n6 = w_ada.shape
    bsz = c.shape[0]
    rows = 16
    c_pad = jnp.pad(c, ((0, rows - bsz), (0, 0)))
    tn = TN_ADA
    out = pl.pallas_call(
        _ada_kernel,
        grid=(depth, n6 // tn),
        in_specs=[
            pl.BlockSpec((rows, d), lambda l, j: (0, 0)),
            pl.BlockSpec((1, d, tn), lambda l, j: (l, 0, j)),
            pl.BlockSpec((1, 1, tn), lambda l, j: (l, 0, j)),
        ],
        out_specs=pl.BlockSpec((1, rows, tn), lambda l, j: (l, 0, j)),
        out_shape=jax.ShapeDtypeStruct((depth, rows, n6), F32),
        compiler_params=_cparams(("parallel", "parallel")),
        name="ada_mod",
    )(c_pad, w_ada, b_ada.reshape(depth, 1, n6))
    return out[:, :bsz].reshape(depth, bsz, N_ADA, d)


def _in_proj_kernel(x_ref, mod_ref, g_ref, w_ref, o_ref, h_scr):
    @pl.when(pl.program_id(2) == 0)
    def _():
        h = _norm_mod(x_ref[0], g_ref[...], mod_ref[0, 1:2, :], mod_ref[0, 0:1, :])
        h_scr[...] = h.astype(BF16)

    o_ref[0] = _dot(h_scr[...], w_ref[...])


def _in_proj(x, mod_l, g, w_p):
    bsz, seq, d = x.shape
    n = w_p.shape[1]
    tm, tn = min(TM_MM, seq), TN_IN
    return pl.pallas_call(
        _in_proj_kernel,
        grid=(bsz, seq // tm, n // tn),
        in_specs=[
            pl.BlockSpec((1, tm, d), lambda b, i, j: (b, i, 0)),
            pl.BlockSpec((1, N_ADA, d), lambda b, i, j: (b, 0, 0)),
            pl.BlockSpec((1, d), lambda b, i, j: (0, 0)),
            pl.BlockSpec((d, tn), lambda b, i, j: (0, j)),
        ],
        out_specs=pl.BlockSpec((1, tm, tn), lambda b, i, j: (b, i, j)),
        out_shape=jax.ShapeDtypeStruct((bsz, seq, n), F32),
        scratch_shapes=[pltpu.VMEM((tm, d), BF16)],
        compiler_params=_cparams(("parallel", "parallel", "arbitrary")),
        name="in_proj",
    )(x, mod_l, g.reshape(1, d), w_p)


def _rope(x, cos, s_up, s_dn):
    parts = []
    for j in range(x.shape[1] // LANES):
        xs = x[:, j * LANES:(j + 1) * LANES]
        parts.append(xs * cos + pltpu.roll(xs, ROPE_DIM // 2, 1) * s_up
                     + pltpu.roll(xs, LANES - ROPE_DIM // 2, 1) * s_dn)
    return parts[0] if len(parts) == 1 else jnp.concatenate(parts, axis=1)


def _heads_to_lanes(x, heads):
    parts = []
    for p in range(heads // 2):
        t = x[:, p * LANES:(p + 1) * LANES].T
        parts += [t[:A_HEAD_DIM], t[A_HEAD_DIM:]]
    return jnp.concatenate(parts, axis=1)


def _dsa_prep_kernel(aq_ref, iq_ref, akv_ref, misc_ref, cos_ref, sup_ref, sdn_ref, gq_ref, gk_ref,
                     qt_out, iqt_out, k_out, ik_out, vt_out, iwt_out):
    qb = Q_BLOCK
    cos, s_up, s_dn = cos_ref[...], sup_ref[...], sdn_ref[...]
    aq = aq_ref[0]
    ss = _dot_exact_rhs(aq * aq, _group_ones(A_WIDTH, A_HEAD_DIM))
    qn = aq * lax.rsqrt(ss * (1.0 / A_HEAD_DIM) + EPS) * gq_ref[...]
    q = _rope(qn, cos, s_up, s_dn) * (A_HEAD_DIM ** -0.5)
    iq = _rope(iq_ref[0], cos, s_up, s_dn) * (IDX_DIM ** -0.5)
    lane = lax.broadcasted_iota(jnp.int32, (1, LANES), 1)
    first = lane < A_HEAD_DIM
    akv = akv_ref[0]
    kss = jnp.sum(jnp.where(first, akv * akv, 0.0), axis=-1, keepdims=True)
    kn = akv * lax.rsqrt(kss * (1.0 / A_HEAD_DIM) + EPS) * gk_ref[...]
    k_out[0] = _rope(kn, cos, s_up, s_dn)[:, :A_HEAD_DIM].astype(BF16)
    misc = misc_ref[0]
    ik_out[0] = _rope(misc, cos, s_up, s_dn)[:, :IDX_DIM].astype(BF16)
    vts = []
    for j in range(aq.shape[0] // qb):
        rows = slice(j * qb, (j + 1) * qb)
        qt_out[0, j] = _heads_to_lanes(q[rows], A_HEADS).astype(BF16)
        iqt_out[0, j] = _heads_to_lanes(iq[rows], IDX_HEADS).astype(BF16)
        vts.append(akv[rows].T[A_HEAD_DIM:])
        iwt_out[0, j] = misc[rows].T[MISC_IW:MISC_IW + IDX_HEADS]
    vt_out[0, 0] = jnp.concatenate(vts, axis=1).astype(BF16)


def _dsa_prep(proj, tables, g_q, g_k):
    bsz, seq, _ = proj.shape
    qb = Q_BLOCK
    tp = min(K_CHUNK, seq)
    nq = tp // qb
    cos, s_up, s_dn = tables
    gq = jnp.tile(g_q, A_HEADS).reshape(1, A_WIDTH)
    gk = jnp.concatenate([g_k, jnp.ones((LANES - A_HEAD_DIM,), F32)]).reshape(1, LANES)
    tab_spec = pl.BlockSpec((tp, LANES), lambda b, i: (i, 0))
    hq_spec = pl.BlockSpec((1, nq, A_HEAD_DIM, A_HEADS * qb), lambda b, i: (b, i, 0, 0))
    tok_spec = pl.BlockSpec((1, tp, A_HEAD_DIM), lambda b, i: (b, i, 0))
    return pl.pallas_call(
        _dsa_prep_kernel,
        grid=(bsz, seq // tp),
        in_specs=[
            pl.BlockSpec((1, tp, A_WIDTH), lambda b, i: (b, i, COL_AQ // A_WIDTH)),
            pl.BlockSpec((1, tp, I_WIDTH), lambda b, i: (b, i, COL_IQ // I_WIDTH)),
            pl.BlockSpec((1, tp, LANES), lambda b, i: (b, i, COL_AKV // LANES)),
            pl.BlockSpec((1, tp, LANES), lambda b, i: (b, i, COL_MISC // LANES)),
            tab_spec, tab_spec, tab_spec,
            pl.BlockSpec((1, A_WIDTH), lambda b, i: (0, 0)),
            pl.BlockSpec((1, LANES), lambda b, i: (0, 0)),
        ],
        out_specs=[
            hq_spec, hq_spec, tok_spec, tok_spec,
            pl.BlockSpec((1, 1, A_HEAD_DIM, tp), lambda b, i: (b, i, 0, 0)),
            pl.BlockSpec((1, nq, IDX_HEADS, qb), lambda b, i: (b, i, 0, 0)),
        ],
        out_shape=[
            jax.ShapeDtypeStruct((bsz, seq // qb, A_HEAD_DIM, A_HEADS * qb), BF16),
            jax.ShapeDtypeStruct((bsz, seq // qb, IDX_DIM, IDX_HEADS * qb), BF16),
            jax.ShapeDtypeStruct((bsz, seq, A_HEAD_DIM), BF16),
            jax.ShapeDtypeStruct((bsz, seq, IDX_DIM), BF16),
            jax.ShapeDtypeStruct((bsz, seq // tp, A_HEAD_DIM, tp), BF16),
            jax.ShapeDtypeStruct((bsz, seq // qb, IDX_HEADS, qb), F32),
        ],
        compiler_params=_cparams(("parallel", "parallel")),
        name="dsa_prep",
    )(proj, proj, proj, proj, cos, s_up, s_dn, gq, gk)


def _rope_tables(seq):
    half = ROPE_DIM // 2
    inv_freq = ROPE_THETA ** (-jnp.arange(0, ROPE_DIM, 2, dtype=F32) / ROPE_DIM)
    ang = jnp.arange(seq, dtype=F32)[:, None] * inv_freq[None, :]
    cos, sin = jnp.cos(ang), jnp.sin(ang)
    pad = A_HEAD_DIM - ROPE_DIM
    cos64 = jnp.concatenate([cos, cos, jnp.ones((seq, pad), F32)], axis=1)
    up64 = jnp.concatenate([jnp.zeros((seq, half), F32), sin, jnp.zeros((seq, pad), F32)], axis=1)
    dn64 = jnp.concatenate([-sin, jnp.zeros((seq, half + pad), F32)], axis=1)
    rep = LANES // A_HEAD_DIM
    return jnp.tile(cos64, (1, rep)), jnp.tile(up64, (1, rep)), jnp.tile(dn64, (1, rep))


def _dsa_kernel(qt_ref, iqt_ref, iwt_ref, k_ref, vt_ref, ik_ref, got_ref, o_ref,
                key_scr, bias_scr, m_scr, l_scr, acc_scr, kn_scr, *, topk):
    qb, kcs = Q_BLOCK, key_scr.shape[1]
    blk = pl.program_id(1)
    n_chunks = (blk * qb) // kcs + 1
    krow = lax.broadcasted_iota(jnp.int32, (kcs, qb), 0)
    qcol = lax.broadcasted_iota(jnp.int32, (kcs, qb), 1)
    qpos = blk * qb + qcol
    iw = iwt_ref[0, 0] * (IDX_HEADS ** -0.5)

    def key_rows(kc):
        return pl.ds(pl.multiple_of(kc * kcs, kcs), kcs)

    def score_chunk(kc, carry):
        ks = ik_ref[0, key_rows(kc), :]
        logits = _dot(ks, iqt_ref[0, 0])
        acc = jnp.zeros((kcs, qb), F32)
        for h in range(IDX_HEADS):
            acc = acc + jnp.maximum(logits[:, h * qb:(h + 1) * qb], 0.0) * iw[h:h + 1, :]
        acc = jnp.where(acc == 0.0, 0.0, acc)
        sc = jnp.where(kc * kcs + krow <= qpos, acc, -jnp.inf)
        bits = lax.bitcast_convert_type(sc, jnp.int32)
        key_scr[kc] = jnp.where(bits >= 0, bits, bits ^ 0x7FFFFFFF)
        return carry

    lax.fori_loop(0, n_chunks, score_chunk, 0)

    fold = min(COUNT_ROWS, kcs)

    def count(pred_fn):
        def body(kc, c):
            hit = jnp.where(pred_fn(key_scr[kc]), 1.0, 0.0)
            return c + jnp.sum(hit.reshape(kcs // fold, fold, qb), axis=0)
        c = lax.fori_loop(0, n_chunks, body, jnp.zeros((fold, qb), F32))
        return jnp.sum(c, axis=0, keepdims=True)

    kf = float(topk)
    zero = jnp.zeros((1, qb), jnp.int32)
    thr0 = jnp.where(count(lambda k: k >= zero) >= kf, zero, jnp.full((1, qb), INT_MIN, jnp.int32))

    def bit_step(i, thr):
        cand = thr | jnp.left_shift(jnp.int32(1), 30 - i)
        return jnp.where(count(lambda k: k >= cand) >= kf, cand, thr)

    thr = lax.fori_loop(0, 31, bit_step, thr0)

    tie_overflow = jnp.max(count(lambda k: k >= thr)) > kf

    @pl.when(jnp.logical_not(tie_overflow))
    def _():
        def select_chunk(kc, carry):
            sel = (key_scr[kc] >= thr) & (kc * kcs + krow <= qpos)
            bias_scr[kc] = jnp.where(sel, 0.0, NEG_BIG)
            return carry

        lax.fori_loop(0, n_chunks, select_chunk, 0)

    @pl.when(tie_overflow)
    def _():
        need = kf - count(lambda k: k > thr)
        lower = (lax.broadcasted_iota(jnp.int32, (kcs, kcs), 1)
                 <= lax.broadcasted_iota(jnp.int32, (kcs, kcs), 0)).astype(BF16)

        def select_chunk(kc, carry):
            key = key_scr[kc]
            eq = key == thr
            eqf = jnp.where(eq, 1.0, 0.0)
            incl = _dot(lower, eqf.astype(BF16))
            sel = (key > thr) | (eq & (carry + incl - eqf < need))
            sel = sel & (kc * kcs + krow <= qpos)
            bias_scr[kc] = jnp.where(sel, 0.0, NEG_BIG)
            return carry + incl[kcs - 1:kcs, :]

        lax.fori_loop(0, n_chunks, select_chunk, jnp.zeros((1, qb), F32))

    l_scr[...] = jnp.zeros(l_scr.shape, F32)
    acc_scr[...] = jnp.zeros(acc_scr.shape, F32)

    @pl.when(blk == 0)
    def _():
        kf32 = k_ref[0].astype(F32)
        kn_scr[...] = jnp.full(kn_scr.shape, jnp.max(jnp.sum(kf32 * kf32, axis=-1, keepdims=True)))

    qf32 = qt_ref[0, 0].astype(F32)
    logit_bound_sq = jnp.max(jnp.sum(qf32 * qf32, axis=0, keepdims=True)) * kn_scr[0, 0]
    unshifted = logit_bound_sq < SAFE_LOGIT * SAFE_LOGIT

    @pl.when(unshifted)
    def _():
        def attend_chunk(kc, carry):
            kk = k_ref[0, key_rows(kc), :]
            vt = vt_ref[0, kc]
            bias = bias_scr[kc]
            s = _dot(kk, qt_ref[0, 0])
            p = jnp.concatenate([jnp.exp(s[:, h * qb:(h + 1) * qb] + bias) for h in range(A_HEADS)],
                                axis=1)
            l_scr[...] += jnp.sum(p.reshape(kcs // SUBLANES, SUBLANES, A_HEADS * qb), axis=0)
            acc_scr[...] += _dot(vt, p.astype(BF16))
            return carry

        lax.fori_loop(0, n_chunks, attend_chunk, 0)

    @pl.when(jnp.logical_not(unshifted))
    def _():
        m_scr[...] = jnp.full(m_scr.shape, NEG_BIG, F32)

        def attend_chunk(kc, carry):
            kk = k_ref[0, key_rows(kc), :]
            vt = vt_ref[0, kc]
            bias = bias_scr[kc]
            for h in range(A_HEADS):
                cols = slice(h * qb, (h + 1) * qb)
                s = _dot(kk, qt_ref[0, 0, :, cols]) + bias
                m_prev = m_scr[0:1, cols]
                m_new = jnp.maximum(m_prev, jnp.max(s, axis=0, keepdims=True))
                p = jnp.exp(s - m_new)
                alpha = jnp.exp(m_prev - m_new)
                l_scr[0:1, cols] = alpha * l_scr[0:1, cols] + jnp.sum(p, axis=0, keepdims=True)
                acc_scr[:, cols] = alpha * acc_scr[:, cols] + _dot(vt, p.astype(BF16))
                m_scr[0:1, cols] = m_new
            return carry

        lax.fori_loop(0, n_chunks, attend_chunk, 0)

    o = acc_scr[...] / jnp.sum(l_scr[...], axis=0, keepdims=True)
    o = o * lax.rsqrt(jnp.mean(o * o, axis=0, keepdims=True) + EPS) * got_ref[...]
    pairs = [jnp.concatenate([o[:, 2 * p * qb:(2 * p + 1) * qb], o[:, (2 * p + 1) * qb:(2 * p + 2) * qb]],
                             axis=0).T for p in range(A_HEADS // 2)]
    o_ref[0] = jnp.concatenate(pairs, axis=1).astype(BF16)


def _dsa(qt, iqt, iwt, k_r, vt, ik_r, g_out_a):
    bsz, seq, _ = k_r.shape
    qb = Q_BLOCK
    topk = min(TOPK_MAX, seq // 4)
    n_blk = seq // qb
    lanes = A_HEADS * qb
    n_kc, kcs = vt.shape[1], vt.shape[3]
    got = jnp.repeat(g_out_a.reshape(A_HEADS, A_HEAD_DIM).T, qb, axis=1)
    k_spec = pl.BlockSpec((1, seq, A_HEAD_DIM), lambda b, i: (b, 0, 0))
    q_spec = pl.BlockSpec((1, 1, A_HEAD_DIM, lanes), lambda b, i: (b, i, 0, 0))
    return pl.pallas_call(
        functools.partial(_dsa_kernel, topk=topk),
        grid=(bsz, n_blk),
        in_specs=[
            q_spec, q_spec,
            pl.BlockSpec((1, 1, IDX_HEADS, qb), lambda b, i: (b, i, 0, 0)),
            k_spec,
            pl.BlockSpec((1, n_kc, A_HEAD_DIM, kcs), lambda b, i: (b, 0, 0, 0)),
            k_spec,
            pl.BlockSpec((A_HEAD_DIM, lanes), lambda b, i: (0, 0)),
        ],
        out_specs=pl.BlockSpec((1, qb, A_WIDTH), lambda b, i: (b, i, 0)),
        out_shape=jax.ShapeDtypeStruct((bsz, seq, A_WIDTH), BF16),
        scratch_shapes=[
            pltpu.VMEM((n_kc, kcs, qb), jnp.int32),
            pltpu.VMEM((n_kc, kcs, qb), F32),
            pltpu.VMEM((SUBLANES, lanes), F32),
            pltpu.VMEM((SUBLANES, lanes), F32),
            pltpu.VMEM((A_HEAD_DIM, lanes), F32),
            pltpu.VMEM((SUBLANES, LANES), F32),
        ],
        compiler_params=_cparams(("parallel", "arbitrary")),
        name="dsa_attention",
    )(qt, iqt, iwt, k_r, vt, ik_r, got)


def _sgu_kernel(bu_ref, bv_ref, gv_ref, ws_ref, bst_ref, go_ref, o_ref):
    ch = B_CHUNK
    u = jax.nn.gelu(bu_ref[0])
    v = jax.nn.gelu(bv_ref[0])
    vc = v - jnp.mean(v, axis=-1, keepdims=True)
    vn = vc * lax.rsqrt(jnp.mean(vc * vc, axis=-1, keepdims=True) + EPS) * gv_ref[...]
    vb = vn.astype(BF16)
    row = lax.broadcasted_iota(jnp.int32, (ch, ch), 0)
    col = lax.broadcasted_iota(jnp.int32, (ch, ch), 1)
    causal = col <= row
    grp = lax.broadcasted_iota(jnp.int32, (1, B_WIDTH), 1) // B_GROUP_DIM
    bst = bst_ref[...]
    mixed = jnp.zeros((ch, B_WIDTH), F32)
    for g in range(B_GROUPS):
        w = jnp.where(causal, ws_ref[g], 0.0).astype(BF16)
        mixed = jnp.where(grp == g, _dot(w, vb) + bst[:, g:g + 1], mixed)
    o = u * mixed
    ss = _dot_exact_rhs(o * o, _group_ones(B_WIDTH, B_GROUP_DIM))
    o_ref[0] = (o * lax.rsqrt(ss * (1.0 / B_GROUP_DIM) + EPS) * go_ref[...]).astype(BF16)


def _sgu(proj, g_v, w_s, b_s, g_out_b):
    bsz, seq, _ = proj.shape
    ch = B_CHUNK
    return pl.pallas_call(
        _sgu_kernel,
        grid=(bsz, seq // ch),
        in_specs=[
            pl.BlockSpec((1, ch, B_WIDTH), lambda b, i: (b, i, COL_BU // B_WIDTH)),
            pl.BlockSpec((1, ch, B_WIDTH), lambda b, i: (b, i, COL_BV // B_WIDTH)),
            pl.BlockSpec((1, B_WIDTH), lambda b, i: (0, 0)),
            pl.BlockSpec((B_GROUPS, ch, ch), lambda b, i: (0, 0, 0)),
            pl.BlockSpec((ch, B_GROUPS), lambda b, i: (0, 0)),
            pl.BlockSpec((1, B_WIDTH), lambda b, i: (0, 0)),
        ],
        out_specs=pl.BlockSpec((1, ch, B_WIDTH), lambda b, i: (b, i, 0)),
        out_shape=jax.ShapeDtypeStruct((bsz, seq, B_WIDTH), BF16),
        compiler_params=_cparams(("parallel", "parallel")),
        name="spatial_gating",
    )(proj, proj, g_v.reshape(1, B_WIDTH), w_s, b_s.T, g_out_b.reshape(1, B_WIDTH))


def _log_sigmoid(z):
    return jnp.minimum(z, 0.0) - jnp.log1p(jnp.exp(-jnp.abs(z)))


def _gla_kernel(cq_ref, ck_ref, cv_ref, cr_ref, misc_ref, wa_ref, ba_ref, go_ref, o_ref, st_scr):
    tg, ch = cq_ref.shape[1], C_CHUNK

    @pl.when(pl.program_id(1) == 0)
    def _():
        st_scr[...] = jnp.zeros(st_scr.shape, F32)

    z = _dot3(misc_ref[0], wa_ref[...]) + ba_ref[...]
    log_a = _log_sigmoid(z) * (1.0 / C_GATE_TAU)
    r = lax.broadcasted_iota(jnp.int32, (tg, tg), 0)
    c = lax.broadcasted_iota(jnp.int32, (tg, tg), 1)
    same = (r // ch) == (c // ch)
    b = _dot_exact_lhs((same & (c <= r)).astype(BF16), log_a)
    b_last = _dot_exact_lhs(same.astype(BF16), log_a)
    ck = ck_ref[0]
    q_dec = cq_ref[0] * (C_KEY_DIM ** -0.5) * jnp.exp(b)
    k_neg = (ck * jnp.exp(-b)).astype(BF16)
    k_st = (ck * jnp.exp(b_last - b)).astype(BF16)
    cv = cv_ref[0].astype(BF16)

    kp, vp = 2 * C_KEY_DIM, 2 * C_VAL_DIM
    lane_head = lax.broadcasted_iota(jnp.int32, (2, 1, kp), 2) // C_KEY_DIM
    head_mask = lane_head == lax.broadcasted_iota(jnp.int32, (2, 1, kp), 0)
    tril = (lax.broadcasted_iota(jnp.int32, (1, ch, ch), 2)
            <= lax.broadcasted_iota(jnp.int32, (1, ch, ch), 1))
    pair_diag = (lax.broadcasted_iota(jnp.int32, (kp, vp), 0) // C_KEY_DIM
                 == lax.broadcasted_iota(jnp.int32, (kp, vp), 1) // C_VAL_DIM)
    ones_cols = jnp.ones((ch, LANES), BF16)

    for n in range(tg // ch):
        rs = slice(n * ch, (n + 1) * ch)
        la_hi, la_lo = _split(log_a[rs])
        decay = jnp.exp(_dot_tn(la_hi, ones_cols) + _dot_tn(la_lo, ones_cols))[:, 0:1]
        parts = []
        for p in range(C_HEADS // 2):
            kl, vl = slice(p * kp, (p + 1) * kp), slice(p * vp, (p + 1) * vp)
            qd = q_dec[rs, kl]
            vn = cv[rs, vl]
            qm = jnp.where(head_mask, qd[None], 0.0).reshape(2 * ch, kp).astype(BF16)
            att = _dot_nt(qm, k_neg[rs, kl]).reshape(2, ch, ch)
            att = jnp.where(tril, att, 0.0).astype(BF16)
            o_intra = jnp.concatenate([_dot(att[0], vn[:, :C_VAL_DIM]), _dot(att[1], vn[:, C_VAL_DIM:])],
                                      axis=1)
            state = st_scr[p]
            o = o_intra + _dot(qd.astype(BF16), state.astype(BF16))
            st_scr[p] = jnp.where(pair_diag, decay[kl] * state + _dot_tn(k_st[rs, kl], vn), 0.0)
            for oh in (o[:, :C_VAL_DIM], o[:, C_VAL_DIM:]):
                parts.append(oh * lax.rsqrt(jnp.mean(oh * oh, axis=-1, keepdims=True) + EPS))
        cr = cr_ref[0, rs, :]
        o_ref[0, rs, :] = (jnp.concatenate(parts, axis=1) * (cr * jax.nn.sigmoid(cr))
                           * go_ref[...]).astype(BF16)


def _gla(proj, w_a2, b_a, g_out_c):
    bsz, seq, _ = proj.shape
    tg = min(T_GLA, seq)
    wa = jnp.zeros((LANES, C_KWIDTH), F32).at[MISC_CA:MISC_CA + C_GATE_RANK].set(w_a2)
    return pl.pallas_call(
        _gla_kernel,
        grid=(bsz, seq // tg),
        in_specs=[
            pl.BlockSpec((1, tg, C_KWIDTH), lambda b, i: (b, i, COL_CQ // C_KWIDTH)),
            pl.BlockSpec((1, tg, C_KWIDTH), lambda b, i: (b, i, COL_CK // C_KWIDTH)),
            pl.BlockSpec((1, tg, C_WIDTH), lambda b, i: (b, i, COL_CV // C_WIDTH)),
            pl.BlockSpec((1, tg, C_WIDTH), lambda b, i: (b, i, COL_CR // C_WIDTH)),
            pl.BlockSpec((1, tg, LANES), lambda b, i: (b, i, COL_MISC // LANES)),
            pl.BlockSpec((LANES, C_KWIDTH), lambda b, i: (0, 0)),
            pl.BlockSpec((1, C_KWIDTH), lambda b, i: (0, 0)),
            pl.BlockSpec((1, C_WIDTH), lambda b, i: (0, 0)),
        ],
        out_specs=pl.BlockSpec((1, tg, C_WIDTH), lambda b, i: (b, i, 0)),
        out_shape=jax.ShapeDtypeStruct((bsz, seq, C_WIDTH), BF16),
        scratch_shapes=[pltpu.VMEM((C_HEADS // 2, 2 * C_KEY_DIM, 2 * C_VAL_DIM), F32)],
        compiler_params=_cparams(("parallel", "arbitrary")),
        name="gla",
    )(proj, proj, proj, proj, proj, wa, b_a.reshape(1, C_KWIDTH), g_out_c.reshape(1, C_WIDTH))


def _out_proj_kernel(oa_ref, ob_ref, oc_ref, w_ref, x_ref, mod_ref, o_ref):
    y = _dot(oa_ref[0], w_ref[0:A_WIDTH, :])
    y = y + _dot(ob_ref[0], w_ref[A_WIDTH:A_WIDTH + B_WIDTH, :])
    y = y + _dot(oc_ref[0], w_ref[A_WIDTH + B_WIDTH:, :])
    o_ref[0] = x_ref[0] + mod_ref[0, 2:3, :] * y


def _out_proj(o_a, o_b, o_c, w_out_bf16, x, mod_l):
    bsz, seq, d = x.shape
    tm = min(TM_MM, seq)
    dm = w_out_bf16.shape[0]
    return pl.pallas_call(
        _out_proj_kernel,
        grid=(bsz, seq // tm),
        in_specs=[
            pl.BlockSpec((1, tm, A_WIDTH), lambda b, i: (b, i, 0)),
            pl.BlockSpec((1, tm, B_WIDTH), lambda b, i: (b, i, 0)),
            pl.BlockSpec((1, tm, C_WIDTH), lambda b, i: (b, i, 0)),
            pl.BlockSpec((dm, d), lambda b, i: (0, 0)),
            pl.BlockSpec((1, tm, d), lambda b, i: (b, i, 0)),
            pl.BlockSpec((1, N_ADA, d), lambda b, i: (b, 0, 0)),
        ],
        out_specs=pl.BlockSpec((1, tm, d), lambda b, i: (b, i, 0)),
        out_shape=jax.ShapeDtypeStruct((bsz, seq, d), F32),
        compiler_params=_cparams(("parallel", "parallel")),
        name="out_proj",
    )(o_a, o_b, o_c, w_out_bf16, x, mod_l)


def _swiglu_step(h, wg_ref, wu_ref, wd_ref, acc_scr, f, n_f, f_tail):
    def step(width):
        gate = _dot(h, wg_ref[:, :width])
        a = gate * jax.nn.sigmoid(gate) * _dot(h, wu_ref[:, :width])
        acc_scr[...] += _dot(a.astype(BF16), wd_ref[:width, :])

    tf = wg_ref.shape[1]
    if f_tail == tf:
        step(tf)
        return

    @pl.when(f < n_f - 1)
    def _():
        step(tf)

    @pl.when(f == n_f - 1)
    def _():
        step(f_tail)


def _hidden_tiles(f_width):
    n_f = pl.cdiv(f_width, TF_FFN)
    return n_f, f_width - (n_f - 1) * TF_FFN


def _ffn_kernel(x_ref, mod_ref, g_ref, wg_ref, wu_ref, wd_ref, o_ref, h_scr, acc_scr, *, n_f, f_tail):
    f = pl.program_id(2)

    @pl.when(f == 0)
    def _():
        h = _norm_mod(x_ref[0], g_ref[...], mod_ref[0, 4:5, :], mod_ref[0, 3:4, :])
        h_scr[...] = h.astype(BF16)
        acc_scr[...] = jnp.zeros(acc_scr.shape, F32)

    _swiglu_step(h_scr[...], wg_ref, wu_ref, wd_ref, acc_scr, f, n_f, f_tail)

    @pl.when(f == n_f - 1)
    def _():
        o_ref[0] = x_ref[0] + mod_ref[0, 5:6, :] * acc_scr[...]


def _ffn(x, mod_l, g, w_gate, w_up, w_down):
    bsz, seq, d = x.shape
    tm, tf = min(TM_MM, seq), TF_FFN
    n_f, f_tail = _hidden_tiles(w_gate.shape[-1])
    row = lambda b, i, f: (b, i, 0)
    return pl.pallas_call(
        functools.partial(_ffn_kernel, n_f=n_f, f_tail=f_tail),
        grid=(bsz, seq // tm, n_f),
        in_specs=[
            pl.BlockSpec((1, tm, d), row),
            pl.BlockSpec((1, N_ADA, d), lambda b, i, f: (b, 0, 0)),
            pl.BlockSpec((1, d), lambda b, i, f: (0, 0)),
            pl.BlockSpec((d, tf), lambda b, i, f: (0, f)),
            pl.BlockSpec((d, tf), lambda b, i, f: (0, f)),
            pl.BlockSpec((tf, d), lambda b, i, f: (f, 0)),
        ],
        out_specs=pl.BlockSpec((1, tm, d), row),
        out_shape=jax.ShapeDtypeStruct((bsz, seq, d), F32),
        scratch_shapes=[pltpu.VMEM((tm, d), BF16), pltpu.VMEM((tm, d), F32)],
        compiler_params=_cparams(("parallel", "parallel", "arbitrary")),
        name="dense_ffn",
    )(x, mod_l, g.reshape(1, d), w_gate, w_up, w_down)


def _router_kernel(x_ref, mod_ref, g_ref, wr_ref, h_ref, info_ref, cnt_ref, run_scr):
    @pl.when((pl.program_id(0) == 0) & (pl.program_id(1) == 0))
    def _():
        run_scr[...] = jnp.zeros(run_scr.shape, F32)

    h = _norm_mod(x_ref[0], g_ref[...], mod_ref[0, 4:5, :], mod_ref[0, 3:4, :])
    h_ref[0] = h
    logits = _dot3(h, wr_ref[...])
    tm = logits.shape[0]
    lane = lax.broadcasted_iota(jnp.int32, logits.shape, 1).astype(F32)
    logits = jnp.where(lane < N_EXPERTS, logits, -jnp.inf)
    m1 = jnp.max(logits, axis=-1, keepdims=True)
    i1 = jnp.min(jnp.where(logits == m1, lane, float(LANES)), axis=-1, keepdims=True)
    rest = jnp.where(lane == i1, -jnp.inf, logits)
    m2 = jnp.max(rest, axis=-1, keepdims=True)
    i2 = jnp.min(jnp.where(rest == m2, lane, float(LANES)), axis=-1, keepdims=True)
    e2 = jnp.exp(m2 - m1)
    den = 1.0 + e2
    hit1, hit2 = lane == i1, lane == i2
    hits = jnp.where(hit1 | hit2, 1.0, 0.0)
    earlier = (lax.broadcasted_iota(jnp.int32, (tm, tm), 1)
               < lax.broadcasted_iota(jnp.int32, (tm, tm), 0)).astype(BF16)
    rank = run_scr[0:1, :] + _dot(earlier, hits.astype(BF16))
    r1 = jnp.sum(jnp.where(hit1, rank, 0.0), axis=-1, keepdims=True)
    r2 = jnp.sum(jnp.where(hit2, rank, 0.0), axis=-1, keepdims=True)
    run_scr[0:1, :] = run_scr[0:1, :] + jnp.sum(hits, axis=0, keepdims=True)
    info = jnp.zeros(logits.shape, F32)
    for k, val in ((INFO_E1, i1), (INFO_E2, i2), (INFO_R1, r1), (INFO_R2, r2),
                   (INFO_W1, 1.0 / den), (INFO_W2, e2 / den)):
        info = jnp.where(lane == float(k), val, info)
    info_ref[0] = info
    cnt_ref[...] = jnp.broadcast_to(run_scr[0:1, :], cnt_ref.shape)


def _router(x, mod_l, g, w_router):
    bsz, seq, d = x.shape
    tm = min(TM_MM, seq)
    wr = jnp.pad(w_router, ((0, 0), (0, LANES - N_EXPERTS)))
    return pl.pallas_call(
        _router_kernel,
        grid=(bsz, seq // tm),
        in_specs=[
            pl.BlockSpec((1, tm, d), lambda b, i: (b, i, 0)),
            pl.BlockSpec((1, N_ADA, d), lambda b, i: (b, 0, 0)),
            pl.BlockSpec((1, d), lambda b, i: (0, 0)),
            pl.BlockSpec((d, LANES), lambda b, i: (0, 0)),
        ],
        out_specs=[
            pl.BlockSpec((1, tm, d), lambda b, i: (b, i, 0)),
            pl.BlockSpec((1, tm, LANES), lambda b, i: (b, i, 0)),
            pl.BlockSpec((SUBLANES, LANES), lambda b, i: (0, 0)),
        ],
        out_shape=[
            jax.ShapeDtypeStruct((bsz, seq, d), F32),
            jax.ShapeDtypeStruct((bsz, seq, LANES), F32),
            jax.ShapeDtypeStruct((SUBLANES, LANES), F32),
        ],
        scratch_shapes=[pltpu.VMEM((SUBLANES, LANES), F32)],
        compiler_params=_cparams(("arbitrary", "arbitrary")),
        name="router",
    )(x, mod_l, g.reshape(1, d), wr)


def _row_copy(src, dst, sem):
    return pltpu.make_async_copy(src, dst, sem)


def _dispatch_kernel(p1_ref, p2_ref, h_ref, zero_hbm, out_hbm, sem):
    del zero_hbm
    g = h_ref.shape[0]
    base = pl.program_id(0) * g

    def issue(r, carry):
        row = h_ref.at[pl.ds(r, 1)]
        _row_copy(row, out_hbm.at[pl.ds(p1_ref[base + r], 1)], sem).start()
        _row_copy(row, out_hbm.at[pl.ds(p2_ref[base + r], 1)], sem).start()
        return carry

    lax.fori_loop(0, g, issue, 0, unroll=DMA_UNROLL)

    def drain(r, carry):
        _row_copy(h_ref.at[pl.ds(0, 1)], out_hbm.at[pl.ds(0, 1)], sem).wait()
        return carry

    lax.fori_loop(0, 2 * g, drain, 0, unroll=DMA_UNROLL)


def _dispatch(h2d, pos1, pos2, n_rows):
    n, d = h2d.shape
    g = min(G_ROWS, n)
    grid_spec = pltpu.PrefetchScalarGridSpec(
        num_scalar_prefetch=2,
        grid=(n // g,),
        in_specs=[
            pl.BlockSpec((g, d), lambda i, p1, p2: (i, 0)),
            pl.BlockSpec(memory_space=pl.ANY),
        ],
        out_specs=pl.BlockSpec(memory_space=pl.ANY),
        scratch_shapes=[pltpu.SemaphoreType.DMA],
    )
    return pl.pallas_call(
        _dispatch_kernel,
        grid_spec=grid_spec,
        out_shape=jax.ShapeDtypeStruct((n_rows, d), F32),
        input_output_aliases={3: 0},
        compiler_params=_cparams(("arbitrary",)),
        name="moe_dispatch",
    )(pos1, pos2, h2d, jnp.zeros((n_rows, d), F32))


def _moe_ffn_kernel(te_ref, nu_ref, hs_ref, wg_ref, wu_ref, wd_ref, o_ref, h_scr, acc_scr, *, n_f,
                    f_tail):
    del te_ref
    j, f = pl.program_id(0), pl.program_id(1)
    used = j < nu_ref[0]

    @pl.when(used & (f == 0))
    def _():
        h_scr[...] = hs_ref[...].astype(BF16)
        acc_scr[...] = jnp.zeros(acc_scr.shape, F32)

    @pl.when(used)
    def _():
        _swiglu_step(h_scr[...], wg_ref, wu_ref, wd_ref, acc_scr, f, n_f, f_tail)

    @pl.when(used & (f == n_f - 1))
    def _():
        o_ref[...] = acc_scr[...]

    @pl.when(jnp.logical_not(used) & (f == n_f - 1))
    def _():
        o_ref[...] = jnp.zeros(o_ref.shape, F32)


def _moe_ffn(h_sorted, tile_expert, n_used, w_gate, w_up, w_down):
    n_rows, d = h_sorted.shape
    tm, tf = TM_MOE, TF_FFN
    n_f, f_tail = _hidden_tiles(w_gate.shape[-1])
    n_tiles = n_rows // tm

    def live(j, nu):
        return jnp.minimum(j, nu[0] - 1)

    def fcol(j, f, nu):
        return jnp.where(j < nu[0], f, n_f - 1)

    grid_spec = pltpu.PrefetchScalarGridSpec(
        num_scalar_prefetch=2,
        grid=(n_tiles, n_f),
        in_specs=[
            pl.BlockSpec((tm, d), lambda j, f, te, nu: (live(j, nu), 0)),
            pl.BlockSpec((None, d, tf), lambda j, f, te, nu: (te[live(j, nu)], 0, fcol(j, f, nu))),
            pl.BlockSpec((None, d, tf), lambda j, f, te, nu: (te[live(j, nu)], 0, fcol(j, f, nu))),
            pl.BlockSpec((None, tf, d), lambda j, f, te, nu: (te[live(j, nu)], fcol(j, f, nu), 0)),
        ],
        out_specs=pl.BlockSpec((tm, d), lambda j, f, te, nu: (j, 0)),
        scratch_shapes=[pltpu.VMEM((tm, d), BF16), pltpu.VMEM((tm, d), F32)],
    )
    return pl.pallas_call(
        functools.partial(_moe_ffn_kernel, n_f=n_f, f_tail=f_tail),
        grid_spec=grid_spec,
        out_shape=jax.ShapeDtypeStruct((n_rows, d), F32),
        compiler_params=_cparams(("arbitrary", "arbitrary")),
        name="moe_ffn",
    )(tile_expert, n_used, h_sorted, w_gate, w_up, w_down)


def _combine_kernel(p1_ref, p2_ref, x_ref, mod_ref, info_ref, y_hbm, o_ref, buf, sem):
    g = x_ref.shape[1]
    base = (pl.program_id(0) * pl.num_programs(1) + pl.program_id(1)) * g

    def issue(r, carry):
        _row_copy(y_hbm.at[pl.ds(p1_ref[base + r], 1)], buf.at[0, pl.ds(r, 1)], sem).start()
        _row_copy(y_hbm.at[pl.ds(p2_ref[base + r], 1)], buf.at[1, pl.ds(r, 1)], sem).start()
        return carry

    lax.fori_loop(0, g, issue, 0, unroll=DMA_UNROLL)

    def drain(r, carry):
        _row_copy(y_hbm.at[pl.ds(0, 1)], buf.at[0, pl.ds(0, 1)], sem).wait()
        return carry

    lax.fori_loop(0, 2 * g, drain, 0, unroll=DMA_UNROLL)
    info = info_ref[0]
    y = info[:, INFO_W1:INFO_W1 + 1] * buf[0] + info[:, INFO_W2:INFO_W2 + 1] * buf[1]
    o_ref[0] = x_ref[0] + mod_ref[0, 5:6, :] * y


def _combine(x, mod_l, info, y_sorted, pos1, pos2):
    bsz, seq, d = x.shape
    g = min(G_ROWS, seq)
    grid_spec = pltpu.PrefetchScalarGridSpec(
        num_scalar_prefetch=2,
        grid=(bsz, seq // g),
        in_specs=[
            pl.BlockSpec((1, g, d), lambda b, i, p1, p2: (b, i, 0)),
            pl.BlockSpec((1, N_ADA, d), lambda b, i, p1, p2: (b, 0, 0)),
            pl.BlockSpec((1, g, LANES), lambda b, i, p1, p2: (b, i, 0)),
            pl.BlockSpec(memory_space=pl.ANY),
        ],
        out_specs=pl.BlockSpec((1, g, d), lambda b, i, p1, p2: (b, i, 0)),
        scratch_shapes=[pltpu.VMEM((2, g, d), F32), pltpu.SemaphoreType.DMA],
    )
    return pl.pallas_call(
        _combine_kernel,
        grid_spec=grid_spec,
        out_shape=jax.ShapeDtypeStruct((bsz, seq, d), F32),
        compiler_params=_cparams(("arbitrary", "arbitrary")),
        name="moe_combine",
    )(pos1, pos2, x, mod_l, info, y_sorted)


def _moe(x, mod_l, g, w_router, first_expert, w_gate, w_up, w_down):
    bsz, seq, d = x.shape
    n = bsz * seq
    tm = TM_MOE
    h, info, counts = _router(x, mod_l, g, w_router)
    cnt = counts[0, :N_EXPERTS].astype(jnp.int32)
    padded = (cnt + tm - 1) // tm * tm
    ends = jnp.cumsum(padded)
    starts = ends - padded
    n_rows = 2 * n + N_EXPERTS * tm
    tile_start = jnp.arange(n_rows // tm, dtype=jnp.int32) * tm
    tile_expert = jnp.minimum(jnp.sum(tile_start[:, None] >= ends[None, :], axis=1), N_EXPERTS - 1)
    n_used = (ends[-1] // tm).reshape(1).astype(jnp.int32)
    rec = info.reshape(n, LANES)
    e1, e2 = rec[:, INFO_E1].astype(jnp.int32), rec[:, INFO_E2].astype(jnp.int32)
    experts = jnp.arange(N_EXPERTS, dtype=jnp.int32)[None, :]
    pos1 = jnp.sum(jnp.where(e1[:, None] == experts, starts[None, :], 0), axis=1) + rec[:, INFO_R1].astype(jnp.int32)
    pos2 = jnp.sum(jnp.where(e2[:, None] == experts, starts[None, :], 0), axis=1) + rec[:, INFO_R2].astype(jnp.int32)
    h_sorted = _dispatch(h.reshape(n, d), pos1, pos2, n_rows)
    y_sorted = _moe_ffn(h_sorted, (tile_expert + first_expert).astype(jnp.int32), n_used,
                        w_gate, w_up, w_down)
    return _combine(x, mod_l, info, y_sorted, pos1, pos2)


def _reorder_w_in(w):
    sizes = (A_WIDTH, A_HEAD_DIM, A_HEAD_DIM, I_WIDTH, IDX_DIM, IDX_HEADS, B_WIDTH, B_WIDTH,
             C_KWIDTH, C_KWIDTH, C_WIDTH, C_GATE_RANK, C_WIDTH)
    offs = [0]
    for s in sizes:
        offs.append(offs[-1] + s)
    aq, ak, av, iq, ik, iw, bu, bv, cq, ck, cv, ca, cr = (
        w[:, offs[i]:offs[i + 1]] for i in range(len(sizes)))
    pad = jnp.zeros((w.shape[0], LANES - IDX_DIM - IDX_HEADS - C_GATE_RANK), w.dtype)
    return jnp.concatenate([aq, iq, bu, bv, cq, ck, cv, cr, ak, av, ik, iw, ca, pad],
                           axis=1).astype(BF16)


def _stack_experts(w):
    return w.astype(BF16).reshape((w.shape[0] * w.shape[1],) + w.shape[2:])


def kernel(x, c, w_ada, b_ada, g_norm1, g_norm2, w_in, g_q, g_k, g_v_b, w_s, b_s, w_a2, b_a,
           g_out, w_out, w_ff_gate, w_ff_up, w_ff_down, w_router, w_e_gate, w_e_up, w_e_down):
    depth = w_in.shape[0]
    mod = _ada_mod(c, w_ada, b_ada)
    tables = _rope_tables(x.shape[1])
    expert_w = tuple(_stack_experts(w) for w in (w_e_gate, w_e_up, w_e_down))
    for layer in range(depth):
        mod_l = mod[layer]
        proj = _in_proj(x, mod_l, g_norm1[layer], _reorder_w_in(w_in[layer]))
        qt, iqt, k_r, ik_r, vt, iwt = _dsa_prep(proj, tables, g_q[layer], g_k[layer])
        o_a = _dsa(qt, iqt, iwt, k_r, vt, ik_r, g_out[layer, :A_WIDTH])
        o_b = _sgu(proj, g_v_b[layer], w_s[layer], b_s[layer], g_out[layer, A_WIDTH:A_WIDTH + B_WIDTH])
        o_c = _gla(proj, w_a2[layer], b_a[layer], g_out[layer, A_WIDTH + B_WIDTH:])
        x = _out_proj(o_a, o_b, o_c, w_out[layer].astype(BF16), x, mod_l)
        j = layer // 2
        if layer % 2 == 0:
            x = _ffn(x, mod_l, g_norm2[layer], w_ff_gate[j].astype(BF16), w_ff_up[j].astype(BF16),
                     w_ff_down[j].astype(BF16))
        else:
            x = _moe(x, mod_l, g_norm2[layer], w_router[j], j * N_EXPERTS, *expert_w)
    return x
```

```python
import functools

import jax
import jax.numpy as jnp
from jax import lax
from jax.experimental import pallas as pl
from jax.experimental.pallas import tpu as pltpu

F32 = jnp.float32
BF16 = jnp.bfloat16

A_HEADS = 8
A_HEAD_DIM = 64
IDX_HEADS = 8
IDX_DIM = 64
TOPK_MAX = 256
B_GROUPS = 8
B_GROUP_DIM = 64
B_CHUNK = 128
C_HEADS = 8
C_VAL_DIM = 128
C_KEY_DIM = 64
C_GATE_RANK = 16
C_GATE_TAU = 16.0
C_CHUNK = 64
ROPE_THETA = 500000.0
ROPE_DIM = 16
N_EXPERTS = 8
N_ADA = 6
EPS = 1e-6

A_WIDTH = A_HEADS * A_HEAD_DIM
I_WIDTH = IDX_HEADS * IDX_DIM
B_WIDTH = B_GROUPS * B_GROUP_DIM
C_KWIDTH = C_HEADS * C_KEY_DIM
C_WIDTH = C_HEADS * C_VAL_DIM

LANES = 128
SUBLANES = 8
VMEM_LIMIT_BYTES = 56 * 1024 * 1024

COL_AQ = 0
COL_IQ = 512
COL_BU = 1024
COL_BV = 1536
COL_CQ = 2048
COL_CK = 2560
COL_CV = 3072
COL_CR = 4096
COL_AKV = 5120
COL_MISC = 5248
N_PROJ = 5376
MISC_IW = IDX_DIM
MISC_CA = IDX_DIM + IDX_HEADS

NEG_BIG = -1e30
SAFE_LOGIT = 60.0
INT_MIN = -(2 ** 31)
COUNT_ROWS = 64

TM_MM = 512
TN_IN = 1792
DMA_UNROLL = 8
TF_FFN = 512
TN_ADA = 1024
Q_BLOCK = 128
K_CHUNK = 512
T_GLA = 256
TM_MOE = 512
G_ROWS = 256

INFO_E1, INFO_E2, INFO_R1, INFO_R2, INFO_W1, INFO_W2 = range(6)


def _cparams(sem):
    return pltpu.CompilerParams(dimension_semantics=sem, vmem_limit_bytes=VMEM_LIMIT_BYTES)


def _dot(a, b):
    return jnp.dot(a, b, preferred_element_type=F32)


def _dot_nt(a, b):
    return lax.dot_general(a, b, (((1,), (1,)), ((), ())), preferred_element_type=F32)


def _dot_tn(a, b):
    return lax.dot_general(a, b, (((0,), (0,)), ((), ())), preferred_element_type=F32)


def _split(x):
    hi = x.astype(BF16)
    lo = (x - hi.astype(F32)).astype(BF16)
    return hi, lo


def _dot_exact_lhs(m_bf16, x):
    hi, lo = _split(x)
    return _dot(m_bf16, hi) + _dot(m_bf16, lo)


def _dot_exact_rhs(x, m_bf16):
    hi, lo = _split(x)
    return _dot(hi, m_bf16) + _dot(lo, m_bf16)


def _dot3(a, b):
    ah, al = _split(a)
    bh, bl = _split(b)
    return _dot(ah, bh) + (_dot(al, bh) + _dot(ah, bl))


def _norm_mod(x, g, scale, shift):
    ms = jnp.mean(x * x, axis=-1, keepdims=True)
    return (x * lax.rsqrt(ms + EPS) * g) * (1.0 + scale) + shift


def _group_ones(width, group):
    r = lax.broadcasted_iota(jnp.int32, (width, width), 0) // group
    c = lax.broadcasted_iota(jnp.int32, (width, width), 1) // group
    return (r == c).astype(BF16)


def _ada_kernel(c_ref, w_ref, b_ref, o_ref):
    c = c_ref[...]
    cond = (c * jax.nn.sigmoid(c)).astype(BF16)
    o_ref[0] = _dot(cond, w_ref[0].astype(BF16)) + b_ref[0]


def _ada_mod(c, w_ada, b_ada):
    depth, d, n6 = w_ada.shape
    bsz = c.shape[0]
    rows = 16
    c_pad = jnp.pad(c, ((0, rows - bsz), (0, 0)))
    tn = TN_ADA
    out = pl.pallas_call(
        _ada_kernel,
        grid=(depth, n6 // tn),
        in_specs=[
            pl.BlockSpec((rows, d), lambda l, j: (0, 0)),
            pl.BlockSpec((1, d, tn), lambda l, j: (l, 0, j)),
            pl.BlockSpec((1, 1, tn), lambda l, j: (l, 0, j)),
        ],
        out_specs=pl.BlockSpec((1, rows, tn), lambda l, j: (l, 0, j)),
        out_shape=jax.ShapeDtypeStruct((depth, rows, n6), F32),
        compiler_params=_cparams(("parallel", "parallel")),
        name="ada_mod",
    )(c_pad, w_ada, b_ada.reshape(depth, 1, n6))
    return out[:, :bsz].reshape(depth, bsz, N_ADA, d)


def _in_proj_kernel(x_ref, mod_ref, g_ref, w_ref, o_ref, h_scr):
    @pl.when(pl.program_id(2) == 0)
    def _():
        h = _norm_mod(x_ref[0], g_ref[...], mod_ref[0, 1:2, :], mod_ref[0, 0:1, :])
        h_scr[...] = h.astype(BF16)

    o_ref[0] = _dot(h_scr[...], w_ref[...])


def _in_proj(x, mod_l, g, w_p):
    bsz, seq, d = x.shape
    n = w_p.shape[1]
    tm, tn = min(TM_MM, seq), TN_IN
    return pl.pallas_call(
        _in_proj_kernel,
        grid=(bsz, seq // tm, n // tn),
        in_specs=[
            pl.BlockSpec((1, tm, d), lambda b, i, j: (b, i, 0)),
            pl.BlockSpec((1, N_ADA, d), lambda b, i, j: (b, 0, 0)),
            pl.BlockSpec((1, d), lambda b, i, j: (0, 0)),
            pl.BlockSpec((d, tn), lambda b, i, j: (0, j)),
        ],
        out_specs=pl.BlockSpec((1, tm, tn), lambda b, i, j: (b, i, j)),
        out_shape=jax.ShapeDtypeStruct((bsz, seq, n), F32),
        scratch_shapes=[pltpu.VMEM((tm, d), BF16)],
        compiler_params=_cparams(("parallel", "parallel", "arbitrary")),
        name="in_proj",
    )(x, mod_l, g.reshape(1, d), w_p)


def _rope(x, cos, s_up, s_dn):
    parts = []
    for j in range(x.shape[1] // LANES):
        xs = x[:, j * LANES:(j + 1) * LANES]
        parts.append(xs * cos + pltpu.roll(xs, ROPE_DIM // 2, 1) * s_up
                     + pltpu.roll(xs, LANES - ROPE_DIM // 2, 1) * s_dn)
    return parts[0] if len(parts) == 1 else jnp.concatenate(parts, axis=1)


def _heads_to_lanes(x, heads):
    parts = []
    for p in range(heads // 2):
        t = x[:, p * LANES:(p + 1) * LANES].T
        parts += [t[:A_HEAD_DIM], t[A_HEAD_DIM:]]
    return jnp.concatenate(parts, axis=1)


def _dsa_prep_kernel(aq_ref, iq_ref, akv_ref, misc_ref, cos_ref, sup_ref, sdn_ref, gq_ref, gk_ref,
                     qt_out, iqt_out, k_out, ik_out, vt_out, iwt_out):
    qb = Q_BLOCK
    cos, s_up, s_dn = cos_ref[...], sup_ref[...], sdn_ref[...]
    aq = aq_ref[0]
    ss = _dot_exact_rhs(aq * aq, _group_ones(A_WIDTH, A_HEAD_DIM))
    qn = aq * lax.rsqrt(ss * (1.0 / A_HEAD_DIM) + EPS) * gq_ref[...]
    q = _rope(qn, cos, s_up, s_dn) * (A_HEAD_DIM ** -0.5)
    iq = _rope(iq_ref[0], cos, s_up, s_dn) * (IDX_DIM ** -0.5)
    lane = lax.broadcasted_iota(jnp.int32, (1, LANES), 1)
    first = lane < A_HEAD_DIM
    akv = akv_ref[0]
    kss = jnp.sum(jnp.where(first, akv * akv, 0.0), axis=-1, keepdims=True)
    kn = akv * lax.rsqrt(kss * (1.0 / A_HEAD_DIM) + EPS) * gk_ref[...]
    k_out[0] = _rope(kn, cos, s_up, s_dn)[:, :A_HEAD_DIM].astype(BF16)
    misc = misc_ref[0]
    ik_out[0] = _rope(misc, cos, s_up, s_dn)[:, :IDX_DIM].astype(BF16)
    vts = []
    for j in range(aq.shape[0] // qb):
        rows = slice(j * qb, (j + 1) * qb)
        qt_out[0, j] = _heads_to_lanes(q[rows], A_HEADS).astype(BF16)
        iqt_out[0, j] = _heads_to_lanes(iq[rows], IDX_HEADS).astype(BF16)
        vts.append(akv[rows].T[A_HEAD_DIM:])
        iwt_out[0, j] = misc[rows].T[MISC_IW:MISC_IW + IDX_HEADS]
    vt_out[0, 0] = jnp.concatenate(vts, axis=1).astype(BF16)


def _dsa_prep(proj, tables, g_q, g_k):
    bsz, seq, _ = proj.shape
    qb = Q_BLOCK
    tp = min(K_CHUNK, seq)
    nq = tp // qb
    cos, s_up, s_dn = tables
    gq = jnp.tile(g_q, A_HEADS).reshape(1, A_WIDTH)
    gk = jnp.concatenate([g_k, jnp.ones((LANES - A_HEAD_DIM,), F32)]).reshape(1, LANES)
    tab_spec = pl.BlockSpec((tp, LANES), lambda b, i: (i, 0))
    hq_spec = pl.BlockSpec((1, nq, A_HEAD_DIM, A_HEADS * qb), lambda b, i: (b, i, 0, 0))
    tok_spec = pl.BlockSpec((1, tp, A_HEAD_DIM), lambda b, i: (b, i, 0))
    return pl.pallas_call(
        _dsa_prep_kernel,
        grid=(bsz, seq // tp),
        in_specs=[
            pl.BlockSpec((1, tp, A_WIDTH), lambda b, i: (b, i, COL_AQ // A_WIDTH)),
            pl.BlockSpec((1, tp, I_WIDTH), lambda b, i: (b, i, COL_IQ // I_WIDTH)),
            pl.BlockSpec((1, tp, LANES), lambda b, i: (b, i, COL_AKV // LANES)),
            pl.BlockSpec((1, tp, LANES), lambda b, i: (b, i, COL_MISC // LANES)),
            tab_spec, tab_spec, tab_spec,
            pl.BlockSpec((1, A_WIDTH), lambda b, i: (0, 0)),
            pl.BlockSpec((1, LANES), lambda b, i: (0, 0)),
        ],
        out_specs=[
            hq_spec, hq_spec, tok_spec, tok_spec,
            pl.BlockSpec((1, 1, A_HEAD_DIM, tp), lambda b, i: (b, i, 0, 0)),
            pl.BlockSpec((1, nq, IDX_HEADS, qb), lambda b, i: (b, i, 0, 0)),
        ],
        out_shape=[
            jax.ShapeDtypeStruct((bsz, seq // qb, A_HEAD_DIM, A_HEADS * qb), BF16),
            jax.ShapeDtypeStruct((bsz, seq // qb, IDX_DIM, IDX_HEADS * qb), BF16),
            jax.ShapeDtypeStruct((bsz, seq, A_HEAD_DIM), BF16),
            jax.ShapeDtypeStruct((bsz, seq, IDX_DIM), BF16),
            jax.ShapeDtypeStruct((bsz, seq // tp, A_HEAD_DIM, tp), BF16),
            jax.ShapeDtypeStruct((bsz, seq // qb, IDX_HEADS, qb), F32),
        ],
        compiler_params=_cparams(("parallel", "parallel")),
        name="dsa_prep",
    )(proj, proj, proj, proj, cos, s_up, s_dn, gq, gk)


def _rope_tables(seq):
    half = ROPE_DIM // 2
    inv_freq = ROPE_THETA ** (-jnp.arange(0, ROPE_DIM, 2, dtype=F32) / ROPE_DIM)
    ang = jnp.arange(seq, dtype=F32)[:, None] * inv_freq[None, :]
    cos, sin = jnp.cos(ang), jnp.sin(ang)
    pad = A_HEAD_DIM - ROPE_DIM
    cos64 = jnp.concatenate([cos, cos, jnp.ones((seq, pad), F32)], axis=1)
    up64 = jnp.concatenate([jnp.zeros((seq, half), F32), sin, jnp.zeros((seq, pad), F32)], axis=1)
    dn64 = jnp.concatenate([-sin, jnp.zeros((seq, half + pad), F32)], axis=1)
    rep = LANES // A_HEAD_DIM
    return jnp.tile(cos64, (1, rep)), jnp.tile(up64, (1, rep)), jnp.tile(dn64, (1, rep))


def _dsa_kernel(qt_ref, iqt_ref, iwt_ref, k_ref, vt_ref, ik_ref, got_ref, o_ref,
                key_scr, bias_scr, m_scr, l_scr, acc_scr, kn_scr, *, topk):
    qb, kcs = Q_BLOCK, key_scr.shape[1]
    blk = pl.program_id(1)
    n_chunks = (blk * qb) // kcs + 1
    krow = lax.broadcasted_iota(jnp.int32, (kcs, qb), 0)
    qcol = lax.broadcasted_iota(jnp.int32, (kcs, qb), 1)
    qpos = blk * qb + qcol
    iw = iwt_ref[0, 0] * (IDX_HEADS ** -0.5)

    def key_rows(kc):
        return pl.ds(pl.multiple_of(kc * kcs, kcs), kcs)

    def score_chunk(kc, carry):
        ks = ik_ref[0, key_rows(kc), :]
        logits = _dot(ks, iqt_ref[0, 0])
        acc = jnp.zeros((kcs, qb), F32)
        for h in range(IDX_HEADS):
            acc = acc + jnp.maximum(logits[:, h * qb:(h + 1) * qb], 0.0) * iw[h:h + 1, :]
        acc = jnp.where(acc == 0.0, 0.0, acc)
        sc = jnp.where(kc * kcs + krow <= qpos, acc, -jnp.inf)
        bits = lax.bitcast_convert_type(sc, jnp.int32)
        key_scr[kc] = jnp.where(bits >= 0, bits, bits ^ 0x7FFFFFFF)
        return carry

    lax.fori_loop(0, n_chunks, score_chunk, 0)

    fold = min(COUNT_ROWS, kcs)

    def count(pred_fn):
        def body(kc, c):
            hit = jnp.where(pred_fn(key_scr[kc]), 1.0, 0.0)
            return c + jnp.sum(hit.reshape(kcs // fold, fold, qb), axis=0)
        c = lax.fori_loop(0, n_chunks, body, jnp.zeros((fold, qb), F32))
        return jnp.sum(c, axis=0, keepdims=True)

    kf = float(topk)
    zero = jnp.zeros((1, qb), jnp.int32)
    n_all = float(kcs) * n_chunks.astype(F32)
    c0 = count(lambda k: k >= zero)
    thr0 = jnp.where(c0 >= kf, zero, jnp.full((1, qb), INT_MIN, jnp.int32))
    n_ge0 = jnp.where(c0 >= kf, c0, jnp.zeros((1, qb), F32) + n_all)

    def bit_step(i, carry):
        thr, n_ge = carry
        cand = thr | jnp.left_shift(jnp.int32(1), 30 - i)
        c = count(lambda k: k >= cand)
        return jnp.where(c >= kf, cand, thr), jnp.where(c >= kf, c, n_ge)

    thr, n_ge = lax.fori_loop(0, 31, bit_step, (thr0, n_ge0))

    tie_overflow = jnp.max(n_ge) > kf

    @pl.when(jnp.logical_not(tie_overflow))
    def _():
        def select_chunk(kc, carry):
            sel = (key_scr[kc] >= thr) & (kc * kcs + krow <= qpos)
            bias_scr[kc] = jnp.where(sel, 0.0, NEG_BIG)
            return carry

        lax.fori_loop(0, n_chunks, select_chunk, 0)

    @pl.when(tie_overflow)
    def _():
        need = kf - count(lambda k: k > thr)
        lower = (lax.broadcasted_iota(jnp.int32, (kcs, kcs), 1)
                 <= lax.broadcasted_iota(jnp.int32, (kcs, kcs), 0)).astype(BF16)

        def select_chunk(kc, carry):
            key = key_scr[kc]
            eq = key == thr
            eqf = jnp.where(eq, 1.0, 0.0)
            incl = _dot(lower, eqf.astype(BF16))
            sel = (key > thr) | (eq & (carry + incl - eqf < need))
            sel = sel & (kc * kcs + krow <= qpos)
            bias_scr[kc] = jnp.where(sel, 0.0, NEG_BIG)
            return carry + incl[kcs - 1:kcs, :]

        lax.fori_loop(0, n_chunks, select_chunk, jnp.zeros((1, qb), F32))

    l_scr[...] = jnp.zeros(l_scr.shape, F32)
    acc_scr[...] = jnp.zeros(acc_scr.shape, F32)

    @pl.when(blk == 0)
    def _():
        kf32 = k_ref[0].astype(F32)
        kn_scr[...] = jnp.full(kn_scr.shape, jnp.max(jnp.sum(kf32 * kf32, axis=-1, keepdims=True)))

    qf32 = qt_ref[0, 0].astype(F32)
    logit_bound_sq = jnp.max(jnp.sum(qf32 * qf32, axis=0, keepdims=True)) * kn_scr[0, 0]
    unshifted = logit_bound_sq < SAFE_LOGIT * SAFE_LOGIT

    @pl.when(unshifted)
    def _():
        def attend_chunk(kc, carry):
            kk = k_ref[0, key_rows(kc), :]
            vt = vt_ref[0, kc]
            bias = bias_scr[kc]
            s = _dot(kk, qt_ref[0, 0])
            p = jnp.concatenate([jnp.exp(s[:, h * qb:(h + 1) * qb] + bias) for h in range(A_HEADS)],
                                axis=1)
            l_scr[...] += jnp.sum(p.reshape(kcs // SUBLANES, SUBLANES, A_HEADS * qb), axis=0)
            acc_scr[...] += _dot(vt, p.astype(BF16))
            return carry

        lax.fori_loop(0, n_chunks, attend_chunk, 0)

    @pl.when(jnp.logical_not(unshifted))
    def _():
        m_scr[...] = jnp.full(m_scr.shape, NEG_BIG, F32)

        def attend_chunk(kc, carry):
            kk = k_ref[0, key_rows(kc), :]
            vt = vt_ref[0, kc]
            bias = bias_scr[kc]
            for h in range(A_HEADS):
                cols = slice(h * qb, (h + 1) * qb)
                s = _dot(kk, qt_ref[0, 0, :, cols]) + bias
                m_prev = m_scr[0:1, cols]
                m_new = jnp.maximum(m_prev, jnp.max(s, axis=0, keepdims=True))
                p = jnp.exp(s - m_new)
                alpha = jnp.exp(m_prev - m_new)
                l_scr[0:1, cols] = alpha * l_scr[0:1, cols] + jnp.sum(p, axis=0, keepdims=True)
                acc_scr[:, cols] = alpha * acc_scr[:, cols] + _dot(vt, p.astype(BF16))
                m_scr[0:1, cols] = m_new
            return carry

        lax.fori_loop(0, n_chunks, attend_chunk, 0)

    o = acc_scr[...] / jnp.sum(l_scr[...], axis=0, keepdims=True)
    o = o * lax.rsqrt(jnp.mean(o * o, axis=0, keepdims=True) + EPS) * got_ref[...]
    pairs = [jnp.concatenate([o[:, 2 * p * qb:(2 * p + 1) * qb], o[:, (2 * p + 1) * qb:(2 * p + 2) * qb]],
                             axis=0).T for p in range(A_HEADS // 2)]
    o_ref[0] = jnp.concatenate(pairs, axis=1).astype(BF16)


def _dsa(qt, iqt, iwt, k_r, vt, ik_r, g_out_a):
    bsz, seq, _ = k_r.shape
    qb = Q_BLOCK
    topk = min(TOPK_MAX, seq // 4)
    n_blk = seq // qb
    lanes = A_HEADS * qb
    n_kc, kcs = vt.shape[1], vt.shape[3]
    got = jnp.repeat(g_out_a.reshape(A_HEADS, A_HEAD_DIM).T, qb, axis=1)
    k_spec = pl.BlockSpec((1, seq, A_HEAD_DIM), lambda b, i: (b, 0, 0))
    q_spec = pl.BlockSpec((1, 1, A_HEAD_DIM, lanes), lambda b, i: (b, i, 0, 0))
    return pl.pallas_call(
        functools.partial(_dsa_kernel, topk=topk),
        grid=(bsz, n_blk),
        in_specs=[
            q_spec, q_spec,
            pl.BlockSpec((1, 1, IDX_HEADS, qb), lambda b, i: (b, i, 0, 0)),
            k_spec,
            pl.BlockSpec((1, n_kc, A_HEAD_DIM, kcs), lambda b, i: (b, 0, 0, 0)),
            k_spec,
            pl.BlockSpec((A_HEAD_DIM, lanes), lambda b, i: (0, 0)),
        ],
        out_specs=pl.BlockSpec((1, qb, A_WIDTH), lambda b, i: (b, i, 0)),
        out_shape=jax.ShapeDtypeStruct((bsz, seq, A_WIDTH), BF16),
        scratch_shapes=[
            pltpu.VMEM((n_kc, kcs, qb), jnp.int32),
            pltpu.VMEM((n_kc, kcs, qb), F32),
            pltpu.VMEM((SUBLANES, lanes), F32),
            pltpu.VMEM((SUBLANES, lanes), F32),
            pltpu.VMEM((A_HEAD_DIM, lanes), F32),
            pltpu.VMEM((SUBLANES, LANES), F32),
        ],
        compiler_params=_cparams(("parallel", "arbitrary")),
        name="dsa_attention",
    )(qt, iqt, iwt, k_r, vt, ik_r, got)


def _sgu_kernel(bu_ref, bv_ref, gv_ref, ws_ref, bst_ref, go_ref, o_ref):
    ch = B_CHUNK
    u = jax.nn.gelu(bu_ref[0])
    v = jax.nn.gelu(bv_ref[0])
    vc = v - jnp.mean(v, axis=-1, keepdims=True)
    vn = vc * lax.rsqrt(jnp.mean(vc * vc, axis=-1, keepdims=True) + EPS) * gv_ref[...]
    vb = vn.astype(BF16)
    row = lax.broadcasted_iota(jnp.int32, (ch, ch), 0)
    col = lax.broadcasted_iota(jnp.int32, (ch, ch), 1)
    causal = col <= row
    grp = lax.broadcasted_iota(jnp.int32, (1, B_WIDTH), 1) // B_GROUP_DIM
    bst = bst_ref[...]
    mixed = jnp.zeros((ch, B_WIDTH), F32)
    for g in range(B_GROUPS):
        w = jnp.where(causal, ws_ref[g], 0.0).astype(BF16)
        mixed = jnp.where(grp == g, _dot(w, vb) + bst[:, g:g + 1], mixed)
    o = u * mixed
    ss = _dot_exact_rhs(o * o, _group_ones(B_WIDTH, B_GROUP_DIM))
    o_ref[0] = (o * lax.rsqrt(ss * (1.0 / B_GROUP_DIM) + EPS) * go_ref[...]).astype(BF16)


def _sgu(proj, g_v, w_s, b_s, g_out_b):
    bsz, seq, _ = proj.shape
    ch = B_CHUNK
    return pl.pallas_call(
        _sgu_kernel,
        grid=(bsz, seq // ch),
        in_specs=[
            pl.BlockSpec((1, ch, B_WIDTH), lambda b, i: (b, i, COL_BU // B_WIDTH)),
            pl.BlockSpec((1, ch, B_WIDTH), lambda b, i: (b, i, COL_BV // B_WIDTH)),
            pl.BlockSpec((1, B_WIDTH), lambda b, i: (0, 0)),
            pl.BlockSpec((B_GROUPS, ch, ch), lambda b, i: (0, 0, 0)),
            pl.BlockSpec((ch, B_GROUPS), lambda b, i: (0, 0)),
            pl.BlockSpec((1, B_WIDTH), lambda b, i: (0, 0)),
        ],
        out_specs=pl.BlockSpec((1, ch, B_WIDTH), lambda b, i: (b, i, 0)),
        out_shape=jax.ShapeDtypeStruct((bsz, seq, B_WIDTH), BF16),
        compiler_params=_cparams(("parallel", "parallel")),
        name="spatial_gating",
    )(proj, proj, g_v.reshape(1, B_WIDTH), w_s, b_s.T, g_out_b.reshape(1, B_WIDTH))


def _log_sigmoid(z):
    return jnp.minimum(z, 0.0) - jnp.log1p(jnp.exp(-jnp.abs(z)))


def _gla_kernel(cq_ref, ck_ref, cv_ref, cr_ref, misc_ref, wa_ref, ba_ref, go_ref, o_ref, st_scr):
    tg, ch = cq_ref.shape[1], C_CHUNK

    @pl.when(pl.program_id(1) == 0)
    def _():
        st_scr[...] = jnp.zeros(st_scr.shape, F32)

    z = _dot3(misc_ref[0], wa_ref[...]) + ba_ref[...]
    log_a = _log_sigmoid(z) * (1.0 / C_GATE_TAU)
    r = lax.broadcasted_iota(jnp.int32, (tg, tg), 0)
    c = lax.broadcasted_iota(jnp.int32, (tg, tg), 1)
    same = (r // ch) == (c // ch)
    b = _dot_exact_lhs((same & (c <= r)).astype(BF16), log_a)
    b_last = _dot_exact_lhs(same.astype(BF16), log_a)
    ck = ck_ref[0]
    q_dec = cq_ref[0] * (C_KEY_DIM ** -0.5) * jnp.exp(b)
    k_neg = (ck * jnp.exp(-b)).astype(BF16)
    k_st = (ck * jnp.exp(b_last - b)).astype(BF16)
    cv = cv_ref[0].astype(BF16)

    kp, vp = 2 * C_KEY_DIM, 2 * C_VAL_DIM
    lane_head = lax.broadcasted_iota(jnp.int32, (2, 1, kp), 2) // C_KEY_DIM
    head_mask = lane_head == lax.broadcasted_iota(jnp.int32, (2, 1, kp), 0)
    tril = (lax.broadcasted_iota(jnp.int32, (1, ch, ch), 2)
            <= lax.broadcasted_iota(jnp.int32, (1, ch, ch), 1))
    pair_diag = (lax.broadcasted_iota(jnp.int32, (kp, vp), 0) // C_KEY_DIM
                 == lax.broadcasted_iota(jnp.int32, (kp, vp), 1) // C_VAL_DIM)
    ones_cols = jnp.ones((ch, LANES), BF16)

    for n in range(tg // ch):
        rs = slice(n * ch, (n + 1) * ch)
        la_hi, la_lo = _split(log_a[rs])
        decay = jnp.exp(_dot_tn(la_hi, ones_cols) + _dot_tn(la_lo, ones_cols))[:, 0:1]
        parts = []
        for p in range(C_HEADS // 2):
            kl, vl = slice(p * kp, (p + 1) * kp), slice(p * vp, (p + 1) * vp)
            qd = q_dec[rs, kl]
            vn = cv[rs, vl]
            qm = jnp.where(head_mask, qd[None], 0.0).reshape(2 * ch, kp).astype(BF16)
            att = _dot_nt(qm, k_neg[rs, kl]).reshape(2, ch, ch)
            att = jnp.where(tril, att, 0.0).astype(BF16)
            o_intra = jnp.concatenate([_dot(att[0], vn[:, :C_VAL_DIM]), _dot(att[1], vn[:, C_VAL_DIM:])],
                                      axis=1)
            state = st_scr[p]
            o = o_intra + _dot(qd.astype(BF16), state.astype(BF16))
            st_scr[p] = jnp.where(pair_diag, decay[kl] * state + _dot_tn(k_st[rs, kl], vn), 0.0)
            for oh in (o[:, :C_VAL_DIM], o[:, C_VAL_DIM:]):
                parts.append(oh * lax.rsqrt(jnp.mean(oh * oh, axis=-1, keepdims=True) + EPS))
        cr = cr_ref[0, rs, :]
        o_ref[0, rs, :] = (jnp.concatenate(parts, axis=1) * (cr * jax.nn.sigmoid(cr))
                           * go_ref[...]).astype(BF16)


def _gla(proj, w_a2, b_a, g_out_c):
    bsz, seq, _ = proj.shape
    tg = min(T_GLA, seq)
    wa = jnp.zeros((LANES, C_KWIDTH), F32).at[MISC_CA:MISC_CA + C_GATE_RANK].set(w_a2)
    return pl.pallas_call(
        _gla_kernel,
        grid=(bsz, seq // tg),
        in_specs=[
            pl.BlockSpec((1, tg, C_KWIDTH), lambda b, i: (b, i, COL_CQ // C_KWIDTH)),
            pl.BlockSpec((1, tg, C_KWIDTH), lambda b, i: (b, i, COL_CK // C_KWIDTH)),
            pl.BlockSpec((1, tg, C_WIDTH), lambda b, i: (b, i, COL_CV // C_WIDTH)),
            pl.BlockSpec((1, tg, C_WIDTH), lambda b, i: (b, i, COL_CR // C_WIDTH)),
            pl.BlockSpec((1, tg, LANES), lambda b, i: (b, i, COL_MISC // LANES)),
            pl.BlockSpec((LANES, C_KWIDTH), lambda b, i: (0, 0)),
            pl.BlockSpec((1, C_KWIDTH), lambda b, i: (0, 0)),
            pl.BlockSpec((1, C_WIDTH), lambda b, i: (0, 0)),
        ],
        out_specs=pl.BlockSpec((1, tg, C_WIDTH), lambda b, i: (b, i, 0)),
        out_shape=jax.ShapeDtypeStruct((bsz, seq, C_WIDTH), BF16),
        scratch_shapes=[pltpu.VMEM((C_HEADS // 2, 2 * C_KEY_DIM, 2 * C_VAL_DIM), F32)],
        compiler_params=_cparams(("parallel", "arbitrary")),
        name="gla",
    )(proj, proj, proj, proj, proj, wa, b_a.reshape(1, C_KWIDTH), g_out_c.reshape(1, C_WIDTH))


def _out_proj_kernel(oa_ref, ob_ref, oc_ref, w_ref, x_ref, mod_ref, o_ref):
    y = _dot(oa_ref[0], w_ref[0:A_WIDTH, :])
    y = y + _dot(ob_ref[0], w_ref[A_WIDTH:A_WIDTH + B_WIDTH, :])
    y = y + _dot(oc_ref[0], w_ref[A_WIDTH + B_WIDTH:, :])
    o_ref[0] = x_ref[0] + mod_ref[0, 2:3, :] * y


def _out_proj(o_a, o_b, o_c, w_out_bf16, x, mod_l):
    bsz, seq, d = x.shape
    tm = min(TM_MM, seq)
    dm = w_out_bf16.shape[0]
    return pl.pallas_call(
        _out_proj_kernel,
        grid=(bsz, seq // tm),
        in_specs=[
            pl.BlockSpec((1, tm, A_WIDTH), lambda b, i: (b, i, 0)),
            pl.BlockSpec((1, tm, B_WIDTH), lambda b, i: (b, i, 0)),
            pl.BlockSpec((1, tm, C_WIDTH), lambda b, i: (b, i, 0)),
            pl.BlockSpec((dm, d), lambda b, i: (0, 0)),
            pl.BlockSpec((1, tm, d), lambda b, i: (b, i, 0)),
            pl.BlockSpec((1, N_ADA, d), lambda b, i: (b, 0, 0)),
        ],
        out_specs=pl.BlockSpec((1, tm, d), lambda b, i: (b, i, 0)),
        out_shape=jax.ShapeDtypeStruct((bsz, seq, d), F32),
        compiler_params=_cparams(("parallel", "parallel")),
        name="out_proj",
    )(o_a, o_b, o_c, w_out_bf16, x, mod_l)


def _swiglu_step(h, wg_ref, wu_ref, wd_ref, acc_scr, f, n_f, f_tail):
    def step(width):
        gate = _dot(h, wg_ref[:, :width])
        a = gate * jax.nn.sigmoid(gate) * _dot(h, wu_ref[:, :width])
        acc_scr[...] += _dot(a.astype(BF16), wd_ref[:width, :])

    tf = wg_ref.shape[1]
    if f_tail == tf:
        step(tf)
        return

    @pl.when(f < n_f - 1)
    def _():
        step(tf)

    @pl.when(f == n_f - 1)
    def _():
        step(f_tail)


def _hidden_tiles(f_width):
    n_f = pl.cdiv(f_width, TF_FFN)
    return n_f, f_width - (n_f - 1) * TF_FFN


def _ffn_kernel(x_ref, mod_ref, g_ref, wg_ref, wu_ref, wd_ref, o_ref, h_scr, acc_scr, *, n_f, f_tail):
    f = pl.program_id(2)

    @pl.when(f == 0)
    def _():
        h = _norm_mod(x_ref[0], g_ref[...], mod_ref[0, 4:5, :], mod_ref[0, 3:4, :])
        h_scr[...] = h.astype(BF16)
        acc_scr[...] = jnp.zeros(acc_scr.shape, F32)

    _swiglu_step(h_scr[...], wg_ref, wu_ref, wd_ref, acc_scr, f, n_f, f_tail)

    @pl.when(f == n_f - 1)
    def _():
        o_ref[0] = x_ref[0] + mod_ref[0, 5:6, :] * acc_scr[...]


def _ffn(x, mod_l, g, w_gate, w_up, w_down):
    bsz, seq, d = x.shape
    tm, tf = min(TM_MM, seq), TF_FFN
    n_f, f_tail = _hidden_tiles(w_gate.shape[-1])
    row = lambda b, i, f: (b, i, 0)
    return pl.pallas_call(
        functools.partial(_ffn_kernel, n_f=n_f, f_tail=f_tail),
        grid=(bsz, seq // tm, n_f),
        in_specs=[
            pl.BlockSpec((1, tm, d), row),
            pl.BlockSpec((1, N_ADA, d), lambda b, i, f: (b, 0, 0)),
            pl.BlockSpec((1, d), lambda b, i, f: (0, 0)),
            pl.BlockSpec((d, tf), lambda b, i, f: (0, f)),
            pl.BlockSpec((d, tf), lambda b, i, f: (0, f)),
            pl.BlockSpec((tf, d), lambda b, i, f: (f, 0)),
        ],
        out_specs=pl.BlockSpec((1, tm, d), row),
        out_shape=jax.ShapeDtypeStruct((bsz, seq, d), F32),
        scratch_shapes=[pltpu.VMEM((tm, d), BF16), pltpu.VMEM((tm, d), F32)],
        compiler_params=_cparams(("parallel", "parallel", "arbitrary")),
        name="dense_ffn",
    )(x, mod_l, g.reshape(1, d), w_gate, w_up, w_down)


def _router_kernel(x_ref, mod_ref, g_ref, wr_ref, h_ref, info_ref, cnt_ref, run_scr):
    @pl.when((pl.program_id(0) == 0) & (pl.program_id(1) == 0))
    def _():
        run_scr[...] = jnp.zeros(run_scr.shape, F32)

    h = _norm_mod(x_ref[0], g_ref[...], mod_ref[0, 4:5, :], mod_ref[0, 3:4, :])
    h_ref[0] = h
    logits = _dot3(h, wr_ref[...])
    tm = logits.shape[0]
    lane = lax.broadcasted_iota(jnp.int32, logits.shape, 1).astype(F32)
    logits = jnp.where(lane < N_EXPERTS, logits, -jnp.inf)
    m1 = jnp.max(logits, axis=-1, keepdims=True)
    i1 = jnp.min(jnp.where(logits == m1, lane, float(LANES)), axis=-1, keepdims=True)
    rest = jnp.where(lane == i1, -jnp.inf, logits)
    m2 = jnp.max(rest, axis=-1, keepdims=True)
    i2 = jnp.min(jnp.where(rest == m2, lane, float(LANES)), axis=-1, keepdims=True)
    e2 = jnp.exp(m2 - m1)
    den = 1.0 + e2
    hit1, hit2 = lane == i1, lane == i2
    hits = jnp.where(hit1 | hit2, 1.0, 0.0)
    earlier = (lax.broadcasted_iota(jnp.int32, (tm, tm), 1)
               < lax.broadcasted_iota(jnp.int32, (tm, tm), 0)).astype(BF16)
    rank = run_scr[0:1, :] + _dot(earlier, hits.astype(BF16))
    r1 = jnp.sum(jnp.where(hit1, rank, 0.0), axis=-1, keepdims=True)
    r2 = jnp.sum(jnp.where(hit2, rank, 0.0), axis=-1, keepdims=True)
    run_scr[0:1, :] = run_scr[0:1, :] + jnp.sum(hits, axis=0, keepdims=True)
    info = jnp.zeros(logits.shape, F32)
    for k, val in ((INFO_E1, i1), (INFO_E2, i2), (INFO_R1, r1), (INFO_R2, r2),
                   (INFO_W1, 1.0 / den), (INFO_W2, e2 / den)):
        info = jnp.where(lane == float(k), val, info)
    info_ref[0] = info
    cnt_ref[...] = jnp.broadcast_to(run_scr[0:1, :], cnt_ref.shape)


def _router(x, mod_l, g, w_router):
    bsz, seq, d = x.shape
    tm = min(TM_MM, seq)
    wr = jnp.pad(w_router, ((0, 0), (0, LANES - N_EXPERTS)))
    return pl.pallas_call(
        _router_kernel,
        grid=(bsz, seq // tm),
        in_specs=[
            pl.BlockSpec((1, tm, d), lambda b, i: (b, i, 0)),
            pl.BlockSpec((1, N_ADA, d), lambda b, i: (b, 0, 0)),
            pl.BlockSpec((1, d), lambda b, i: (0, 0)),
            pl.BlockSpec((d, LANES), lambda b, i: (0, 0)),
        ],
        out_specs=[
            pl.BlockSpec((1, tm, d), lambda b, i: (b, i, 0)),
            pl.BlockSpec((1, tm, LANES), lambda b, i: (b, i, 0)),
            pl.BlockSpec((SUBLANES, LANES), lambda b, i: (0, 0)),
        ],
        out_shape=[
            jax.ShapeDtypeStruct((bsz, seq, d), F32),
            jax.ShapeDtypeStruct((bsz, seq, LANES), F32),
            jax.ShapeDtypeStruct((SUBLANES, LANES), F32),
        ],
        scratch_shapes=[pltpu.VMEM((SUBLANES, LANES), F32)],
        compiler_params=_cparams(("arbitrary", "arbitrary")),
        name="router",
    )(x, mod_l, g.reshape(1, d), wr)


def _row_copy(src, dst, sem):
    return pltpu.make_async_copy(src, dst, sem)


def _dispatch_kernel(p1_ref, p2_ref, pad0_ref, padn_ref, h_ref, out_hbm, sem, zrow):
    g = h_ref.shape[0]
    base = pl.program_id(0) * g

    @pl.when(pl.program_id(0) == 0)
    def _():
        zrow[...] = jnp.zeros(zrow.shape, F32)
        zero_row = zrow.at[pl.ds(0, 1)]
        n_pad = 0
        for e in range(N_EXPERTS + 1):
            def zero_issue(r, carry, e=e):
                _row_copy(zero_row, out_hbm.at[pl.ds(pad0_ref[e] + r, 1)], sem).start()
                return carry

            lax.fori_loop(0, padn_ref[e], zero_issue, 0)
            n_pad = n_pad + padn_ref[e]

        def zero_drain(r, carry):
            _row_copy(zero_row, out_hbm.at[pl.ds(0, 1)], sem).wait()
            return carry

        lax.fori_loop(0, n_pad, zero_drain, 0)

    def issue(r, carry):
        row = h_ref.at[pl.ds(r, 1)]
        _row_copy(row, out_hbm.at[pl.ds(p1_ref[base + r], 1)], sem).start()
        _row_copy(row, out_hbm.at[pl.ds(p2_ref[base + r], 1)], sem).start()
        return carry

    lax.fori_loop(0, g, issue, 0, unroll=DMA_UNROLL)

    def drain(r, carry):
        _row_copy(h_ref.at[pl.ds(0, 1)], out_hbm.at[pl.ds(0, 1)], sem).wait()
        return carry

    lax.fori_loop(0, 2 * g, drain, 0, unroll=DMA_UNROLL)


def _dispatch(h2d, pos1, pos2, pad_start, pad_len, n_rows):
    n, d = h2d.shape
    g = min(G_ROWS, n)
    grid_spec = pltpu.PrefetchScalarGridSpec(
        num_scalar_prefetch=4,
        grid=(n // g,),
        in_specs=[pl.BlockSpec((g, d), lambda i, p1, p2, s0, sn: (i, 0))],
        out_specs=pl.BlockSpec(memory_space=pl.ANY),
        scratch_shapes=[pltpu.SemaphoreType.DMA, pltpu.VMEM((SUBLANES, d), F32)],
    )
    return pl.pallas_call(
        _dispatch_kernel,
        grid_spec=grid_spec,
        out_shape=jax.ShapeDtypeStruct((n_rows, d), F32),
        compiler_params=_cparams(("arbitrary",)),
        name="moe_dispatch",
    )(pos1, pos2, pad_start, pad_len, h2d)


def _moe_ffn_kernel(te_ref, nu_ref, hs_ref, wg_ref, wu_ref, wd_ref, o_ref, h_scr, acc_scr, *, n_f,
                    f_tail):
    del te_ref
    j, f = pl.program_id(0), pl.program_id(1)
    used = j < nu_ref[0]

    @pl.when(used & (f == 0))
    def _():
        h_scr[...] = hs_ref[...].astype(BF16)
        acc_scr[...] = jnp.zeros(acc_scr.shape, F32)

    @pl.when(used)
    def _():
        _swiglu_step(h_scr[...], wg_ref, wu_ref, wd_ref, acc_scr, f, n_f, f_tail)

    @pl.when(used & (f == n_f - 1))
    def _():
        o_ref[...] = acc_scr[...]

    @pl.when(jnp.logical_not(used) & (f == n_f - 1))
    def _():
        o_ref[...] = jnp.zeros(o_ref.shape, F32)


def _moe_ffn(h_sorted, tile_expert, n_used, w_gate, w_up, w_down):
    n_rows, d = h_sorted.shape
    tm, tf = TM_MOE, TF_FFN
    n_f, f_tail = _hidden_tiles(w_gate.shape[-1])
    n_tiles = n_rows // tm

    def live(j, nu):
        return jnp.minimum(j, nu[0] - 1)

    def fcol(j, f, nu):
        return jnp.where(j < nu[0], f, n_f - 1)

    grid_spec = pltpu.PrefetchScalarGridSpec(
        num_scalar_prefetch=2,
        grid=(n_tiles, n_f),
        in_specs=[
            pl.BlockSpec((tm, d), lambda j, f, te, nu: (live(j, nu), 0)),
            pl.BlockSpec((None, d, tf), lambda j, f, te, nu: (te[live(j, nu)], 0, fcol(j, f, nu))),
            pl.BlockSpec((None, d, tf), lambda j, f, te, nu: (te[live(j, nu)], 0, fcol(j, f, nu))),
            pl.BlockSpec((None, tf, d), lambda j, f, te, nu: (te[live(j, nu)], fcol(j, f, nu), 0)),
        ],
        out_specs=pl.BlockSpec((tm, d), lambda j, f, te, nu: (j, 0)),
        scratch_shapes=[pltpu.VMEM((tm, d), BF16), pltpu.VMEM((tm, d), F32)],
    )
    return pl.pallas_call(
        functools.partial(_moe_ffn_kernel, n_f=n_f, f_tail=f_tail),
        grid_spec=grid_spec,
        out_shape=jax.ShapeDtypeStruct((n_rows, d), F32),
        compiler_params=_cparams(("arbitrary", "arbitrary")),
        name="moe_ffn",
    )(tile_expert, n_used, h_sorted, w_gate, w_up, w_down)


def _combine_kernel(p1_ref, p2_ref, x_ref, mod_ref, info_ref, y_hbm, o_ref, buf, sem):
    g = x_ref.shape[1]
    base = (pl.program_id(0) * pl.num_programs(1) + pl.program_id(1)) * g

    def issue(r, carry):
        _row_copy(y_hbm.at[pl.ds(p1_ref[base + r], 1)], buf.at[0, pl.ds(r, 1)], sem).start()
        _row_copy(y_hbm.at[pl.ds(p2_ref[base + r], 1)], buf.at[1, pl.ds(r, 1)], sem).start()
        return carry

    lax.fori_loop(0, g, issue, 0, unroll=DMA_UNROLL)

    def drain(r, carry):
        _row_copy(y_hbm.at[pl.ds(0, 1)], buf.at[0, pl.ds(0, 1)], sem).wait()
        return carry

    lax.fori_loop(0, 2 * g, drain, 0, unroll=DMA_UNROLL)
    info = info_ref[0]
    y = info[:, INFO_W1:INFO_W1 + 1] * buf[0] + info[:, INFO_W2:INFO_W2 + 1] * buf[1]
    o_ref[0] = x_ref[0] + mod_ref[0, 5:6, :] * y


def _combine(x, mod_l, info, y_sorted, pos1, pos2):
    bsz, seq, d = x.shape
    g = min(G_ROWS, seq)
    grid_spec = pltpu.PrefetchScalarGridSpec(
        num_scalar_prefetch=2,
        grid=(bsz, seq // g),
        in_specs=[
            pl.BlockSpec((1, g, d), lambda b, i, p1, p2: (b, i, 0)),
            pl.BlockSpec((1, N_ADA, d), lambda b, i, p1, p2: (b, 0, 0)),
            pl.BlockSpec((1, g, LANES), lambda b, i, p1, p2: (b, i, 0)),
            pl.BlockSpec(memory_space=pl.ANY),
        ],
        out_specs=pl.BlockSpec((1, g, d), lambda b, i, p1, p2: (b, i, 0)),
        scratch_shapes=[pltpu.VMEM((2, g, d), F32), pltpu.SemaphoreType.DMA],
    )
    return pl.pallas_call(
        _combine_kernel,
        grid_spec=grid_spec,
        out_shape=jax.ShapeDtypeStruct((bsz, seq, d), F32),
        compiler_params=_cparams(("arbitrary", "arbitrary")),
        name="moe_combine",
    )(pos1, pos2, x, mod_l, info, y_sorted)


def _moe(x, mod_l, g, w_router, first_expert, w_gate, w_up, w_down):
    bsz, seq, d = x.shape
    n = bsz * seq
    tm = TM_MOE
    h, info, counts = _router(x, mod_l, g, w_router)
    cnt = counts[0, :N_EXPERTS].astype(jnp.int32)
    padded = (cnt + tm - 1) // tm * tm
    ends = jnp.cumsum(padded)
    starts = ends - padded
    n_rows = 2 * n + N_EXPERTS * tm
    tile_start = jnp.arange(n_rows // tm, dtype=jnp.int32) * tm
    tile_expert = jnp.minimum(jnp.sum(tile_start[:, None] >= ends[None, :], axis=1), N_EXPERTS - 1)
    n_used = (ends[-1] // tm).reshape(1).astype(jnp.int32)
    rec = info.reshape(n, LANES)
    e1, e2 = rec[:, INFO_E1].astype(jnp.int32), rec[:, INFO_E2].astype(jnp.int32)
    experts = jnp.arange(N_EXPERTS, dtype=jnp.int32)[None, :]
    pos1 = jnp.sum(jnp.where(e1[:, None] == experts, starts[None, :], 0), axis=1) + rec[:, INFO_R1].astype(jnp.int32)
    pos2 = jnp.sum(jnp.where(e2[:, None] == experts, starts[None, :], 0), axis=1) + rec[:, INFO_R2].astype(jnp.int32)
    pad_start = jnp.concatenate([starts + cnt, ends[-1:]]).astype(jnp.int32)
    pad_len = jnp.concatenate([padded - cnt, n_rows - ends[-1:]]).astype(jnp.int32)
    h_sorted = _dispatch(h.reshape(n, d), pos1, pos2, pad_start, pad_len, n_rows)
    y_sorted = _moe_ffn(h_sorted, (tile_expert + first_expert).astype(jnp.int32), n_used,
                        w_gate, w_up, w_down)
    return _combine(x, mod_l, info, y_sorted, pos1, pos2)


def _reorder_w_in(w):
    sizes = (A_WIDTH, A_HEAD_DIM, A_HEAD_DIM, I_WIDTH, IDX_DIM, IDX_HEADS, B_WIDTH, B_WIDTH,
             C_KWIDTH, C_KWIDTH, C_WIDTH, C_GATE_RANK, C_WIDTH)
    offs = [0]
    for s in sizes:
        offs.append(offs[-1] + s)
    aq, ak, av, iq, ik, iw, bu, bv, cq, ck, cv, ca, cr = (
        w[:, offs[i]:offs[i + 1]] for i in range(len(sizes)))
    pad = jnp.zeros((w.shape[0], LANES - IDX_DIM - IDX_HEADS - C_GATE_RANK), w.dtype)
    return jnp.concatenate([aq, iq, bu, bv, cq, ck, cv, cr, ak, av, ik, iw, ca, pad],
                           axis=1).astype(BF16)


def _stack_experts(w):
    return w.astype(BF16).reshape((w.shape[0] * w.shape[1],) + w.shape[2:])


def kernel(x, c, w_ada, b_ada, g_norm1, g_norm2, w_in, g_q, g_k, g_v_b, w_s, b_s, w_a2, b_a,
           g_out, w_out, w_ff_gate, w_ff_up, w_ff_down, w_router, w_e_gate, w_e_up, w_e_down):
    depth = w_in.shape[0]
    mod = _ada_mod(c, w_ada, b_ada)
    tables = _rope_tables(x.shape[1])
    expert_w = tuple(_stack_experts(w) for w in (w_e_gate, w_e_up, w_e_down))
    for layer in range(depth):
        mod_l = mod[layer]
        proj = _in_proj(x, mod_l, g_norm1[layer], _reorder_w_in(w_in[layer]))
        qt, iqt, k_r, ik_r, vt, iwt = _dsa_prep(proj, tables, g_q[layer], g_k[layer])
        o_a = _dsa(qt, iqt, iwt, k_r, vt, ik_r, g_out[layer, :A_WIDTH])
        o_b = _sgu(proj, g_v_b[layer], w_s[layer], b_s[layer], g_out[layer, A_WIDTH:A_WIDTH + B_WIDTH])
        o_c = _gla(proj, w_a2[layer], b_a[layer], g_out[layer, A_WIDTH + B_WIDTH:])
        x = _out_proj(o_a, o_b, o_c, w_out[layer].astype(BF16), x, mod_l)
        j = layer // 2
        if layer % 2 == 0:
            x = _ffn(x, mod_l, g_norm2[layer], w_ff_gate[j].astype(BF16), w_ff_up[j].astype(BF16),
                     w_ff_down[j].astype(BF16))
        else:
            x = _moe(x, mod_l, g_norm2[layer], w_router[j], j * N_EXPERTS, *expert_w)
    return x
```

```python
import functools

import jax
import jax.numpy as jnp
from jax import lax
from jax.experimental import pallas as pl
from jax.experimental.pallas import tpu as pltpu

F32 = jnp.float32
BF16 = jnp.bfloat16

A_HEADS = 8
A_HEAD_DIM = 64
IDX_HEADS = 8
IDX_DIM = 64
TOPK_MAX = 256
B_GROUPS = 8
B_GROUP_DIM = 64
B_CHUNK = 128
C_HEADS = 8
C_VAL_DIM = 128
C_KEY_DIM = 64
C_GATE_RANK = 16
C_GATE_TAU = 16.0
C_CHUNK = 64
ROPE_THETA = 500000.0
ROPE_DIM = 16
N_EXPERTS = 8
N_ADA = 6
EPS = 1e-6

A_WIDTH = A_HEADS * A_HEAD_DIM
I_WIDTH = IDX_HEADS * IDX_DIM
B_WIDTH = B_GROUPS * B_GROUP_DIM
C_KWIDTH = C_HEADS * C_KEY_DIM
C_WIDTH = C_HEADS * C_VAL_DIM

LANES = 128
SUBLANES = 8
VMEM_LIMIT_BYTES = 56 * 1024 * 1024

COL_AQ = 0
COL_IQ = 512
COL_BU = 1024
COL_BV = 1536
COL_CQ = 2048
COL_CK = 2560
COL_CV = 3072
COL_CR = 4096
COL_AKV = 5120
COL_MISC = 5248
N_PROJ = 5376
MISC_IW = IDX_DIM
MISC_CA = IDX_DIM + IDX_HEADS

NEG_BIG = -1e30
SAFE_LOGIT = 40.0
INT_MIN = -(2 ** 31)
COUNT_ROWS = 64

TM_MM = 512
TN_IN = 1792
DMA_UNROLL = 8
TF_FFN = 512
TN_ADA = 1024
Q_BLOCK = 128
K_CHUNK = 512
T_GLA = 256
TM_MOE = 512
G_ROWS = 256

INFO_E1, INFO_E2, INFO_R1, INFO_R2, INFO_W1, INFO_W2 = range(6)


def _cparams(sem):
    return pltpu.CompilerParams(dimension_semantics=sem, vmem_limit_bytes=VMEM_LIMIT_BYTES)


def _dot(a, b):
    return jnp.dot(a, b, preferred_element_type=F32)


def _dot_nt(a, b):
    return lax.dot_general(a, b, (((1,), (1,)), ((), ())), preferred_element_type=F32)


def _dot_tn(a, b):
    return lax.dot_general(a, b, (((0,), (0,)), ((), ())), preferred_element_type=F32)


def _split(x):
    hi = x.astype(BF16)
    lo = (x - hi.astype(F32)).astype(BF16)
    return hi, lo


def _dot_exact_lhs(m_bf16, x):
    hi, lo = _split(x)
    return _dot(m_bf16, hi) + _dot(m_bf16, lo)


def _dot_exact_rhs(x, m_bf16):
    hi, lo = _split(x)
    return _dot(hi, m_bf16) + _dot(lo, m_bf16)


def _dot3(a, b):
    ah, al = _split(a)
    bh, bl = _split(b)
    return _dot(ah, bh) + (_dot(al, bh) + _dot(ah, bl))


def _norm_mod(x, g, scale, shift):
    ms = jnp.mean(x * x, axis=-1, keepdims=True)
    return (x * lax.rsqrt(ms + EPS) * g) * (1.0 + scale) + shift


def _group_ones(width, group):
    r = lax.broadcasted_iota(jnp.int32, (width, width), 0) // group
    c = lax.broadcasted_iota(jnp.int32, (width, width), 1) // group
    return (r == c).astype(BF16)


def _ada_kernel(c_ref, w_ref, b_ref, o_ref):
    c = c_ref[...]
    cond = (c * jax.nn.sigmoid(c)).astype(BF16)
    o_ref[0] = _dot(cond, w_ref[0].astype(BF16)) + b_ref[0]


def _ada_mod(c, w_ada, b_ada):
    depth, d, n6 = w_ada.shape
    bsz = c.shape[0]
    rows = 16
    c_pad = jnp.pad(c, ((0, rows - bsz), (0, 0)))
    tn = TN_ADA
    out = pl.pallas_call(
        _ada_kernel,
        grid=(depth, n6 // tn),
        in_specs=[
            pl.BlockSpec((rows, d), lambda l, j: (0, 0)),
            pl.BlockSpec((1, d, tn), lambda l, j: (l, 0, j)),
            pl.BlockSpec((1, 1, tn), lambda l, j: (l, 0, j)),
        ],
        out_specs=pl.BlockSpec((1, rows, tn), lambda l, j: (l, 0, j)),
        out_shape=jax.ShapeDtypeStruct((depth, rows, n6), F32),
        compiler_params=_cparams(("parallel", "parallel")),
        name="ada_mod",
    )(c_pad, w_ada, b_ada.reshape(depth, 1, n6))
    return out[:, :bsz].reshape(depth, bsz, N_ADA, d)


def _in_proj_kernel(x_ref, mod_ref, g_ref, w_ref, o_ref, h_scr):
    @pl.when(pl.program_id(2) == 0)
    def _():
        h = _norm_mod(x_ref[0], g_ref[...], mod_ref[0, 1:2, :], mod_ref[0, 0:1, :])
        h_scr[...] = h.astype(BF16)

    o_ref[0] = _dot(h_scr[...], w_ref[...])


def _in_proj(x, mod_l, g, w_p):
    bsz, seq, d = x.shape
    n = w_p.shape[1]
    tm, tn = min(TM_MM, seq), TN_IN
    return pl.pallas_call(
        _in_proj_kernel,
        grid=(bsz, seq // tm, n // tn),
        in_specs=[
            pl.BlockSpec((1, tm, d), lambda b, i, j: (b, i, 0)),
            pl.BlockSpec((1, N_ADA, d), lambda b, i, j: (b, 0, 0)),
            pl.BlockSpec((1, d), lambda b, i, j: (0, 0)),
            pl.BlockSpec((d, tn), lambda b, i, j: (0, j)),
        ],
        out_specs=pl.BlockSpec((1, tm, tn), lambda b, i, j: (b, i, j)),
        out_shape=jax.ShapeDtypeStruct((bsz, seq, n), F32),
        scratch_shapes=[pltpu.VMEM((tm, d), BF16)],
        compiler_params=_cparams(("parallel", "parallel", "arbitrary")),
        name="in_proj",
    )(x, mod_l, g.reshape(1, d), w_p)


def _rope(x, cos, s_up, s_dn):
    parts = []
    for j in range(x.shape[1] // LANES):
        xs = x[:, j * LANES:(j + 1) * LANES]
        parts.append(xs * cos + pltpu.roll(xs, ROPE_DIM // 2, 1) * s_up
                     + pltpu.roll(xs, LANES - ROPE_DIM // 2, 1) * s_dn)
    return parts[0] if len(parts) == 1 else jnp.concatenate(parts, axis=1)


def _heads_to_lanes(x, heads):
    parts = []
    for p in range(heads // 2):
        t = x[:, p * LANES:(p + 1) * LANES].T
        parts += [t[:A_HEAD_DIM], t[A_HEAD_DIM:]]
    return jnp.concatenate(parts, axis=1)


def _dsa_prep_kernel(aq_ref, iq_ref, akv_ref, misc_ref, cos_ref, sup_ref, sdn_ref, gq_ref, gk_ref,
                     qt_out, iqt_out, k_out, ik_out, vt_out, iwt_out):
    qb = Q_BLOCK
    cos, s_up, s_dn = cos_ref[...], sup_ref[...], sdn_ref[...]
    aq = aq_ref[0]
    ss = _dot_exact_rhs(aq * aq, _group_ones(A_WIDTH, A_HEAD_DIM))
    qn = aq * lax.rsqrt(ss * (1.0 / A_HEAD_DIM) + EPS) * gq_ref[...]
    q = _rope(qn, cos, s_up, s_dn) * (A_HEAD_DIM ** -0.5)
    iq = _rope(iq_ref[0], cos, s_up, s_dn) * (IDX_DIM ** -0.5)
    lane = lax.broadcasted_iota(jnp.int32, (1, LANES), 1)
    first = lane < A_HEAD_DIM
    akv = akv_ref[0]
    kss = jnp.sum(jnp.where(first, akv * akv, 0.0), axis=-1, keepdims=True)
    kn = akv * lax.rsqrt(kss * (1.0 / A_HEAD_DIM) + EPS) * gk_ref[...]
    k_out[0] = _rope(kn, cos, s_up, s_dn)[:, :A_HEAD_DIM].astype(BF16)
    misc = misc_ref[0]
    ik_out[0] = _rope(misc, cos, s_up, s_dn)[:, :IDX_DIM].astype(BF16)
    vts = []
    for j in range(aq.shape[0] // qb):
        rows = slice(j * qb, (j + 1) * qb)
        qt_out[0, j] = _heads_to_lanes(q[rows], A_HEADS).astype(BF16)
        iqt_out[0, j] = _heads_to_lanes(iq[rows], IDX_HEADS).astype(BF16)
        vts.append(akv[rows].T[A_HEAD_DIM:])
        iwt_out[0, j] = misc[rows].T[MISC_IW:MISC_IW + IDX_HEADS]
    vt_out[0, 0] = jnp.concatenate(vts, axis=1).astype(BF16)


def _dsa_prep(proj, tables, g_q, g_k):
    bsz, seq, _ = proj.shape
    qb = Q_BLOCK
    tp = min(K_CHUNK, seq)
    nq = tp // qb
    cos, s_up, s_dn = tables
    gq = jnp.tile(g_q, A_HEADS).reshape(1, A_WIDTH)
    gk = jnp.concatenate([g_k, jnp.ones((LANES - A_HEAD_DIM,), F32)]).reshape(1, LANES)
    tab_spec = pl.BlockSpec((tp, LANES), lambda b, i: (i, 0))
    hq_spec = pl.BlockSpec((1, nq, A_HEAD_DIM, A_HEADS * qb), lambda b, i: (b, i, 0, 0))
    tok_spec = pl.BlockSpec((1, tp, A_HEAD_DIM), lambda b, i: (b, i, 0))
    return pl.pallas_call(
        _dsa_prep_kernel,
        grid=(bsz, seq // tp),
        in_specs=[
            pl.BlockSpec((1, tp, A_WIDTH), lambda b, i: (b, i, COL_AQ // A_WIDTH)),
            pl.BlockSpec((1, tp, I_WIDTH), lambda b, i: (b, i, COL_IQ // I_WIDTH)),
            pl.BlockSpec((1, tp, LANES), lambda b, i: (b, i, COL_AKV // LANES)),
            pl.BlockSpec((1, tp, LANES), lambda b, i: (b, i, COL_MISC // LANES)),
            tab_spec, tab_spec, tab_spec,
            pl.BlockSpec((1, A_WIDTH), lambda b, i: (0, 0)),
            pl.BlockSpec((1, LANES), lambda b, i: (0, 0)),
        ],
        out_specs=[
            hq_spec, hq_spec, tok_spec, tok_spec,
            pl.BlockSpec((1, 1, A_HEAD_DIM, tp), lambda b, i: (b, i, 0, 0)),
            pl.BlockSpec((1, nq, IDX_HEADS, qb), lambda b, i: (b, i, 0, 0)),
        ],
        out_shape=[
            jax.ShapeDtypeStruct((bsz, seq // qb, A_HEAD_DIM, A_HEADS * qb), BF16),
            jax.ShapeDtypeStruct((bsz, seq // qb, IDX_DIM, IDX_HEADS * qb), BF16),
            jax.ShapeDtypeStruct((bsz, seq, A_HEAD_DIM), BF16),
            jax.ShapeDtypeStruct((bsz, seq, IDX_DIM), BF16),
            jax.ShapeDtypeStruct((bsz, seq // tp, A_HEAD_DIM, tp), BF16),
            jax.ShapeDtypeStruct((bsz, seq // qb, IDX_HEADS, qb), F32),
        ],
        compiler_params=_cparams(("parallel", "parallel")),
        name="dsa_prep",
    )(proj, proj, proj, proj, cos, s_up, s_dn, gq, gk)


def _rope_tables(seq):
    half = ROPE_DIM // 2
    inv_freq = ROPE_THETA ** (-jnp.arange(0, ROPE_DIM, 2, dtype=F32) / ROPE_DIM)
    ang = jnp.arange(seq, dtype=F32)[:, None] * inv_freq[None, :]
    cos, sin = jnp.cos(ang), jnp.sin(ang)
    pad = A_HEAD_DIM - ROPE_DIM
    cos64 = jnp.concatenate([cos, cos, jnp.ones((seq, pad), F32)], axis=1)
    up64 = jnp.concatenate([jnp.zeros((seq, half), F32), sin, jnp.zeros((seq, pad), F32)], axis=1)
    dn64 = jnp.concatenate([-sin, jnp.zeros((seq, half + pad), F32)], axis=1)
    rep = LANES // A_HEAD_DIM
    return jnp.tile(cos64, (1, rep)), jnp.tile(up64, (1, rep)), jnp.tile(dn64, (1, rep))


def _dsa_kernel(qt_ref, iqt_ref, iwt_ref, k_ref, vt_ref, ik_ref, got_ref, o_ref,
                key_scr, bias_scr, m_scr, l_scr, acc_scr, kn_scr, *, topk):
    qb, kcs = Q_BLOCK, key_scr.shape[1]
    blk = pl.program_id(1)
    n_chunks = (blk * qb) // kcs + 1
    krow = lax.broadcasted_iota(jnp.int32, (kcs, qb), 0)
    qcol = lax.broadcasted_iota(jnp.int32, (kcs, qb), 1)
    qpos = blk * qb + qcol
    iw = iwt_ref[0, 0] * (IDX_HEADS ** -0.5)

    def key_rows(kc):
        return pl.ds(pl.multiple_of(kc * kcs, kcs), kcs)

    def score_chunk(kc, carry):
        ks = ik_ref[0, key_rows(kc), :]
        logits = _dot(ks, iqt_ref[0, 0])
        acc = jnp.zeros((kcs, qb), F32)
        for h in range(IDX_HEADS):
            acc = acc + jnp.maximum(logits[:, h * qb:(h + 1) * qb], 0.0) * iw[h:h + 1, :]
        acc = jnp.where(acc == 0.0, 0.0, acc)
        sc = jnp.where(kc * kcs + krow <= qpos, acc, -jnp.inf)
        bits = lax.bitcast_convert_type(sc, jnp.int32)
        key_scr[kc] = jnp.where(bits >= 0, bits, bits ^ 0x7FFFFFFF)
        return carry

    lax.fori_loop(0, n_chunks, score_chunk, 0)

    fold = min(COUNT_ROWS, kcs)

    def count(pred_fn):
        def body(kc, c):
            hit = jnp.where(pred_fn(key_scr[kc]), 1.0, 0.0)
            return c + jnp.sum(hit.reshape(kcs // fold, fold, qb), axis=0)
        c = lax.fori_loop(0, n_chunks, body, jnp.zeros((fold, qb), F32))
        return jnp.sum(c, axis=0, keepdims=True)

    kf = float(topk)
    zero = jnp.zeros((1, qb), jnp.int32)
    n_all = float(kcs) * n_chunks.astype(F32)
    c0 = count(lambda k: k >= zero)
    thr0 = jnp.where(c0 >= kf, zero, jnp.full((1, qb), INT_MIN, jnp.int32))
    n_ge0 = jnp.where(c0 >= kf, c0, jnp.zeros((1, qb), F32) + n_all)

    def bit_step(i, carry):
        thr, n_ge = carry
        cand = thr | jnp.left_shift(jnp.int32(1), 30 - i)
        c = count(lambda k: k >= cand)
        return jnp.where(c >= kf, cand, thr), jnp.where(c >= kf, c, n_ge)

    thr, n_ge = lax.fori_loop(0, 31, bit_step, (thr0, n_ge0))

    tie_overflow = jnp.max(n_ge) > kf

    @pl.when(blk == 0)
    def _():
        kf32 = k_ref[0].astype(F32)
        kn_scr[...] = jnp.full(kn_scr.shape, jnp.max(jnp.sum(kf32 * kf32, axis=-1, keepdims=True)))

    qf32 = qt_ref[0, 0].astype(F32)
    logit_bound_sq = jnp.max(jnp.sum(qf32 * qf32, axis=0, keepdims=True)) * kn_scr[0, 0]
    unshifted = logit_bound_sq < SAFE_LOGIT * SAFE_LOGIT
    sel_bias = jnp.where(unshifted, -jnp.sqrt(jnp.zeros((1, qb), F32) + logit_bound_sq), 0.0)

    @pl.when(jnp.logical_not(tie_overflow))
    def _():
        def select_chunk(kc, carry):
            sel = (key_scr[kc] >= thr) & (kc * kcs + krow <= qpos)
            bias_scr[kc] = jnp.where(sel, sel_bias, NEG_BIG)
            return carry

        lax.fori_loop(0, n_chunks, select_chunk, 0)

    @pl.when(tie_overflow)
    def _():
        need = kf - count(lambda k: k > thr)
        lower = (lax.broadcasted_iota(jnp.int32, (kcs, kcs), 1)
                 <= lax.broadcasted_iota(jnp.int32, (kcs, kcs), 0)).astype(BF16)

        def select_chunk(kc, carry):
            key = key_scr[kc]
            eq = key == thr
            eqf = jnp.where(eq, 1.0, 0.0)
            incl = _dot(lower, eqf.astype(BF16))
            sel = (key > thr) | (eq & (carry + incl - eqf < need))
            sel = sel & (kc * kcs + krow <= qpos)
            bias_scr[kc] = jnp.where(sel, sel_bias, NEG_BIG)
            return carry + incl[kcs - 1:kcs, :]

        lax.fori_loop(0, n_chunks, select_chunk, jnp.zeros((1, qb), F32))

    l_scr[...] = jnp.zeros(l_scr.shape, F32)
    acc_scr[...] = jnp.zeros(acc_scr.shape, F32)

    @pl.when(unshifted)
    def _():
        def attend_chunk(kc, carry):
            kk = k_ref[0, key_rows(kc), :]
            vt = vt_ref[0, kc]
            bias = bias_scr[kc]
            s = _dot(kk, qt_ref[0, 0])
            p = jnp.concatenate([jnp.exp(s[:, h * qb:(h + 1) * qb] + bias) for h in range(A_HEADS)],
                                axis=1)
            l_scr[...] += jnp.sum(p.reshape(kcs // SUBLANES, SUBLANES, A_HEADS * qb), axis=0)
            acc_scr[...] += _dot(vt, p.astype(BF16))
            return carry

        lax.fori_loop(0, n_chunks, attend_chunk, 0)

    @pl.when(jnp.logical_not(unshifted))
    def _():
        m_scr[...] = jnp.full(m_scr.shape, NEG_BIG, F32)

        def attend_chunk(kc, carry):
            kk = k_ref[0, key_rows(kc), :]
            vt = vt_ref[0, kc]
            bias = bias_scr[kc]
            for h in range(A_HEADS):
                cols = slice(h * qb, (h + 1) * qb)
                s = _dot(kk, qt_ref[0, 0, :, cols]) + bias
                m_prev = m_scr[0:1, cols]
                m_new = jnp.maximum(m_prev, jnp.max(s, axis=0, keepdims=True))
                p = jnp.exp(s - m_new)
                alpha = jnp.exp(m_prev - m_new)
                l_scr[0:1, cols] = alpha * l_scr[0:1, cols] + jnp.sum(p, axis=0, keepdims=True)
                acc_scr[:, cols] = alpha * acc_scr[:, cols] + _dot(vt, p.astype(BF16))
                m_scr[0:1, cols] = m_new
            return carry

        lax.fori_loop(0, n_chunks, attend_chunk, 0)

    o = acc_scr[...] / jnp.sum(l_scr[...], axis=0, keepdims=True)
    o = o * lax.rsqrt(jnp.mean(o * o, axis=0, keepdims=True) + EPS) * got_ref[...]
    pairs = [jnp.concatenate([o[:, 2 * p * qb:(2 * p + 1) * qb], o[:, (2 * p + 1) * qb:(2 * p + 2) * qb]],
                             axis=0).T for p in range(A_HEADS // 2)]
    o_ref[0] = jnp.concatenate(pairs, axis=1).astype(BF16)


def _dsa(qt, iqt, iwt, k_r, vt, ik_r, g_out_a):
    bsz, seq, _ = k_r.shape
    qb = Q_BLOCK
    topk = min(TOPK_MAX, seq // 4)
    n_blk = seq // qb
    lanes = A_HEADS * qb
    n_kc, kcs = vt.shape[1], vt.shape[3]
    got = jnp.repeat(g_out_a.reshape(A_HEADS, A_HEAD_DIM).T, qb, axis=1)
    k_spec = pl.BlockSpec((1, seq, A_HEAD_DIM), lambda b, i: (b, 0, 0))
    q_spec = pl.BlockSpec((1, 1, A_HEAD_DIM, lanes), lambda b, i: (b, i, 0, 0))
    return pl.pallas_call(
        functools.partial(_dsa_kernel, topk=topk),
        grid=(bsz, n_blk),
        in_specs=[
            q_spec, q_spec,
            pl.BlockSpec((1, 1, IDX_HEADS, qb), lambda b, i: (b, i, 0, 0)),
            k_spec,
            pl.BlockSpec((1, n_kc, A_HEAD_DIM, kcs), lambda b, i: (b, 0, 0, 0)),
            k_spec,
            pl.BlockSpec((A_HEAD_DIM, lanes), lambda b, i: (0, 0)),
        ],
        out_specs=pl.BlockSpec((1, qb, A_WIDTH), lambda b, i: (b, i, 0)),
        out_shape=jax.ShapeDtypeStruct((bsz, seq, A_WIDTH), BF16),
        scratch_shapes=[
            pltpu.VMEM((n_kc, kcs, qb), jnp.int32),
            pltpu.VMEM((n_kc, kcs, qb), F32),
            pltpu.VMEM((SUBLANES, lanes), F32),
            pltpu.VMEM((SUBLANES, lanes), F32),
            pltpu.VMEM((A_HEAD_DIM, lanes), F32),
            pltpu.VMEM((SUBLANES, LANES), F32),
        ],
        compiler_params=_cparams(("parallel", "arbitrary")),
        name="dsa_attention",
    )(qt, iqt, iwt, k_r, vt, ik_r, got)


def _sgu_kernel(bu_ref, bv_ref, gv_ref, ws_ref, bst_ref, go_ref, o_ref):
    ch = B_CHUNK
    u = jax.nn.gelu(bu_ref[0])
    v = jax.nn.gelu(bv_ref[0])
    vc = v - jnp.mean(v, axis=-1, keepdims=True)
    vn = vc * lax.rsqrt(jnp.mean(vc * vc, axis=-1, keepdims=True) + EPS) * gv_ref[...]
    vb = vn.astype(BF16)
    row = lax.broadcasted_iota(jnp.int32, (ch, ch), 0)
    col = lax.broadcasted_iota(jnp.int32, (ch, ch), 1)
    causal = col <= row
    grp = lax.broadcasted_iota(jnp.int32, (1, B_WIDTH), 1) // B_GROUP_DIM
    bst = bst_ref[...]
    mixed = jnp.zeros((ch, B_WIDTH), F32)
    for g in range(B_GROUPS):
        w = jnp.where(causal, ws_ref[g], 0.0).astype(BF16)
        mixed = jnp.where(grp == g, _dot(w, vb) + bst[:, g:g + 1], mixed)
    o = u * mixed
    ss = _dot_exact_rhs(o * o, _group_ones(B_WIDTH, B_GROUP_DIM))
    o_ref[0] = (o * lax.rsqrt(ss * (1.0 / B_GROUP_DIM) + EPS) * go_ref[...]).astype(BF16)


def _sgu(proj, g_v, w_s, b_s, g_out_b):
    bsz, seq, _ = proj.shape
    ch = B_CHUNK
    return pl.pallas_call(
        _sgu_kernel,
        grid=(bsz, seq // ch),
        in_specs=[
            pl.BlockSpec((1, ch, B_WIDTH), lambda b, i: (b, i, COL_BU // B_WIDTH)),
            pl.BlockSpec((1, ch, B_WIDTH), lambda b, i: (b, i, COL_BV // B_WIDTH)),
            pl.BlockSpec((1, B_WIDTH), lambda b, i: (0, 0)),
            pl.BlockSpec((B_GROUPS, ch, ch), lambda b, i: (0, 0, 0)),
            pl.BlockSpec((ch, B_GROUPS), lambda b, i: (0, 0)),
            pl.BlockSpec((1, B_WIDTH), lambda b, i: (0, 0)),
        ],
        out_specs=pl.BlockSpec((1, ch, B_WIDTH), lambda b, i: (b, i, 0)),
        out_shape=jax.ShapeDtypeStruct((bsz, seq, B_WIDTH), BF16),
        compiler_params=_cparams(("parallel", "parallel")),
        name="spatial_gating",
    )(proj, proj, g_v.reshape(1, B_WIDTH), w_s, b_s.T, g_out_b.reshape(1, B_WIDTH))


def _log_sigmoid(z):
    return jnp.minimum(z, 0.0) - jnp.log1p(jnp.exp(-jnp.abs(z)))


def _gla_kernel(cq_ref, ck_ref, cv_ref, cr_ref, misc_ref, wa_ref, ba_ref, go_ref, o_ref, st_scr):
    tg, ch = cq_ref.shape[1], C_CHUNK

    @pl.when(pl.program_id(1) == 0)
    def _():
        st_scr[...] = jnp.zeros(st_scr.shape, F32)

    z = _dot3(misc_ref[0], wa_ref[...]) + ba_ref[...]
    log_a = _log_sigmoid(z) * (1.0 / C_GATE_TAU)
    r = lax.broadcasted_iota(jnp.int32, (tg, tg), 0)
    c = lax.broadcasted_iota(jnp.int32, (tg, tg), 1)
    same = (r // ch) == (c // ch)
    b = _dot_exact_lhs((same & (c <= r)).astype(BF16), log_a)
    b_last = _dot_exact_lhs(same.astype(BF16), log_a)
    ck = ck_ref[0]
    q_dec = cq_ref[0] * (C_KEY_DIM ** -0.5) * jnp.exp(b)
    k_neg = (ck * jnp.exp(-b)).astype(BF16)
    k_st = (ck * jnp.exp(b_last - b)).astype(BF16)
    cv = cv_ref[0].astype(BF16)

    kp, vp = 2 * C_KEY_DIM, 2 * C_VAL_DIM
    lane_head = lax.broadcasted_iota(jnp.int32, (2, 1, kp), 2) // C_KEY_DIM
    head_mask = lane_head == lax.broadcasted_iota(jnp.int32, (2, 1, kp), 0)
    tril = (lax.broadcasted_iota(jnp.int32, (1, ch, ch), 2)
            <= lax.broadcasted_iota(jnp.int32, (1, ch, ch), 1))
    pair_diag = (lax.broadcasted_iota(jnp.int32, (kp, vp), 0) // C_KEY_DIM
                 == lax.broadcasted_iota(jnp.int32, (kp, vp), 1) // C_VAL_DIM)
    ones_cols = jnp.ones((ch, LANES), BF16)

    for n in range(tg // ch):
        rs = slice(n * ch, (n + 1) * ch)
        la_hi, la_lo = _split(log_a[rs])
        decay = jnp.exp(_dot_tn(la_hi, ones_cols) + _dot_tn(la_lo, ones_cols))[:, 0:1]
        parts = []
        for p in range(C_HEADS // 2):
            kl, vl = slice(p * kp, (p + 1) * kp), slice(p * vp, (p + 1) * vp)
            qd = q_dec[rs, kl]
            vn = cv[rs, vl]
            qm = jnp.where(head_mask, qd[None], 0.0).reshape(2 * ch, kp).astype(BF16)
            att = _dot_nt(qm, k_neg[rs, kl]).reshape(2, ch, ch)
            att = jnp.where(tril, att, 0.0).astype(BF16)
            o_intra = jnp.concatenate([_dot(att[0], vn[:, :C_VAL_DIM]), _dot(att[1], vn[:, C_VAL_DIM:])],
                                      axis=1)
            state = st_scr[p]
            o = o_intra + _dot(qd.astype(BF16), state.astype(BF16))
            st_scr[p] = jnp.where(pair_diag, decay[kl] * state + _dot_tn(k_st[rs, kl], vn), 0.0)
            for oh in (o[:, :C_VAL_DIM], o[:, C_VAL_DIM:]):
                parts.append(oh * lax.rsqrt(jnp.mean(oh * oh, axis=-1, keepdims=True) + EPS))
        cr = cr_ref[0, rs, :]
        o_ref[0, rs, :] = (jnp.concatenate(parts, axis=1) * (cr * jax.nn.sigmoid(cr))
                           * go_ref[...]).astype(BF16)


def _gla(proj, w_a2, b_a, g_out_c):
    bsz, seq, _ = proj.shape
    tg = min(T_GLA, seq)
    wa = jnp.zeros((LANES, C_KWIDTH), F32).at[MISC_CA:MISC_CA + C_GATE_RANK].set(w_a2)
    return pl.pallas_call(
        _gla_kernel,
        grid=(bsz, seq // tg),
        in_specs=[
            pl.BlockSpec((1, tg, C_KWIDTH), lambda b, i: (b, i, COL_CQ // C_KWIDTH)),
            pl.BlockSpec((1, tg, C_KWIDTH), lambda b, i: (b, i, COL_CK // C_KWIDTH)),
            pl.BlockSpec((1, tg, C_WIDTH), lambda b, i: (b, i, COL_CV // C_WIDTH)),
            pl.BlockSpec((1, tg, C_WIDTH), lambda b, i: (b, i, COL_CR // C_WIDTH)),
            pl.BlockSpec((1, tg, LANES), lambda b, i: (b, i, COL_MISC // LANES)),
            pl.BlockSpec((LANES, C_KWIDTH), lambda b, i: (0, 0)),
            pl.BlockSpec((1, C_KWIDTH), lambda b, i: (0, 0)),
            pl.BlockSpec((1, C_WIDTH), lambda b, i: (0, 0)),
        ],
        out_specs=pl.BlockSpec((1, tg, C_WIDTH), lambda b, i: (b, i, 0)),
        out_shape=jax.ShapeDtypeStruct((bsz, seq, C_WIDTH), BF16),
        scratch_shapes=[pltpu.VMEM((C_HEADS // 2, 2 * C_KEY_DIM, 2 * C_VAL_DIM), F32)],
        compiler_params=_cparams(("parallel", "arbitrary")),
        name="gla",
    )(proj, proj, proj, proj, proj, wa, b_a.reshape(1, C_KWIDTH), g_out_c.reshape(1, C_WIDTH))


def _out_proj_kernel(oa_ref, ob_ref, oc_ref, w_ref, x_ref, mod_ref, o_ref):
    y = _dot(oa_ref[0], w_ref[0:A_WIDTH, :])
    y = y + _dot(ob_ref[0], w_ref[A_WIDTH:A_WIDTH + B_WIDTH, :])
    y = y + _dot(oc_ref[0], w_ref[A_WIDTH + B_WIDTH:, :])
    o_ref[0] = x_ref[0] + mod_ref[0, 2:3, :] * y


def _out_proj(o_a, o_b, o_c, w_out_bf16, x, mod_l):
    bsz, seq, d = x.shape
    tm = min(TM_MM, seq)
    dm = w_out_bf16.shape[0]
    return pl.pallas_call(
        _out_proj_kernel,
        grid=(bsz, seq // tm),
        in_specs=[
            pl.BlockSpec((1, tm, A_WIDTH), lambda b, i: (b, i, 0)),
            pl.BlockSpec((1, tm, B_WIDTH), lambda b, i: (b, i, 0)),
            pl.BlockSpec((1, tm, C_WIDTH), lambda b, i: (b, i, 0)),
            pl.BlockSpec((dm, d), lambda b, i: (0, 0)),
            pl.BlockSpec((1, tm, d), lambda b, i: (b, i, 0)),
            pl.BlockSpec((1, N_ADA, d), lambda b, i: (b, 0, 0)),
        ],
        out_specs=pl.BlockSpec((1, tm, d), lambda b, i: (b, i, 0)),
        out_shape=jax.ShapeDtypeStruct((bsz, seq, d), F32),
        compiler_params=_cparams(("parallel", "parallel")),
        name="out_proj",
    )(o_a, o_b, o_c, w_out_bf16, x, mod_l)


def _swiglu_step(h, wg_ref, wu_ref, wd_ref, acc_scr, f, n_f, f_tail):
    def step(width):
        gate = _dot(h, wg_ref[:, :width])
        a = gate * jax.nn.sigmoid(gate) * _dot(h, wu_ref[:, :width])
        acc_scr[...] += _dot(a.astype(BF16), wd_ref[:width, :])

    tf = wg_ref.shape[1]
    if f_tail == tf:
        step(tf)
        return

    @pl.when(f < n_f - 1)
    def _():
        step(tf)

    @pl.when(f == n_f - 1)
    def _():
        step(f_tail)


def _hidden_tiles(f_width):
    n_f = pl.cdiv(f_width, TF_FFN)
    return n_f, f_width - (n_f - 1) * TF_FFN


def _ffn_kernel(x_ref, mod_ref, g_ref, wg_ref, wu_ref, wd_ref, o_ref, h_scr, acc_scr, *, n_f, f_tail):
    f = pl.program_id(2)

    @pl.when(f == 0)
    def _():
        h = _norm_mod(x_ref[0], g_ref[...], mod_ref[0, 4:5, :], mod_ref[0, 3:4, :])
        h_scr[...] = h.astype(BF16)
        acc_scr[...] = jnp.zeros(acc_scr.shape, F32)

    _swiglu_step(h_scr[...], wg_ref, wu_ref, wd_ref, acc_scr, f, n_f, f_tail)

    @pl.when(f == n_f - 1)
    def _():
        o_ref[0] = x_ref[0] + mod_ref[0, 5:6, :] * acc_scr[...]


def _ffn(x, mod_l, g, w_gate, w_up, w_down):
    bsz, seq, d = x.shape
    tm, tf = min(TM_MM, seq), TF_FFN
    n_f, f_tail = _hidden_tiles(w_gate.shape[-1])
    row = lambda b, i, f: (b, i, 0)
    return pl.pallas_call(
        functools.partial(_ffn_kernel, n_f=n_f, f_tail=f_tail),
        grid=(bsz, seq // tm, n_f),
        in_specs=[
            pl.BlockSpec((1, tm, d), row),
            pl.BlockSpec((1, N_ADA, d), lambda b, i, f: (b, 0, 0)),
            pl.BlockSpec((1, d), lambda b, i, f: (0, 0)),
            pl.BlockSpec((d, tf), lambda b, i, f: (0, f)),
            pl.BlockSpec((d, tf), lambda b, i, f: (0, f)),
            pl.BlockSpec((tf, d), lambda b, i, f: (f, 0)),
        ],
        out_specs=pl.BlockSpec((1, tm, d), row),
        out_shape=jax.ShapeDtypeStruct((bsz, seq, d), F32),
        scratch_shapes=[pltpu.VMEM((tm, d), BF16), pltpu.VMEM((tm, d), F32)],
        compiler_params=_cparams(("parallel", "parallel", "arbitrary")),
        name="dense_ffn",
    )(x, mod_l, g.reshape(1, d), w_gate, w_up, w_down)


def _router_kernel(x_ref, mod_ref, g_ref, wr_ref, h_ref, info_ref, cnt_ref, run_scr):
    @pl.when((pl.program_id(0) == 0) & (pl.program_id(1) == 0))
    def _():
        run_scr[...] = jnp.zeros(run_scr.shape, F32)

    h = _norm_mod(x_ref[0], g_ref[...], mod_ref[0, 4:5, :], mod_ref[0, 3:4, :])
    h_ref[0] = h
    logits = _dot3(h, wr_ref[...])
    tm = logits.shape[0]
    lane = lax.broadcasted_iota(jnp.int32, logits.shape, 1).astype(F32)
    logits = jnp.where(lane < N_EXPERTS, logits, -jnp.inf)
    m1 = jnp.max(logits, axis=-1, keepdims=True)
    i1 = jnp.min(jnp.where(logits == m1, lane, float(LANES)), axis=-1, keepdims=True)
    rest = jnp.where(lane == i1, -jnp.inf, logits)
    m2 = jnp.max(rest, axis=-1, keepdims=True)
    i2 = jnp.min(jnp.where(rest == m2, lane, float(LANES)), axis=-1, keepdims=True)
    e2 = jnp.exp(m2 - m1)
    den = 1.0 + e2
    hit1, hit2 = lane == i1, lane == i2
    hits = jnp.where(hit1 | hit2, 1.0, 0.0)
    earlier = (lax.broadcasted_iota(jnp.int32, (tm, tm), 1)
               < lax.broadcasted_iota(jnp.int32, (tm, tm), 0)).astype(BF16)
    rank = run_scr[0:1, :] + _dot(earlier, hits.astype(BF16))
    r1 = jnp.sum(jnp.where(hit1, rank, 0.0), axis=-1, keepdims=True)
    r2 = jnp.sum(jnp.where(hit2, rank, 0.0), axis=-1, keepdims=True)
    run_scr[0:1, :] = run_scr[0:1, :] + jnp.sum(hits, axis=0, keepdims=True)
    info = jnp.zeros(logits.shape, F32)
    for k, val in ((INFO_E1, i1), (INFO_E2, i2), (INFO_R1, r1), (INFO_R2, r2),
                   (INFO_W1, 1.0 / den), (INFO_W2, e2 / den)):
        info = jnp.where(lane == float(k), val, info)
    info_ref[0] = info
    cnt_ref[...] = jnp.broadcast_to(run_scr[0:1, :], cnt_ref.shape)


def _router(x, mod_l, g, w_router):
    bsz, seq, d = x.shape
    tm = min(TM_MM, seq)
    wr = jnp.pad(w_router, ((0, 0), (0, LANES - N_EXPERTS)))
    return pl.pallas_call(
        _router_kernel,
        grid=(bsz, seq // tm),
        in_specs=[
            pl.BlockSpec((1, tm, d), lambda b, i: (b, i, 0)),
            pl.BlockSpec((1, N_ADA, d), lambda b, i: (b, 0, 0)),
            pl.BlockSpec((1, d), lambda b, i: (0, 0)),
            pl.BlockSpec((d, LANES), lambda b, i: (0, 0)),
        ],
        out_specs=[
            pl.BlockSpec((1, tm, d), lambda b, i: (b, i, 0)),
            pl.BlockSpec((1, tm, LANES), lambda b, i: (b, i, 0)),
            pl.BlockSpec((SUBLANES, LANES), lambda b, i: (0, 0)),
        ],
        out_shape=[
            jax.ShapeDtypeStruct((bsz, seq, d), F32),
            jax.ShapeDtypeStruct((bsz, seq, LANES), F32),
            jax.ShapeDtypeStruct((SUBLANES, LANES), F32),
        ],
        scratch_shapes=[pltpu.VMEM((SUBLANES, LANES), F32)],
        compiler_params=_cparams(("arbitrary", "arbitrary")),
        name="router",
    )(x, mod_l, g.reshape(1, d), wr)


def _row_copy(src, dst, sem):
    return pltpu.make_async_copy(src, dst, sem)


def _dispatch_kernel(p1_ref, p2_ref, pad0_ref, padn_ref, h_ref, out_hbm, sem, zrow):
    g = h_ref.shape[0]
    base = pl.program_id(0) * g

    @pl.when(pl.program_id(0) == 0)
    def _():
        zrow[...] = jnp.zeros(zrow.shape, F32)
        zero_row = zrow.at[pl.ds(0, 1)]
        n_pad = 0
        for e in range(N_EXPERTS + 1):
            def zero_issue(r, carry, e=e):
                _row_copy(zero_row, out_hbm.at[pl.ds(pad0_ref[e] + r, 1)], sem).start()
                return carry

            lax.fori_loop(0, padn_ref[e], zero_issue, 0)
            n_pad = n_pad + padn_ref[e]

        def zero_drain(r, carry):
            _row_copy(zero_row, out_hbm.at[pl.ds(0, 1)], sem).wait()
            return carry

        lax.fori_loop(0, n_pad, zero_drain, 0)

    def issue(r, carry):
        row = h_ref.at[pl.ds(r, 1)]
        _row_copy(row, out_hbm.at[pl.ds(p1_ref[base + r], 1)], sem).start()
        _row_copy(row, out_hbm.at[pl.ds(p2_ref[base + r], 1)], sem).start()
        return carry

    lax.fori_loop(0, g, issue, 0, unroll=DMA_UNROLL)

    def drain(r, carry):
        _row_copy(h_ref.at[pl.ds(0, 1)], out_hbm.at[pl.ds(0, 1)], sem).wait()
        return carry

    lax.fori_loop(0, 2 * g, drain, 0, unroll=DMA_UNROLL)


def _dispatch(h2d, pos1, pos2, pad_start, pad_len, n_rows):
    n, d = h2d.shape
    g = min(G_ROWS, n)
    grid_spec = pltpu.PrefetchScalarGridSpec(
        num_scalar_prefetch=4,
        grid=(n // g,),
        in_specs=[pl.BlockSpec((g, d), lambda i, p1, p2, s0, sn: (i, 0))],
        out_specs=pl.BlockSpec(memory_space=pl.ANY),
        scratch_shapes=[pltpu.SemaphoreType.DMA, pltpu.VMEM((SUBLANES, d), F32)],
    )
    return pl.pallas_call(
        _dispatch_kernel,
        grid_spec=grid_spec,
        out_shape=jax.ShapeDtypeStruct((n_rows, d), F32),
        compiler_params=_cparams(("arbitrary",)),
        name="moe_dispatch",
    )(pos1, pos2, pad_start, pad_len, h2d)


def _moe_ffn_kernel(te_ref, nu_ref, hs_ref, wg_ref, wu_ref, wd_ref, o_ref, h_scr, acc_scr, *, n_f,
                    f_tail):
    del te_ref
    j, f = pl.program_id(0), pl.program_id(1)
    used = j < nu_ref[0]

    @pl.when(used & (f == 0))
    def _():
        h_scr[...] = hs_ref[...].astype(BF16)
        acc_scr[...] = jnp.zeros(acc_scr.shape, F32)

    @pl.when(used)
    def _():
        _swiglu_step(h_scr[...], wg_ref, wu_ref, wd_ref, acc_scr, f, n_f, f_tail)

    @pl.when(used & (f == n_f - 1))
    def _():
        o_ref[...] = acc_scr[...]

    @pl.when(jnp.logical_not(used) & (f == n_f - 1))
    def _():
        o_ref[...] = jnp.zeros(o_ref.shape, F32)


def _moe_ffn(h_sorted, tile_expert, n_used, w_gate, w_up, w_down):
    n_rows, d = h_sorted.shape
    tm, tf = TM_MOE, TF_FFN
    n_f, f_tail = _hidden_tiles(w_gate.shape[-1])
    n_tiles = n_rows // tm

    def live(j, nu):
        return jnp.minimum(j, nu[0] - 1)

    def fcol(j, f, nu):
        return jnp.where(j < nu[0], f, n_f - 1)

    grid_spec = pltpu.PrefetchScalarGridSpec(
        num_scalar_prefetch=2,
        grid=(n_tiles, n_f),
        in_specs=[
            pl.BlockSpec((tm, d), lambda j, f, te, nu: (live(j, nu), 0)),
            pl.BlockSpec((None, d, tf), lambda j, f, te, nu: (te[live(j, nu)], 0, fcol(j, f, nu))),
            pl.BlockSpec((None, d, tf), lambda j, f, te, nu: (te[live(j, nu)], 0, fcol(j, f, nu))),
            pl.BlockSpec((None, tf, d), lambda j, f, te, nu: (te[live(j, nu)], fcol(j, f, nu), 0)),
        ],
        out_specs=pl.BlockSpec((tm, d), lambda j, f, te, nu: (j, 0)),
        scratch_shapes=[pltpu.VMEM((tm, d), BF16), pltpu.VMEM((tm, d), F32)],
    )
    return pl.pallas_call(
        functools.partial(_moe_ffn_kernel, n_f=n_f, f_tail=f_tail),
        grid_spec=grid_spec,
        out_shape=jax.ShapeDtypeStruct((n_rows, d), F32),
        compiler_params=_cparams(("arbitrary", "arbitrary")),
        name="moe_ffn",
    )(tile_expert, n_used, h_sorted, w_gate, w_up, w_down)


def _combine_kernel(p1_ref, p2_ref, x_ref, mod_ref, info_ref, y_hbm, o_ref, buf, sem):
    g = x_ref.shape[1]
    base = (pl.program_id(0) * pl.num_programs(1) + pl.program_id(1)) * g

    def issue(r, carry):
        _row_copy(y_hbm.at[pl.ds(p1_ref[base + r], 1)], buf.at[0, pl.ds(r, 1)], sem).start()
        _row_copy(y_hbm.at[pl.ds(p2_ref[base + r], 1)], buf.at[1, pl.ds(r, 1)], sem).start()
        return carry

    lax.fori_loop(0, g, issue, 0, unroll=DMA_UNROLL)

    def drain(r, carry):
        _row_copy(y_hbm.at[pl.ds(0, 1)], buf.at[0, pl.ds(0, 1)], sem).wait()
        return carry

    lax.fori_loop(0, 2 * g, drain, 0, unroll=DMA_UNROLL)
    info = info_ref[0]
    y = info[:, INFO_W1:INFO_W1 + 1] * buf[0] + info[:, INFO_W2:INFO_W2 + 1] * buf[1]
    o_ref[0] = x_ref[0] + mod_ref[0, 5:6, :] * y


def _combine(x, mod_l, info, y_sorted, pos1, pos2):
    bsz, seq, d = x.shape
    g = min(G_ROWS, seq)
    grid_spec = pltpu.PrefetchScalarGridSpec(
        num_scalar_prefetch=2,
        grid=(bsz, seq // g),
        in_specs=[
            pl.BlockSpec((1, g, d), lambda b, i, p1, p2: (b, i, 0)),
            pl.BlockSpec((1, N_ADA, d), lambda b, i, p1, p2: (b, 0, 0)),
            pl.BlockSpec((1, g, LANES), lambda b, i, p1, p2: (b, i, 0)),
            pl.BlockSpec(memory_space=pl.ANY),
        ],
        out_specs=pl.BlockSpec((1, g, d), lambda b, i, p1, p2: (b, i, 0)),
        scratch_shapes=[pltpu.VMEM((2, g, d), F32), pltpu.SemaphoreType.DMA],
    )
    return pl.pallas_call(
        _combine_kernel,
        grid_spec=grid_spec,
        out_shape=jax.ShapeDtypeStruct((bsz, seq, d), F32),
        compiler_params=_cparams(("arbitrary", "arbitrary")),
        name="moe_combine",
    )(pos1, pos2, x, mod_l, info, y_sorted)


def _moe(x, mod_l, g, w_router, first_expert, w_gate, w_up, w_down):
    bsz, seq, d = x.shape
    n = bsz * seq
    tm = TM_MOE
    h, info, counts = _router(x, mod_l, g, w_router)
    cnt = counts[0, :N_EXPERTS].astype(jnp.int32)
    padded = (cnt + tm - 1) // tm * tm
    ends = jnp.cumsum(padded)
    starts = ends - padded
    n_rows = 2 * n + N_EXPERTS * tm
    tile_start = jnp.arange(n_rows // tm, dtype=jnp.int32) * tm
    tile_expert = jnp.minimum(jnp.sum(tile_start[:, None] >= ends[None, :], axis=1), N_EXPERTS - 1)
    n_used = (ends[-1] // tm).reshape(1).astype(jnp.int32)
    rec = info.reshape(n, LANES)
    e1, e2 = rec[:, INFO_E1].astype(jnp.int32), rec[:, INFO_E2].astype(jnp.int32)
    experts = jnp.arange(N_EXPERTS, dtype=jnp.int32)[None, :]
    pos1 = jnp.sum(jnp.where(e1[:, None] == experts, starts[None, :], 0), axis=1) + rec[:, INFO_R1].astype(jnp.int32)
    pos2 = jnp.sum(jnp.where(e2[:, None] == experts, starts[None, :], 0), axis=1) + rec[:, INFO_R2].astype(jnp.int32)
    pad_start = jnp.concatenate([starts + cnt, ends[-1:]]).astype(jnp.int32)
    pad_len = jnp.concatenate([padded - cnt, n_rows - ends[-1:]]).astype(jnp.int32)
    h_sorted = _dispatch(h.reshape(n, d), pos1, pos2, pad_start, pad_len, n_rows)
    y_sorted = _moe_ffn(h_sorted, (tile_expert + first_expert).astype(jnp.int32), n_used,
                        w_gate, w_up, w_down)
    return _combine(x, mod_l, info, y_sorted, pos1, pos2)


def _reorder_w_in(w):
    sizes = (A_WIDTH, A_HEAD_DIM, A_HEAD_DIM, I_WIDTH, IDX_DIM, IDX_HEADS, B_WIDTH, B_WIDTH,
             C_KWIDTH, C_KWIDTH, C_WIDTH, C_GATE_RANK, C_WIDTH)
    offs = [0]
    for s in sizes:
        offs.append(offs[-1] + s)
    aq, ak, av, iq, ik, iw, bu, bv, cq, ck, cv, ca, cr = (
        w[:, offs[i]:offs[i + 1]] for i in range(len(sizes)))
    pad = jnp.zeros((w.shape[0], LANES - IDX_DIM - IDX_HEADS - C_GATE_RANK), w.dtype)
    return jnp.concatenate([aq, iq, bu, bv, cq, ck, cv, cr, ak, av, ik, iw, ca, pad],
                           axis=1).astype(BF16)


def _stack_experts(w):
    return w.astype(BF16).reshape((w.shape[0] * w.shape[1],) + w.shape[2:])


def kernel(x, c, w_ada, b_ada, g_norm1, g_norm2, w_in, g_q, g_k, g_v_b, w_s, b_s, w_a2, b_a,
           g_out, w_out, w_ff_gate, w_ff_up, w_ff_down, w_router, w_e_gate, w_e_up, w_e_down):
    depth = w_in.shape[0]
    mod = _ada_mod(c, w_ada, b_ada)
    tables = _rope_tables(x.shape[1])
    expert_w = tuple(_stack_experts(w) for w in (w_e_gate, w_e_up, w_e_down))
    for layer in range(depth):
        mod_l = mod[layer]
        proj = _in_proj(x, mod_l, g_norm1[layer], _reorder_w_in(w_in[layer]))
        qt, iqt, k_r, ik_r, vt, iwt = _dsa_prep(proj, tables, g_q[layer], g_k[layer])
        o_a = _dsa(qt, iqt, iwt, k_r, vt, ik_r, g_out[layer, :A_WIDTH])
        o_b = _sgu(proj, g_v_b[layer], w_s[layer], b_s[layer], g_out[layer, A_WIDTH:A_WIDTH + B_WIDTH])
        o_c = _gla(proj, w_a2[layer], b_a[layer], g_out[layer, A_WIDTH + B_WIDTH:])
        x = _out_proj(o_a, o_b, o_c, w_out[layer].astype(BF16), x, mod_l)
        j = layer // 2
        if layer % 2 == 0:
            x = _ffn(x, mod_l, g_norm2[layer], w_ff_gate[j].astype(BF16), w_ff_up[j].astype(BF16),
                     w_ff_down[j].astype(BF16))
        else:
            x = _moe(x, mod_l, g_norm2[layer], w_router[j], j * N_EXPERTS, *expert_w)
    return x
```

```python
import functools

import jax
import jax.numpy as jnp
from jax import lax
from jax.experimental import pallas as pl
from jax.experimental.pallas import tpu as pltpu

F32 = jnp.float32
BF16 = jnp.bfloat16

A_HEADS = 8
A_HEAD_DIM = 64
IDX_HEADS = 8
IDX_DIM = 64
TOPK_MAX = 256
B_GROUPS = 8
B_GROUP_DIM = 64
B_CHUNK = 128
C_HEADS = 8
C_VAL_DIM = 128
C_KEY_DIM = 64
C_GATE_RANK = 16
C_GATE_TAU = 16.0
C_CHUNK = 64
ROPE_THETA = 500000.0
ROPE_DIM = 16
N_EXPERTS = 8
N_ADA = 6
EPS = 1e-6

A_WIDTH = A_HEADS * A_HEAD_DIM
I_WIDTH = IDX_HEADS * IDX_DIM
B_WIDTH = B_GROUPS * B_GROUP_DIM
C_KWIDTH = C_HEADS * C_KEY_DIM
C_WIDTH = C_HEADS * C_VAL_DIM

LANES = 128
SUBLANES = 8
VMEM_LIMIT_BYTES = 56 * 1024 * 1024

COL_AQ = 0
COL_IQ = 512
COL_BU = 1024
COL_BV = 1536
COL_CQ = 2048
COL_CK = 2560
COL_CV = 3072
COL_CR = 4096
COL_AKV = 5120
COL_MISC = 5248
N_PROJ = 5376
MISC_IW = IDX_DIM
MISC_CA = IDX_DIM + IDX_HEADS

NEG_BIG = -1e30
SAFE_LOGIT = 40.0
INT_MIN = -(2 ** 31)
COUNT_ROWS = 64

TM_MM = 512
TN_IN = 1792
DMA_UNROLL = 8
TF_FFN = 512
TN_ADA = 1024
Q_BLOCK = 128
K_CHUNK = 512
T_GLA = 256
TM_MOE = 512
G_ROWS = 256

INFO_E1, INFO_E2, INFO_R1, INFO_R2, INFO_W1, INFO_W2 = range(6)


def _cparams(sem):
    return pltpu.CompilerParams(dimension_semantics=sem, vmem_limit_bytes=VMEM_LIMIT_BYTES)


def _dot(a, b):
    return jnp.dot(a, b, preferred_element_type=F32)


def _dot_nt(a, b):
    return lax.dot_general(a, b, (((1,), (1,)), ((), ())), preferred_element_type=F32)


def _dot_tn(a, b):
    return lax.dot_general(a, b, (((0,), (0,)), ((), ())), preferred_element_type=F32)


def _split(x):
    hi = x.astype(BF16)
    lo = (x - hi.astype(F32)).astype(BF16)
    return hi, lo


def _dot_exact_lhs(m_bf16, x):
    hi, lo = _split(x)
    return _dot(m_bf16, hi) + _dot(m_bf16, lo)


def _dot_exact_rhs(x, m_bf16):
    hi, lo = _split(x)
    return _dot(hi, m_bf16) + _dot(lo, m_bf16)


def _dot3(a, b):
    ah, al = _split(a)
    bh, bl = _split(b)
    return _dot(ah, bh) + (_dot(al, bh) + _dot(ah, bl))


def _norm_mod(x, g, scale, shift):
    ms = jnp.mean(x * x, axis=-1, keepdims=True)
    return (x * lax.rsqrt(ms + EPS) * g) * (1.0 + scale) + shift


def _group_ones(width, group):
    r = lax.broadcasted_iota(jnp.int32, (width, width), 0) // group
    c = lax.broadcasted_iota(jnp.int32, (width, width), 1) // group
    return (r == c).astype(BF16)


def _ada_kernel(c_ref, w_ref, b_ref, o_ref):
    c = c_ref[...]
    cond = (c * jax.nn.sigmoid(c)).astype(BF16)
    o_ref[0] = _dot(cond, w_ref[0].astype(BF16)) + b_ref[0]


def _ada_mod(c, w_ada, b_ada):
    depth, d, n6 = w_ada.shape
    bsz = c.shape[0]
    rows = 16
    c_pad = jnp.pad(c, ((0, rows - bsz), (0, 0)))
    tn = TN_ADA
    out = pl.pallas_call(
        _ada_kernel,
        grid=(depth, n6 // tn),
        in_specs=[
            pl.BlockSpec((rows, d), lambda l, j: (0, 0)),
            pl.BlockSpec((1, d, tn), lambda l, j: (l, 0, j)),
            pl.BlockSpec((1, 1, tn), lambda l, j: (l, 0, j)),
        ],
        out_specs=pl.BlockSpec((1, rows, tn), lambda l, j: (l, 0, j)),
        out_shape=jax.ShapeDtypeStruct((depth, rows, n6), F32),
        compiler_params=_cparams(("parallel", "parallel")),
        name="ada_mod",
    )(c_pad, w_ada, b_ada.reshape(depth, 1, n6))
    return out[:, :bsz].reshape(depth, bsz, N_ADA, d)


def _in_proj_kernel(x_ref, mod_ref, g_ref, w_ref, o_ref, h_scr):
    @pl.when(pl.program_id(2) == 0)
    def _():
        h = _norm_mod(x_ref[0], g_ref[...], mod_ref[0, 1:2, :], mod_ref[0, 0:1, :])
        h_scr[...] = h.astype(BF16)

    o_ref[0] = _dot(h_scr[...], w_ref[...])


def _in_proj(x, mod_l, g, w_p):
    bsz, seq, d = x.shape
    n = w_p.shape[1]
    tm, tn = min(TM_MM, seq), TN_IN
    return pl.pallas_call(
        _in_proj_kernel,
        grid=(bsz, seq // tm, n // tn),
        in_specs=[
            pl.BlockSpec((1, tm, d), lambda b, i, j: (b, i, 0)),
            pl.BlockSpec((1, N_ADA, d), lambda b, i, j: (b, 0, 0)),
            pl.BlockSpec((1, d), lambda b, i, j: (0, 0)),
            pl.BlockSpec((d, tn), lambda b, i, j: (0, j)),
        ],
        out_specs=pl.BlockSpec((1, tm, tn), lambda b, i, j: (b, i, j)),
        out_shape=jax.ShapeDtypeStruct((bsz, seq, n), F32),
        scratch_shapes=[pltpu.VMEM((tm, d), BF16)],
        compiler_params=_cparams(("parallel", "parallel", "arbitrary")),
        name="in_proj",
    )(x, mod_l, g.reshape(1, d), w_p)


def _rope(x, cos, s_up, s_dn):
    parts = []
    for j in range(x.shape[1] // LANES):
        xs = x[:, j * LANES:(j + 1) * LANES]
        parts.append(xs * cos + pltpu.roll(xs, ROPE_DIM // 2, 1) * s_up
                     + pltpu.roll(xs, LANES - ROPE_DIM // 2, 1) * s_dn)
    return parts[0] if len(parts) == 1 else jnp.concatenate(parts, axis=1)


def _heads_to_lanes(x, heads):
    parts = []
    for p in range(heads // 2):
        t = x[:, p * LANES:(p + 1) * LANES].T
        parts += [t[:A_HEAD_DIM], t[A_HEAD_DIM:]]
    return jnp.concatenate(parts, axis=1)


def _dsa_prep_kernel(aq_ref, iq_ref, akv_ref, misc_ref, cos_ref, sup_ref, sdn_ref, gq_ref, gk_ref,
                     qt_out, iqt_out, k_out, ik_out, vt_out, iwt_out):
    qb = Q_BLOCK
    cos, s_up, s_dn = cos_ref[...], sup_ref[...], sdn_ref[...]
    aq = aq_ref[0]
    ss = _dot_exact_rhs(aq * aq, _group_ones(A_WIDTH, A_HEAD_DIM))
    qn = aq * lax.rsqrt(ss * (1.0 / A_HEAD_DIM) + EPS) * gq_ref[...]
    q = _rope(qn, cos, s_up, s_dn) * (A_HEAD_DIM ** -0.5)
    iq = _rope(iq_ref[0], cos, s_up, s_dn) * (IDX_DIM ** -0.5)
    lane = lax.broadcasted_iota(jnp.int32, (1, LANES), 1)
    first = lane < A_HEAD_DIM
    akv = akv_ref[0]
    kss = jnp.sum(jnp.where(first, akv * akv, 0.0), axis=-1, keepdims=True)
    kn = akv * lax.rsqrt(kss * (1.0 / A_HEAD_DIM) + EPS) * gk_ref[...]
    k_out[0] = _rope(kn, cos, s_up, s_dn)[:, :A_HEAD_DIM].astype(BF16)
    misc = misc_ref[0]
    ik_out[0] = _rope(misc, cos, s_up, s_dn)[:, :IDX_DIM].astype(BF16)
    vts = []
    for j in range(aq.shape[0] // qb):
        rows = slice(j * qb, (j + 1) * qb)
        qt_out[0, j] = _heads_to_lanes(q[rows], A_HEADS).astype(BF16)
        iqt_out[0, j] = _heads_to_lanes(iq[rows], IDX_HEADS).astype(BF16)
        vts.append(akv[rows].T[A_HEAD_DIM:])
        iwt_out[0, j] = misc[rows].T[MISC_IW:MISC_IW + IDX_HEADS]
    vt_out[0, 0] = jnp.concatenate(vts, axis=1).astype(BF16)


def _dsa_prep(proj, tables, g_q, g_k):
    bsz, seq, _ = proj.shape
    qb = Q_BLOCK
    tp = min(K_CHUNK, seq)
    nq = tp // qb
    cos, s_up, s_dn = tables
    gq = jnp.tile(g_q, A_HEADS).reshape(1, A_WIDTH)
    gk = jnp.concatenate([g_k, jnp.ones((LANES - A_HEAD_DIM,), F32)]).reshape(1, LANES)
    tab_spec = pl.BlockSpec((tp, LANES), lambda b, i: (i, 0))
    hq_spec = pl.BlockSpec((1, nq, A_HEAD_DIM, A_HEADS * qb), lambda b, i: (b, i, 0, 0))
    tok_spec = pl.BlockSpec((1, tp, A_HEAD_DIM), lambda b, i: (b, i, 0))
    return pl.pallas_call(
        _dsa_prep_kernel,
        grid=(bsz, seq // tp),
        in_specs=[
            pl.BlockSpec((1, tp, A_WIDTH), lambda b, i: (b, i, COL_AQ // A_WIDTH)),
            pl.BlockSpec((1, tp, I_WIDTH), lambda b, i: (b, i, COL_IQ // I_WIDTH)),
            pl.BlockSpec((1, tp, LANES), lambda b, i: (b, i, COL_AKV // LANES)),
            pl.BlockSpec((1, tp, LANES), lambda b, i: (b, i, COL_MISC // LANES)),
            tab_spec, tab_spec, tab_spec,
            pl.BlockSpec((1, A_WIDTH), lambda b, i: (0, 0)),
            pl.BlockSpec((1, LANES), lambda b, i: (0, 0)),
        ],
        out_specs=[
            hq_spec, hq_spec, tok_spec, tok_spec,
            pl.BlockSpec((1, 1, A_HEAD_DIM, tp), lambda b, i: (b, i, 0, 0)),
            pl.BlockSpec((1, nq, IDX_HEADS, qb), lambda b, i: (b, i, 0, 0)),
        ],
        out_shape=[
            jax.ShapeDtypeStruct((bsz, seq // qb, A_HEAD_DIM, A_HEADS * qb), BF16),
            jax.ShapeDtypeStruct((bsz, seq // qb, IDX_DIM, IDX_HEADS * qb), BF16),
            jax.ShapeDtypeStruct((bsz, seq, A_HEAD_DIM), BF16),
            jax.ShapeDtypeStruct((bsz, seq, IDX_DIM), BF16),
            jax.ShapeDtypeStruct((bsz, seq // tp, A_HEAD_DIM, tp), BF16),
            jax.ShapeDtypeStruct((bsz, seq // qb, IDX_HEADS, qb), F32),
        ],
        compiler_params=_cparams(("parallel", "parallel")),
        name="dsa_prep",
    )(proj, proj, proj, proj, cos, s_up, s_dn, gq, gk)


def _rope_tables(seq):
    half = ROPE_DIM // 2
    inv_freq = ROPE_THETA ** (-jnp.arange(0, ROPE_DIM, 2, dtype=F32) / ROPE_DIM)
    ang = jnp.arange(seq, dtype=F32)[:, None] * inv_freq[None, :]
    cos, sin = jnp.cos(ang), jnp.sin(ang)
    pad = A_HEAD_DIM - ROPE_DIM
    cos64 = jnp.concatenate([cos, cos, jnp.ones((seq, pad), F32)], axis=1)
    up64 = jnp.concatenate([jnp.zeros((seq, half), F32), sin, jnp.zeros((seq, pad), F32)], axis=1)
    dn64 = jnp.concatenate([-sin, jnp.zeros((seq, half + pad), F32)], axis=1)
    rep = LANES // A_HEAD_DIM
    return jnp.tile(cos64, (1, rep)), jnp.tile(up64, (1, rep)), jnp.tile(dn64, (1, rep))


def _dsa_kernel(qt_ref, iqt_ref, iwt_ref, k_ref, vt_ref, ik_ref, got_ref, o_ref,
                key_scr, bias_scr, m_scr, l_scr, acc_scr, kn_scr, *, topk):
    qb, kcs = Q_BLOCK, key_scr.shape[1]
    blk = pl.program_id(1)
    n_chunks = (blk * qb) // kcs + 1
    krow = lax.broadcasted_iota(jnp.int32, (kcs, qb), 0)
    qcol = lax.broadcasted_iota(jnp.int32, (kcs, qb), 1)
    qpos = blk * qb + qcol
    iw = iwt_ref[0, 0] * (IDX_HEADS ** -0.5)

    def key_rows(kc):
        return pl.ds(pl.multiple_of(kc * kcs, kcs), kcs)

    def score_chunk(kc, carry):
        ks = ik_ref[0, key_rows(kc), :]
        logits = _dot(ks, iqt_ref[0, 0])
        acc = jnp.zeros((kcs, qb), F32)
        for h in range(IDX_HEADS):
            acc = acc + jnp.maximum(logits[:, h * qb:(h + 1) * qb], 0.0) * iw[h:h + 1, :]
        acc = jnp.where(acc == 0.0, 0.0, acc)
        sc = jnp.where(kc * kcs + krow <= qpos, acc, -jnp.inf)
        bits = lax.bitcast_convert_type(sc, jnp.int32)
        key_scr[kc] = jnp.where(bits >= 0, bits, bits ^ 0x7FFFFFFF)
        return carry

    lax.fori_loop(0, n_chunks, score_chunk, 0)

    fold = min(COUNT_ROWS, kcs)

    def count(pred_fn):
        def body(kc, c):
            hit = jnp.where(pred_fn(key_scr[kc]), 1.0, 0.0)
            return c + jnp.sum(hit.reshape(kcs // fold, fold, qb), axis=0)
        c = lax.fori_loop(0, n_chunks, body, jnp.zeros((fold, qb), F32))
        return jnp.sum(c, axis=0, keepdims=True)

    kf = float(topk)
    zero = jnp.zeros((1, qb), jnp.int32)
    n_all = float(kcs) * n_chunks.astype(F32)
    c0 = count(lambda k: k >= zero)
    thr0 = jnp.where(c0 >= kf, zero, jnp.full((1, qb), INT_MIN, jnp.int32))
    n_ge0 = jnp.where(c0 >= kf, c0, jnp.zeros((1, qb), F32) + n_all)

    def bit_step(i, carry):
        thr, n_ge = carry
        cand = thr | jnp.left_shift(jnp.int32(1), 30 - i)
        c = count(lambda k: k >= cand)
        return jnp.where(c >= kf, cand, thr), jnp.where(c >= kf, c, n_ge)

    thr, n_ge = lax.fori_loop(0, 31, bit_step, (thr0, n_ge0))

    tie_overflow = jnp.max(n_ge) > kf

    @pl.when(blk == 0)
    def _():
        kf32 = k_ref[0].astype(F32)
        kn_scr[...] = jnp.full(kn_scr.shape, jnp.max(jnp.sum(kf32 * kf32, axis=-1, keepdims=True)))

    qf32 = qt_ref[0, 0].astype(F32)
    logit_bound_sq = jnp.max(jnp.sum(qf32 * qf32, axis=0, keepdims=True)) * kn_scr[0, 0]
    unshifted = logit_bound_sq < SAFE_LOGIT * SAFE_LOGIT
    sel_bias = jnp.where(unshifted, -jnp.sqrt(jnp.zeros((1, qb), F32) + logit_bound_sq), 0.0)

    @pl.when(jnp.logical_not(tie_overflow))
    def _():
        def select_chunk(kc, carry):
            sel = (key_scr[kc] >= thr) & (kc * kcs + krow <= qpos)
            bias_scr[kc] = jnp.where(sel, sel_bias, NEG_BIG)
            return carry

        lax.fori_loop(0, n_chunks, select_chunk, 0)

    @pl.when(tie_overflow)
    def _():
        need = kf - count(lambda k: k > thr)
        lower = (lax.broadcasted_iota(jnp.int32, (kcs, kcs), 1)
                 <= lax.broadcasted_iota(jnp.int32, (kcs, kcs), 0)).astype(BF16)

        def select_chunk(kc, carry):
            key = key_scr[kc]
            eq = key == thr
            eqf = jnp.where(eq, 1.0, 0.0)
            incl = _dot(lower, eqf.astype(BF16))
            sel = (key > thr) | (eq & (carry + incl - eqf < need))
            sel = sel & (kc * kcs + krow <= qpos)
            bias_scr[kc] = jnp.where(sel, sel_bias, NEG_BIG)
            return carry + incl[kcs - 1:kcs, :]

        lax.fori_loop(0, n_chunks, select_chunk, jnp.zeros((1, qb), F32))

    l_scr[...] = jnp.zeros(l_scr.shape, F32)
    acc_scr[...] = jnp.zeros(acc_scr.shape, F32)

    @pl.when(unshifted)
    def _():
        def attend_chunk(kc, carry):
            kk = k_ref[0, key_rows(kc), :]
            vt = vt_ref[0, kc]
            bias = bias_scr[kc]
            s = _dot(kk, qt_ref[0, 0])
            p = jnp.concatenate([jnp.exp(s[:, h * qb:(h + 1) * qb] + bias) for h in range(A_HEADS)],
                                axis=1)
            l_scr[...] += jnp.sum(p.reshape(kcs // SUBLANES, SUBLANES, A_HEADS * qb), axis=0)
            acc_scr[...] += _dot(vt, p.astype(BF16))
            return carry

        lax.fori_loop(0, n_chunks, attend_chunk, 0)

    @pl.when(jnp.logical_not(unshifted))
    def _():
        m_scr[...] = jnp.full(m_scr.shape, NEG_BIG, F32)

        def attend_chunk(kc, carry):
            kk = k_ref[0, key_rows(kc), :]
            vt = vt_ref[0, kc]
            bias = bias_scr[kc]
            for h in range(A_HEADS):
                cols = slice(h * qb, (h + 1) * qb)
                s = _dot(kk, qt_ref[0, 0, :, cols]) + bias
                m_prev = m_scr[0:1, cols]
                m_new = jnp.maximum(m_prev, jnp.max(s, axis=0, keepdims=True))
                p = jnp.exp(s - m_new)
                alpha = jnp.exp(m_prev - m_new)
                l_scr[0:1, cols] = alpha * l_scr[0:1, cols] + jnp.sum(p, axis=0, keepdims=True)
                acc_scr[:, cols] = alpha * acc_scr[:, cols] + _dot(vt, p.astype(BF16))
                m_scr[0:1, cols] = m_new
            return carry

        lax.fori_loop(0, n_chunks, attend_chunk, 0)

    o = acc_scr[...] / jnp.sum(l_scr[...], axis=0, keepdims=True)
    o = o * lax.rsqrt(jnp.mean(o * o, axis=0, keepdims=True) + EPS) * got_ref[...]
    pairs = [jnp.concatenate([o[:, 2 * p * qb:(2 * p + 1) * qb], o[:, (2 * p + 1) * qb:(2 * p + 2) * qb]],
                             axis=0).T for p in range(A_HEADS // 2)]
    o_ref[0] = jnp.concatenate(pairs, axis=1).astype(BF16)


def _dsa(qt, iqt, iwt, k_r, vt, ik_r, g_out_a):
    bsz, seq, _ = k_r.shape
    qb = Q_BLOCK
    topk = min(TOPK_MAX, seq // 4)
    n_blk = seq // qb
    lanes = A_HEADS * qb
    n_kc, kcs = vt.shape[1], vt.shape[3]
    got = jnp.repeat(g_out_a.reshape(A_HEADS, A_HEAD_DIM).T, qb, axis=1)
    k_spec = pl.BlockSpec((1, seq, A_HEAD_DIM), lambda b, i: (b, 0, 0))
    q_spec = pl.BlockSpec((1, 1, A_HEAD_DIM, lanes), lambda b, i: (b, i, 0, 0))
    return pl.pallas_call(
        functools.partial(_dsa_kernel, topk=topk),
        grid=(bsz, n_blk),
        in_specs=[
            q_spec, q_spec,
            pl.BlockSpec((1, 1, IDX_HEADS, qb), lambda b, i: (b, i, 0, 0)),
            k_spec,
            pl.BlockSpec((1, n_kc, A_HEAD_DIM, kcs), lambda b, i: (b, 0, 0, 0)),
            k_spec,
            pl.BlockSpec((A_HEAD_DIM, lanes), lambda b, i: (0, 0)),
        ],
        out_specs=pl.BlockSpec((1, qb, A_WIDTH), lambda b, i: (b, i, 0)),
        out_shape=jax.ShapeDtypeStruct((bsz, seq, A_WIDTH), BF16),
        scratch_shapes=[
            pltpu.VMEM((n_kc, kcs, qb), jnp.int32),
            pltpu.VMEM((n_kc, kcs, qb), F32),
            pltpu.VMEM((SUBLANES, lanes), F32),
            pltpu.VMEM((SUBLANES, lanes), F32),
            pltpu.VMEM((A_HEAD_DIM, lanes), F32),
            pltpu.VMEM((SUBLANES, LANES), F32),
        ],
        compiler_params=_cparams(("parallel", "arbitrary")),
        name="dsa_attention",
    )(qt, iqt, iwt, k_r, vt, ik_r, got)


def _sgu_kernel(bu_ref, bv_ref, gv_ref, ws_ref, bst_ref, go_ref, o_ref):
    ch = B_CHUNK
    u = jax.nn.gelu(bu_ref[0])
    v = jax.nn.gelu(bv_ref[0])
    vc = v - jnp.mean(v, axis=-1, keepdims=True)
    vn = vc * lax.rsqrt(jnp.mean(vc * vc, axis=-1, keepdims=True) + EPS) * gv_ref[...]
    vb = vn.astype(BF16)
    row = lax.broadcasted_iota(jnp.int32, (ch, ch), 0)
    col = lax.broadcasted_iota(jnp.int32, (ch, ch), 1)
    causal = col <= row
    grp = lax.broadcasted_iota(jnp.int32, (1, B_WIDTH), 1) // B_GROUP_DIM
    bst = bst_ref[...]
    mixed = jnp.zeros((ch, B_WIDTH), F32)
    for g in range(B_GROUPS):
        w = jnp.where(causal, ws_ref[g], 0.0).astype(BF16)
        mixed = jnp.where(grp == g, _dot(w, vb) + bst[:, g:g + 1], mixed)
    o = u * mixed
    ss = _dot_exact_rhs(o * o, _group_ones(B_WIDTH, B_GROUP_DIM))
    o_ref[0] = (o * lax.rsqrt(ss * (1.0 / B_GROUP_DIM) + EPS) * go_ref[...]).astype(BF16)


def _sgu(proj, g_v, w_s, b_s, g_out_b):
    bsz, seq, _ = proj.shape
    ch = B_CHUNK
    return pl.pallas_call(
        _sgu_kernel,
        grid=(bsz, seq // ch),
        in_specs=[
            pl.BlockSpec((1, ch, B_WIDTH), lambda b, i: (b, i, COL_BU // B_WIDTH)),
            pl.BlockSpec((1, ch, B_WIDTH), lambda b, i: (b, i, COL_BV // B_WIDTH)),
            pl.BlockSpec((1, B_WIDTH), lambda b, i: (0, 0)),
            pl.BlockSpec((B_GROUPS, ch, ch), lambda b, i: (0, 0, 0)),
            pl.BlockSpec((ch, B_GROUPS), lambda b, i: (0, 0)),
            pl.BlockSpec((1, B_WIDTH), lambda b, i: (0, 0)),
        ],
        out_specs=pl.BlockSpec((1, ch, B_WIDTH), lambda b, i: (b, i, 0)),
        out_shape=jax.ShapeDtypeStruct((bsz, seq, B_WIDTH), BF16),
        compiler_params=_cparams(("parallel", "parallel")),
        name="spatial_gating",
    )(proj, proj, g_v.reshape(1, B_WIDTH), w_s, b_s.T, g_out_b.reshape(1, B_WIDTH))


def _log_sigmoid(z):
    return jnp.minimum(z, 0.0) - jnp.log1p(jnp.exp(-jnp.abs(z)))


def _gla_kernel(cq_ref, ck_ref, cv_ref, cr_ref, misc_ref, wa_ref, ba_ref, go_ref, o_ref, st_scr):
    tg, ch = cq_ref.shape[1], C_CHUNK

    @pl.when(pl.program_id(1) == 0)
    def _():
        st_scr[...] = jnp.zeros(st_scr.shape, F32)

    z = _dot3(misc_ref[0], wa_ref[...]) + ba_ref[...]
    log_a = _log_sigmoid(z) * (1.0 / C_GATE_TAU)
    r = lax.broadcasted_iota(jnp.int32, (tg, tg), 0)
    c = lax.broadcasted_iota(jnp.int32, (tg, tg), 1)
    same = (r // ch) == (c // ch)
    b = _dot_exact_lhs((same & (c <= r)).astype(BF16), log_a)
    b_last = _dot_exact_lhs(same.astype(BF16), log_a)
    ck = ck_ref[0]
    q_dec = cq_ref[0] * (C_KEY_DIM ** -0.5) * jnp.exp(b)
    k_neg = (ck * jnp.exp(-b)).astype(BF16)
    k_st = (ck * jnp.exp(b_last - b)).astype(BF16)
    cv = cv_ref[0].astype(BF16)

    kp, vp = 2 * C_KEY_DIM, 2 * C_VAL_DIM
    lane_head = lax.broadcasted_iota(jnp.int32, (2, 1, kp), 2) // C_KEY_DIM
    head_mask = lane_head == lax.broadcasted_iota(jnp.int32, (2, 1, kp), 0)
    tril = (lax.broadcasted_iota(jnp.int32, (1, ch, ch), 2)
            <= lax.broadcasted_iota(jnp.int32, (1, ch, ch), 1))
    pair_diag = (lax.broadcasted_iota(jnp.int32, (kp, vp), 0) // C_KEY_DIM
                 == lax.broadcasted_iota(jnp.int32, (kp, vp), 1) // C_VAL_DIM)
    ones_cols = jnp.ones((ch, LANES), BF16)

    for n in range(tg // ch):
        rs = slice(n * ch, (n + 1) * ch)
        la_hi, la_lo = _split(log_a[rs])
        decay = jnp.exp(_dot_tn(la_hi, ones_cols) + _dot_tn(la_lo, ones_cols))[:, 0:1]
        parts = []
        for p in range(C_HEADS // 2):
            kl, vl = slice(p * kp, (p + 1) * kp), slice(p * vp, (p + 1) * vp)
            qd = q_dec[rs, kl]
            vn = cv[rs, vl]
            qm = jnp.where(head_mask, qd[None], 0.0).reshape(2 * ch, kp).astype(BF16)
            att = _dot_nt(qm, k_neg[rs, kl]).reshape(2, ch, ch)
            att = jnp.where(tril, att, 0.0).astype(BF16)
            o_intra = jnp.concatenate([_dot(att[0], vn[:, :C_VAL_DIM]), _dot(att[1], vn[:, C_VAL_DIM:])],
                                      axis=1)
            state = st_scr[p]
            o = o_intra + _dot(qd.astype(BF16), state.astype(BF16))
            st_scr[p] = jnp.where(pair_diag, decay[kl] * state + _dot_tn(k_st[rs, kl], vn), 0.0)
            for oh in (o[:, :C_VAL_DIM], o[:, C_VAL_DIM:]):
                parts.append(oh * lax.rsqrt(jnp.mean(oh * oh, axis=-1, keepdims=True) + EPS))
        cr = cr_ref[0, rs, :]
        o_ref[0, rs, :] = (jnp.concatenate(parts, axis=1) * (cr * jax.nn.sigmoid(cr))
                           * go_ref[...]).astype(BF16)


def _gla(proj, w_a2, b_a, g_out_c):
    bsz, seq, _ = proj.shape
    tg = min(T_GLA, seq)
    wa = jnp.zeros((LANES, C_KWIDTH), F32).at[MISC_CA:MISC_CA + C_GATE_RANK].set(w_a2)
    return pl.pallas_call(
        _gla_kernel,
        grid=(bsz, seq // tg),
        in_specs=[
            pl.BlockSpec((1, tg, C_KWIDTH), lambda b, i: (b, i, COL_CQ // C_KWIDTH)),
            pl.BlockSpec((1, tg, C_KWIDTH), lambda b, i: (b, i, COL_CK // C_KWIDTH)),
            pl.BlockSpec((1, tg, C_WIDTH), lambda b, i: (b, i, COL_CV // C_WIDTH)),
            pl.BlockSpec((1, tg, C_WIDTH), lambda b, i: (b, i, COL_CR // C_WIDTH)),
            pl.BlockSpec((1, tg, LANES), lambda b, i: (b, i, COL_MISC // LANES)),
            pl.BlockSpec((LANES, C_KWIDTH), lambda b, i: (0, 0)),
            pl.BlockSpec((1, C_KWIDTH), lambda b, i: (0, 0)),
            pl.BlockSpec((1, C_WIDTH), lambda b, i: (0, 0)),
        ],
        out_specs=pl.BlockSpec((1, tg, C_WIDTH), lambda b, i: (b, i, 0)),
        out_shape=jax.ShapeDtypeStruct((bsz, seq, C_WIDTH), BF16),
        scratch_shapes=[pltpu.VMEM((C_HEADS // 2, 2 * C_KEY_DIM, 2 * C_VAL_DIM), F32)],
        compiler_params=_cparams(("parallel", "arbitrary")),
        name="gla",
    )(proj, proj, proj, proj, proj, wa, b_a.reshape(1, C_KWIDTH), g_out_c.reshape(1, C_WIDTH))


def _out_proj_kernel(oa_ref, ob_ref, oc_ref, w_ref, x_ref, mod_ref, o_ref):
    y = _dot(oa_ref[0], w_ref[0:A_WIDTH, :])
    y = y + _dot(ob_ref[0], w_ref[A_WIDTH:A_WIDTH + B_WIDTH, :])
    y = y + _dot(oc_ref[0], w_ref[A_WIDTH + B_WIDTH:, :])
    o_ref[0] = x_ref[0] + mod_ref[0, 2:3, :] * y


def _out_proj(o_a, o_b, o_c, w_out_bf16, x, mod_l):
    bsz, seq, d = x.shape
    tm = min(TM_MM, seq)
    dm = w_out_bf16.shape[0]
    return pl.pallas_call(
        _out_proj_kernel,
        grid=(bsz, seq // tm),
        in_specs=[
            pl.BlockSpec((1, tm, A_WIDTH), lambda b, i: (b, i, 0)),
            pl.BlockSpec((1, tm, B_WIDTH), lambda b, i: (b, i, 0)),
            pl.BlockSpec((1, tm, C_WIDTH), lambda b, i: (b, i, 0)),
            pl.BlockSpec((dm, d), lambda b, i: (0, 0)),
            pl.BlockSpec((1, tm, d), lambda b, i: (b, i, 0)),
            pl.BlockSpec((1, N_ADA, d), lambda b, i: (b, 0, 0)),
        ],
        out_specs=pl.BlockSpec((1, tm, d), lambda b, i: (b, i, 0)),
        out_shape=jax.ShapeDtypeStruct((bsz, seq, d), F32),
        compiler_params=_cparams(("parallel", "parallel")),
        name="out_proj",
    )(o_a, o_b, o_c, w_out_bf16, x, mod_l)


def _swiglu_step(h, wg_ref, wu_ref, wd_ref, acc_scr, f, n_f, f_tail):
    def step(width):
        gate = _dot(h, wg_ref[:, :width])
        a = gate * jax.nn.sigmoid(gate) * _dot(h, wu_ref[:, :width])
        acc_scr[...] += _dot(a.astype(BF16), wd_ref[:width, :])

    tf = wg_ref.shape[1]
    if f_tail == tf:
        step(tf)
        return

    @pl.when(f < n_f - 1)
    def _():
        step(tf)

    @pl.when(f == n_f - 1)
    def _():
        step(f_tail)


def _hidden_tiles(f_width):
    n_f = pl.cdiv(f_width, TF_FFN)
    return n_f, f_width - (n_f - 1) * TF_FFN


def _ffn_kernel(x_ref, mod_ref, g_ref, wg_ref, wu_ref, wd_ref, o_ref, h_scr, acc_scr, *, n_f, f_tail):
    f = pl.program_id(2)

    @pl.when(f == 0)
    def _():
        h = _norm_mod(x_ref[0], g_ref[...], mod_ref[0, 4:5, :], mod_ref[0, 3:4, :])
        h_scr[...] = h.astype(BF16)
        acc_scr[...] = jnp.zeros(acc_scr.shape, F32)

    _swiglu_step(h_scr[...], wg_ref, wu_ref, wd_ref, acc_scr, f, n_f, f_tail)

    @pl.when(f == n_f - 1)
    def _():
        o_ref[0] = x_ref[0] + mod_ref[0, 5:6, :] * acc_scr[...]


def _ffn(x, mod_l, g, w_gate, w_up, w_down):
    bsz, seq, d = x.shape
    tm, tf = min(TM_MM, seq), TF_FFN
    n_f, f_tail = _hidden_tiles(w_gate.shape[-1])
    row = lambda b, i, f: (b, i, 0)
    return pl.pallas_call(
        functools.partial(_ffn_kernel, n_f=n_f, f_tail=f_tail),
        grid=(bsz, seq // tm, n_f),
        in_specs=[
            pl.BlockSpec((1, tm, d), row),
            pl.BlockSpec((1, N_ADA, d), lambda b, i, f: (b, 0, 0)),
            pl.BlockSpec((1, d), lambda b, i, f: (0, 0)),
            pl.BlockSpec((d, tf), lambda b, i, f: (0, f)),
            pl.BlockSpec((d, tf), lambda b, i, f: (0, f)),
            pl.BlockSpec((tf, d), lambda b, i, f: (f, 0)),
        ],
        out_specs=pl.BlockSpec((1, tm, d), row),
        out_shape=jax.ShapeDtypeStruct((bsz, seq, d), F32),
        scratch_shapes=[pltpu.VMEM((tm, d), BF16), pltpu.VMEM((tm, d), F32)],
        compiler_params=_cparams(("parallel", "parallel", "arbitrary")),
        name="dense_ffn",
    )(x, mod_l, g.reshape(1, d), w_gate, w_up, w_down)


def _router_kernel(x_ref, mod_ref, g_ref, wr_ref, h_ref, info_ref, cnt_ref, run_scr):
    @pl.when((pl.program_id(0) == 0) & (pl.program_id(1) == 0))
    def _():
        run_scr[...] = jnp.zeros(run_scr.shape, F32)

    h = _norm_mod(x_ref[0], g_ref[...], mod_ref[0, 4:5, :], mod_ref[0, 3:4, :])
    h_ref[0] = h
    logits = _dot3(h, wr_ref[...])
    tm = logits.shape[0]
    lane = lax.broadcasted_iota(jnp.int32, logits.shape, 1).astype(F32)
    logits = jnp.where(lane < N_EXPERTS, logits, -jnp.inf)
    m1 = jnp.max(logits, axis=-1, keepdims=True)
    i1 = jnp.min(jnp.where(logits == m1, lane, float(LANES)), axis=-1, keepdims=True)
    rest = jnp.where(lane == i1, -jnp.inf, logits)
    m2 = jnp.max(rest, axis=-1, keepdims=True)
    i2 = jnp.min(jnp.where(rest == m2, lane, float(LANES)), axis=-1, keepdims=True)
    e2 = jnp.exp(m2 - m1)
    den = 1.0 + e2
    hit1, hit2 = lane == i1, lane == i2
    hits = jnp.where(hit1 | hit2, 1.0, 0.0)
    earlier = (lax.broadcasted_iota(jnp.int32, (tm, tm), 1)
               < lax.broadcasted_iota(jnp.int32, (tm, tm), 0)).astype(BF16)
    rank = run_scr[0:1, :] + _dot(earlier, hits.astype(BF16))
    r1 = jnp.sum(jnp.where(hit1, rank, 0.0), axis=-1, keepdims=True)
    r2 = jnp.sum(jnp.where(hit2, rank, 0.0), axis=-1, keepdims=True)
    run_scr[0:1, :] = run_scr[0:1, :] + jnp.sum(hits, axis=0, keepdims=True)
    info = jnp.zeros(logits.shape, F32)
    for k, val in ((INFO_E1, i1), (INFO_E2, i2), (INFO_R1, r1), (INFO_R2, r2),
                   (INFO_W1, 1.0 / den), (INFO_W2, e2 / den)):
        info = jnp.where(lane == float(k), val, info)
    info_ref[0] = info
    cnt_ref[...] = jnp.broadcast_to(run_scr[0:1, :], cnt_ref.shape)


def _router(x, mod_l, g, w_router):
    bsz, seq, d = x.shape
    tm = min(TM_MM, seq)
    wr = jnp.pad(w_router, ((0, 0), (0, LANES - N_EXPERTS)))
    return pl.pallas_call(
        _router_kernel,
        grid=(bsz, seq // tm),
        in_specs=[
            pl.BlockSpec((1, tm, d), lambda b, i: (b, i, 0)),
            pl.BlockSpec((1, N_ADA, d), lambda b, i: (b, 0, 0)),
            pl.BlockSpec((1, d), lambda b, i: (0, 0)),
            pl.BlockSpec((d, LANES), lambda b, i: (0, 0)),
        ],
        out_specs=[
            pl.BlockSpec((1, tm, d), lambda b, i: (b, i, 0)),
            pl.BlockSpec((1, tm, LANES), lambda b, i: (b, i, 0)),
            pl.BlockSpec((SUBLANES, LANES), lambda b, i: (0, 0)),
        ],
        out_shape=[
            jax.ShapeDtypeStruct((bsz, seq, d), F32),
            jax.ShapeDtypeStruct((bsz, seq, LANES), F32),
            jax.ShapeDtypeStruct((SUBLANES, LANES), F32),
        ],
        scratch_shapes=[pltpu.VMEM((SUBLANES, LANES), F32)],
        compiler_params=_cparams(("arbitrary", "arbitrary")),
        name="router",
    )(x, mod_l, g.reshape(1, d), wr)


def _row_copy(src, dst, sem):
    return pltpu.make_async_copy(src, dst, sem)


def _dispatch_kernel(p1_ref, p2_ref, pad0_ref, padn_ref, h_ref, out_hbm, sem, zrow):
    g = h_ref.shape[0]
    base = pl.program_id(0) * g

    @pl.when(pl.program_id(0) == 0)
    def _():
        zrow[...] = jnp.zeros(zrow.shape, F32)
        zero_row = zrow.at[pl.ds(0, 1)]
        n_pad = 0
        for e in range(N_EXPERTS + 1):
            def zero_issue(r, carry, e=e):
                _row_copy(zero_row, out_hbm.at[pl.ds(pad0_ref[e] + r, 1)], sem).start()
                return carry

            lax.fori_loop(0, padn_ref[e], zero_issue, 0)
            n_pad = n_pad + padn_ref[e]

        def zero_drain(r, carry):
            _row_copy(zero_row, out_hbm.at[pl.ds(0, 1)], sem).wait()
            return carry

        lax.fori_loop(0, n_pad, zero_drain, 0)

    def issue(r, carry):
        row = h_ref.at[pl.ds(r, 1)]
        _row_copy(row, out_hbm.at[pl.ds(p1_ref[base + r], 1)], sem).start(priority=0)
        _row_copy(row, out_hbm.at[pl.ds(p2_ref[base + r], 1)], sem).start(priority=1)
        return carry

    lax.fori_loop(0, g, issue, 0, unroll=DMA_UNROLL)

    def drain(r, carry):
        _row_copy(h_ref.at[pl.ds(0, 1)], out_hbm.at[pl.ds(0, 1)], sem).wait()
        return carry

    lax.fori_loop(0, 2 * g, drain, 0, unroll=DMA_UNROLL)


def _dispatch(h2d, pos1, pos2, pad_start, pad_len, n_rows):
    n, d = h2d.shape
    g = min(G_ROWS, n)
    grid_spec = pltpu.PrefetchScalarGridSpec(
        num_scalar_prefetch=4,
        grid=(n // g,),
        in_specs=[pl.BlockSpec((g, d), lambda i, p1, p2, s0, sn: (i, 0))],
        out_specs=pl.BlockSpec(memory_space=pl.ANY),
        scratch_shapes=[pltpu.SemaphoreType.DMA, pltpu.VMEM((SUBLANES, d), F32)],
    )
    return pl.pallas_call(
        _dispatch_kernel,
        grid_spec=grid_spec,
        out_shape=jax.ShapeDtypeStruct((n_rows, d), F32),
        compiler_params=_cparams(("arbitrary",)),
        name="moe_dispatch",
    )(pos1, pos2, pad_start, pad_len, h2d)


def _moe_ffn_kernel(te_ref, nu_ref, hs_ref, wg_ref, wu_ref, wd_ref, o_ref, h_scr, acc_scr, *, n_f,
                    f_tail):
    del te_ref
    j, f = pl.program_id(0), pl.program_id(1)
    used = j < nu_ref[0]

    @pl.when(used & (f == 0))
    def _():
        h_scr[...] = hs_ref[...].astype(BF16)
        acc_scr[...] = jnp.zeros(acc_scr.shape, F32)

    @pl.when(used)
    def _():
        _swiglu_step(h_scr[...], wg_ref, wu_ref, wd_ref, acc_scr, f, n_f, f_tail)

    @pl.when(used & (f == n_f - 1))
    def _():
        o_ref[...] = acc_scr[...]

    @pl.when(jnp.logical_not(used) & (f == n_f - 1))
    def _():
        o_ref[...] = jnp.zeros(o_ref.shape, F32)


def _moe_ffn(h_sorted, tile_expert, n_used, w_gate, w_up, w_down):
    n_rows, d = h_sorted.shape
    tm, tf = TM_MOE, TF_FFN
    n_f, f_tail = _hidden_tiles(w_gate.shape[-1])
    n_tiles = n_rows // tm

    def live(j, nu):
        return jnp.minimum(j, nu[0] - 1)

    def fcol(j, f, nu):
        return jnp.where(j < nu[0], f, n_f - 1)

    grid_spec = pltpu.PrefetchScalarGridSpec(
        num_scalar_prefetch=2,
        grid=(n_tiles, n_f),
        in_specs=[
            pl.BlockSpec((tm, d), lambda j, f, te, nu: (live(j, nu), 0)),
            pl.BlockSpec((None, d, tf), lambda j, f, te, nu: (te[live(j, nu)], 0, fcol(j, f, nu))),
            pl.BlockSpec((None, d, tf), lambda j, f, te, nu: (te[live(j, nu)], 0, fcol(j, f, nu))),
            pl.BlockSpec((None, tf, d), lambda j, f, te, nu: (te[live(j, nu)], fcol(j, f, nu), 0)),
        ],
        out_specs=pl.BlockSpec((tm, d), lambda j, f, te, nu: (j, 0)),
        scratch_shapes=[pltpu.VMEM((tm, d), BF16), pltpu.VMEM((tm, d), F32)],
    )
    return pl.pallas_call(
        functools.partial(_moe_ffn_kernel, n_f=n_f, f_tail=f_tail),
        grid_spec=grid_spec,
        out_shape=jax.ShapeDtypeStruct((n_rows, d), F32),
        compiler_params=_cparams(("arbitrary", "arbitrary")),
        name="moe_ffn",
    )(tile_expert, n_used, h_sorted, w_gate, w_up, w_down)


def _combine_kernel(p1_ref, p2_ref, x_ref, mod_ref, info_ref, y_hbm, o_ref, buf, sem):
    g = x_ref.shape[1]
    base = (pl.program_id(0) * pl.num_programs(1) + pl.program_id(1)) * g

    def issue(r, carry):
        _row_copy(y_hbm.at[pl.ds(p1_ref[base + r], 1)], buf.at[0, pl.ds(r, 1)], sem).start(priority=0)
        _row_copy(y_hbm.at[pl.ds(p2_ref[base + r], 1)], buf.at[1, pl.ds(r, 1)], sem).start(priority=1)
        return carry

    lax.fori_loop(0, g, issue, 0, unroll=DMA_UNROLL)

    def drain(r, carry):
        _row_copy(y_hbm.at[pl.ds(0, 1)], buf.at[0, pl.ds(0, 1)], sem).wait()
        return carry

    lax.fori_loop(0, 2 * g, drain, 0, unroll=DMA_UNROLL)
    info = info_ref[0]
    y = info[:, INFO_W1:INFO_W1 + 1] * buf[0] + info[:, INFO_W2:INFO_W2 + 1] * buf[1]
    o_ref[0] = x_ref[0] + mod_ref[0, 5:6, :] * y


def _combine(x, mod_l, info, y_sorted, pos1, pos2):
    bsz, seq, d = x.shape
    g = min(G_ROWS, seq)
    grid_spec = pltpu.PrefetchScalarGridSpec(
        num_scalar_prefetch=2,
        grid=(bsz, seq // g),
        in_specs=[
            pl.BlockSpec((1, g, d), lambda b, i, p1, p2: (b, i, 0)),
            pl.BlockSpec((1, N_ADA, d), lambda b, i, p1, p2: (b, 0, 0)),
            pl.BlockSpec((1, g, LANES), lambda b, i, p1, p2: (b, i, 0)),
            pl.BlockSpec(memory_space=pl.ANY),
        ],
        out_specs=pl.BlockSpec((1, g, d), lambda b, i, p1, p2: (b, i, 0)),
        scratch_shapes=[pltpu.VMEM((2, g, d), F32), pltpu.SemaphoreType.DMA],
    )
    return pl.pallas_call(
        _combine_kernel,
        grid_spec=grid_spec,
        out_shape=jax.ShapeDtypeStruct((bsz, seq, d), F32),
        compiler_params=_cparams(("arbitrary", "arbitrary")),
        name="moe_combine",
    )(pos1, pos2, x, mod_l, info, y_sorted)


def _moe(x, mod_l, g, w_router, first_expert, w_gate, w_up, w_down):
    bsz, seq, d = x.shape
    n = bsz * seq
    tm = TM_MOE
    h, info, counts = _router(x, mod_l, g, w_router)
    cnt = counts[0, :N_EXPERTS].astype(jnp.int32)
    padded = (cnt + tm - 1) // tm * tm
    ends = jnp.cumsum(padded)
    starts = ends - padded
    n_rows = 2 * n + N_EXPERTS * tm
    tile_start = jnp.arange(n_rows // tm, dtype=jnp.int32) * tm
    tile_expert = jnp.minimum(jnp.sum(tile_start[:, None] >= ends[None, :], axis=1), N_EXPERTS - 1)
    n_used = (ends[-1] // tm).reshape(1).astype(jnp.int32)
    rec = info.reshape(n, LANES)
    e1, e2 = rec[:, INFO_E1].astype(jnp.int32), rec[:, INFO_E2].astype(jnp.int32)
    experts = jnp.arange(N_EXPERTS, dtype=jnp.int32)[None, :]
    pos1 = jnp.sum(jnp.where(e1[:, None] == experts, starts[None, :], 0), axis=1) + rec[:, INFO_R1].astype(jnp.int32)
    pos2 = jnp.sum(jnp.where(e2[:, None] == experts, starts[None, :], 0), axis=1) + rec[:, INFO_R2].astype(jnp.int32)
    pad_start = jnp.concatenate([starts + cnt, ends[-1:]]).astype(jnp.int32)
    pad_len = jnp.concatenate([padded - cnt, n_rows - ends[-1:]]).astype(jnp.int32)
    h_sorted = _dispatch(h.reshape(n, d), pos1, pos2, pad_start, pad_len, n_rows)
    y_sorted = _moe_ffn(h_sorted, (tile_expert + first_expert).astype(jnp.int32), n_used,
                        w_gate, w_up, w_down)
    return _combine(x, mod_l, info, y_sorted, pos1, pos2)


def _reorder_w_in(w):
    sizes = (A_WIDTH, A_HEAD_DIM, A_HEAD_DIM, I_WIDTH, IDX_DIM, IDX_HEADS, B_WIDTH, B_WIDTH,
             C_KWIDTH, C_KWIDTH, C_WIDTH, C_GATE_RANK, C_WIDTH)
    offs = [0]
    for s in sizes:
        offs.append(offs[-1] + s)
    aq, ak, av, iq, ik, iw, bu, bv, cq, ck, cv, ca, cr = (
        w[:, offs[i]:offs[i + 1]] for i in range(len(sizes)))
    pad = jnp.zeros((w.shape[0], LANES - IDX_DIM - IDX_HEADS - C_GATE_RANK), w.dtype)
    return jnp.concatenate([aq, iq, bu, bv, cq, ck, cv, cr, ak, av, ik, iw, ca, pad],
                           axis=1).astype(BF16)


def _stack_experts(w):
    return w.astype(BF16).reshape((w.shape[0] * w.shape[1],) + w.shape[2:])


def kernel(x, c, w_ada, b_ada, g_norm1, g_norm2, w_in, g_q, g_k, g_v_b, w_s, b_s, w_a2, b_a,
           g_out, w_out, w_ff_gate, w_ff_up, w_ff_down, w_router, w_e_gate, w_e_up, w_e_down):
    depth = w_in.shape[0]
    mod = _ada_mod(c, w_ada, b_ada)
    tables = _rope_tables(x.shape[1])
    expert_w = tuple(_stack_experts(w) for w in (w_e_gate, w_e_up, w_e_down))
    for layer in range(depth):
        mod_l = mod[layer]
        proj = _in_proj(x, mod_l, g_norm1[layer], _reorder_w_in(w_in[layer]))
        qt, iqt, k_r, ik_r, vt, iwt = _dsa_prep(proj, tables, g_q[layer], g_k[layer])
        o_a = _dsa(qt, iqt, iwt, k_r, vt, ik_r, g_out[layer, :A_WIDTH])
        o_b = _sgu(proj, g_v_b[layer], w_s[layer], b_s[layer], g_out[layer, A_WIDTH:A_WIDTH + B_WIDTH])
        o_c = _gla(proj, w_a2[layer], b_a[layer], g_out[layer, A_WIDTH + B_WIDTH:])
        x = _out_proj(o_a, o_b, o_c, w_out[layer].astype(BF16), x, mod_l)
        j = layer // 2
        if layer % 2 == 0:
            x = _ffn(x, mod_l, g_norm2[layer], w_ff_gate[j].astype(BF16), w_ff_up[j].astype(BF16),
                     w_ff_down[j].astype(BF16))
        else:
            x = _moe(x, mod_l, g_norm2[layer], w_router[j], j * N_EXPERTS, *expert_w)
    return x
```
